```python
import jax, jax.numpy as jnp
from jax import lax
import numpy as np

D_MODEL = 1024
BATCH = 8
SEQ = 2048
DEPTH = 1

GRID_W = 64
HEAD_DIM = 64
N_Q_HEADS = 8
N_KV_HEADS = 2
Q_PER_KV = N_Q_HEADS // N_KV_HEADS
ATTN_WIDTH = N_Q_HEADS * HEAD_DIM
KV_WIDTH = N_KV_HEADS * HEAD_DIM
N_FOURIER_GROUPS = 4
FOURIER_GROUP_DIM = 128
FOURIER_WIDTH = N_FOURIER_GROUPS * FOURIER_GROUP_DIM
IN_WIDTH = ATTN_WIDTH + 2 * KV_WIDTH + FOURIER_WIDTH + 2 * D_MODEL
ROPE_THETA = 10000.0
ROPE_PAIRS_PER_AXIS = HEAD_DIM // 4
Q_BLOCK = 128
D_FF = 2816
CONV_WIDTH = 3
NORM_EPS = 1e-6

kernel_name = "hybrid_gqa_fnet_convffn_block"


def rms_norm(x, g):
    xf = x.astype(jnp.float32)
    y = xf * lax.rsqrt(jnp.mean(xf * xf, axis=-1, keepdims=True) + NORM_EPS)
    return (y * g.astype(jnp.float32)).astype(x.dtype)


def axial_rope_tables(seq_len):
    rows_count = seq_len // GRID_W
    row = jnp.repeat(jnp.arange(rows_count, dtype=jnp.float32), GRID_W)
    col = jnp.tile(jnp.arange(GRID_W, dtype=jnp.float32), rows_count)
    inv = ROPE_THETA ** (-jnp.arange(ROPE_PAIRS_PER_AXIS, dtype=jnp.float32) / ROPE_PAIRS_PER_AXIS)
    ang = jnp.concatenate([row[:, None] * inv, col[:, None] * inv], axis=-1)
    return jnp.cos(ang), jnp.sin(ang)


def apply_rope(x, cos, sin):
    xf = x.astype(jnp.float32).reshape(*x.shape[:-1], HEAD_DIM // 2, 2)
    x0, x1 = xf[..., 0], xf[..., 1]
    c = cos[None, :, None, :]
    s = sin[None, :, None, :]
    out = jnp.stack([x0 * c - x1 * s, x0 * s + x1 * c], axis=-1).reshape(x.shape)
    return out.astype(x.dtype)


def block_attention(q, k, v):
    B, S = q.shape[0], q.shape[1]
    nb = S // Q_BLOCK
    qb = q.reshape(B, nb, Q_BLOCK, N_KV_HEADS, Q_PER_KV, HEAD_DIM).transpose(1, 0, 2, 3, 4, 5)
    scale = HEAD_DIM ** -0.5

    def one_block(qi):
        s = jnp.einsum('bqkgd,bskd->bkgqs', qi, k, preferred_element_type=jnp.float32) * scale
        p = jax.nn.softmax(s, axis=-1).astype(v.dtype)
        return jnp.einsum('bkgqs,bskd->bqkgd', p, v)

    o = lax.map(one_block, qb)
    return o.transpose(1, 0, 2, 3, 4, 5).reshape(B, S, ATTN_WIDTH)


def fourier_mix(f):
    B, S = f.shape[0], f.shape[1]
    ff = f.astype(jnp.float32).reshape(B, S, N_FOURIER_GROUPS, FOURIER_GROUP_DIM)
    y = jnp.fft.fft2(ff, axes=(1, 3), norm="ortho").real
    return y.reshape(B, S, FOURIER_WIDTH).astype(f.dtype)


def depthwise_conv_centred(u, w, b):
    S = u.shape[1]
    half = CONV_WIDTH // 2
    up = jnp.pad(u, ((0, 0), (half, CONV_WIDTH - 1 - half), (0, 0)))
    y = up[:, 0:S] * w[0]
    for i in range(1, CONV_WIDTH):
        y = y + up[:, i:i + S] * w[i]
    return y + b


def setup_inputs(seed: int = 0) -> dict:
    key = jax.random.key(seed)
    ks = jax.random.split(key, 19)
    L = DEPTH

    def nrm(k, shape, scale):
        return jax.random.normal(k, shape, jnp.float32) * scale

    return {
        "x": nrm(ks[0], (BATCH, SEQ, D_MODEL), 1.0),
        "c": nrm(ks[1], (BATCH, D_MODEL), 1.0),
        "w_ada": nrm(ks[2], (L, D_MODEL, 6 * D_MODEL), 0.5 * D_MODEL ** -0.5),
        "b_ada": nrm(ks[3], (L, 6 * D_MODEL), 0.01),
        "mix_pre_g": 1.0 + nrm(ks[4], (L, D_MODEL), 0.02),
        "w_in": nrm(ks[5], (L, D_MODEL, IN_WIDTH), D_MODEL ** -0.5),
        "q_norm_g": 1.0 + nrm(ks[6], (L, HEAD_DIM), 0.02),
        "k_norm_g": 1.0 + nrm(ks[7], (L, HEAD_DIM), 0.02),
        "b_gate": nrm(ks[8], (L, 2 * D_MODEL), 0.01),
        "w_attn_branch": nrm(ks[9], (L, ATTN_WIDTH, D_MODEL), ATTN_WIDTH ** -0.5),
        "w_fourier_branch": nrm(ks[10], (L, FOURIER_WIDTH, D_MODEL), FOURIER_WIDTH ** -0.5),
        "w_out": nrm(ks[11], (L, D_MODEL, D_MODEL), D_MODEL ** -0.5),
        "mix_post_g": 1.0 + nrm(ks[12], (L, D_MODEL), 0.02),
        "ffn_pre_g": 1.0 + nrm(ks[13], (L, D_MODEL), 0.02),
        "w_up": nrm(ks[14], (L, D_MODEL, 2 * D_FF), D_MODEL ** -0.5),
        "conv_w": nrm(ks[15], (L, CONV_WIDTH, 2 * D_FF), CONV_WIDTH ** -0.5),
        "conv_b": nrm(ks[16], (L, 2 * D_FF), 0.01),
        "w_down": nrm(ks[17], (L, D_FF, D_MODEL), D_FF ** -0.5),
        "ffn_post_g": 1.0 + nrm(ks[18], (L, D_MODEL), 0.02),
    }


def reference(x, c, w_ada, b_ada, mix_pre_g, w_in, q_norm_g, k_norm_g, b_gate,
              w_attn_branch, w_fourier_branch, w_out, mix_post_g, ffn_pre_g,
              w_up, conv_w, conv_b, w_down, ffn_post_g):
    B, S = x.shape[0], x.shape[1]
    cos, sin = axial_rope_tables(S)
    c_act = jax.nn.silu(c)
    o1 = ATTN_WIDTH
    o2 = o1 + KV_WIDTH
    o3 = o2 + KV_WIDTH
    o4 = o3 + FOURIER_WIDTH
    o5 = o4 + D_MODEL
    for l in range(DEPTH):
        mod = (c_act @ w_ada[l] + b_ada[l])[:, None, :]
        shift1, scale1, gate1, shift2, scale2, gate2 = jnp.split(mod, 6, axis=-1)

        h = rms_norm(x, mix_pre_g[l]) * (1.0 + scale1) + shift1
        z = h @ w_in[l]
        q = z[..., :o1].reshape(B, S, N_Q_HEADS, HEAD_DIM)
        k = z[..., o1:o2].reshape(B, S, N_KV_HEADS, HEAD_DIM)
        v = z[..., o2:o3].reshape(B, S, N_KV_HEADS, HEAD_DIM)
        f = z[..., o3:o4]
        g_a = jax.nn.sigmoid(z[..., o4:o5] + b_gate[l, :D_MODEL])
        g_f = jax.nn.sigmoid(z[..., o5:] + b_gate[l, D_MODEL:])
        q = apply_rope(rms_norm(q, q_norm_g[l]), cos, sin)
        k = apply_rope(rms_norm(k, k_norm_g[l]), cos, sin)
        y_attn = block_attention(q, k, v) @ w_attn_branch[l]
        y_four = fourier_mix(f) @ w_fourier_branch[l]
        y = (g_a * y_attn + g_f * y_four) @ w_out[l]
        x = x + gate1 * rms_norm(y, mix_post_g[l])

        h = rms_norm(x, ffn_pre_g[l]) * (1.0 + scale2) + shift2
        u = depthwise_conv_centred(h @ w_up[l], conv_w[l], conv_b[l])
        a, bval = jnp.split(u, 2, axis=-1)
        y = (jax.nn.gelu(a, approximate=True) * bval) @ w_down[l]
        x = x + gate2 * rms_norm(y, ffn_post_g[l])
    return x
```

```python
import functools
import math

import numpy as np
import jax
import jax.numpy as jnp
from jax import lax
from jax.experimental import pallas as pl
from jax.experimental.pallas import tpu as pltpu

D_MODEL = 1024
GRID_W = 64
HEAD_DIM = 64
N_Q_HEADS = 8
N_KV_HEADS = 2
Q_PER_KV = N_Q_HEADS // N_KV_HEADS
ATTN_WIDTH = N_Q_HEADS * HEAD_DIM
KV_WIDTH = N_KV_HEADS * HEAD_DIM
N_FOURIER_GROUPS = 4
FOURIER_GROUP_DIM = 128
FOURIER_WIDTH = N_FOURIER_GROUPS * FOURIER_GROUP_DIM
IN_WIDTH = ATTN_WIDTH + 2 * KV_WIDTH + FOURIER_WIDTH + 2 * D_MODEL
ROPE_THETA = 10000.0
ROPE_PAIRS_PER_AXIS = HEAD_DIM // 4
D_FF = 2816
CONV_WIDTH = 3
NORM_EPS = 1e-6

LANES = 128
BF16_SUBLANES = 16
VMEM_LIMIT_BYTES = 56 * 1024 * 1024

F32 = jnp.float32
BF16 = jnp.bfloat16

_O_K = ATTN_WIDTH
_O_V = _O_K + KV_WIDTH
_O_F = _O_V + KV_WIDTH
_O_GA = _O_F + FOURIER_WIDTH
_O_GF = _O_GA + D_MODEL


def _params(n_grid_dims):
    return pltpu.CompilerParams(
        dimension_semantics=("arbitrary",) * n_grid_dims,
        vmem_limit_bytes=VMEM_LIMIT_BYTES,
    )


def _dot(a, b):
    return jnp.dot(a, b, preferred_element_type=F32)


def _rms_scale(v):
    return lax.rsqrt(jnp.mean(v * v, axis=-1, keepdims=True) + NORM_EPS)


def _adaln_kernel(c_ref, w_ref, b_ref, o_ref):
    c = c_ref[...]
    a = c / (1.0 + jnp.exp(-c))
    a_hi = a.astype(BF16)
    a_lo = (a - a_hi.astype(F32)).astype(BF16)
    w = w_ref[...]
    w_hi = w.astype(BF16)
    w_lo = (w - w_hi.astype(F32)).astype(BF16)
    o_ref[...] = _dot(a_hi, w_hi) + (_dot(a_lo, w_hi) + _dot(a_hi, w_lo)) + b_ref[...]


def _adaln(c, w_ada, b_ada):
    batch = c.shape[0]
    n = w_ada.shape[1]
    tn = 1536
    return pl.pallas_call(
        _adaln_kernel,
        grid=(n // tn,),
        in_specs=[
            pl.BlockSpec((batch, D_MODEL), lambda j: (0, 0)),
            pl.BlockSpec((D_MODEL, tn), lambda j: (0, j)),
            pl.BlockSpec((1, tn), lambda j: (0, j)),
        ],
        out_specs=pl.BlockSpec((batch, tn), lambda j: (0, j)),
        out_shape=jax.ShapeDtypeStruct((batch, n), F32),
        compiler_params=_params(1),
        name="adaln",
    )(c, w_ada, b_ada.reshape(1, n))


def _swap_pairs(v):
    n = v.shape[-1]
    lane = lax.broadcasted_iota(jnp.int32, v.shape, v.ndim - 1)
    from_right = pltpu.roll(v, n - 1, v.ndim - 1)
    from_left = pltpu.roll(v, 1, v.ndim - 1)
    return jnp.where((lane & 1) == 0, from_right, from_left)


def _head_norm_rope(z, gain, ones_blk, cos, sin):
    reps = z.shape[-1] // LANES
    ssq = _dot((z * z).astype(BF16), ones_blk)
    r = lax.rsqrt(ssq * (1.0 / HEAD_DIM) + NORM_EPS)
    zg = z * gain
    cos_t = jnp.concatenate([cos] * reps, axis=-1) if reps > 1 else cos
    sin_t = jnp.concatenate([sin] * reps, axis=-1) if reps > 1 else sin
    return (zg * cos_t + _swap_pairs(zg) * sin_t) * r


def _in_proj_kernel(x_ref, mod_ref, g_ref, w_ref, bg_ref, gq_ref, gk_ref,
                    cos_ref, sin_ref, ones_ref,
                    q_ref, kt_ref, v_ref, f_ref, ga_ref, gf_ref):
    x = x_ref[...]
    shift = mod_ref[0:1, :]
    scale = mod_ref[1:2, :]
    h = (x * _rms_scale(x)) * (g_ref[...] * (1.0 + scale)) + shift
    hb = h.astype(BF16)
    cos = cos_ref[...]
    sin = sin_ref[...]

    zq = _dot(hb, w_ref[:, 0:_O_K])
    q_ref[...] = _head_norm_rope(zq, gq_ref[...], ones_ref[...], cos, sin).astype(BF16)

    zkv = _dot(hb, w_ref[:, _O_K:_O_F])
    k = _head_norm_rope(zkv[:, 0:KV_WIDTH], gk_ref[...],
                        ones_ref[0:KV_WIDTH, 0:KV_WIDTH], cos, sin)
    kt_ref[...] = k.T.astype(BF16)
    zv = zkv[:, KV_WIDTH:2 * KV_WIDTH]
    ones_cols = jnp.ones((zv.shape[0], HEAD_DIM), F32)
    for kv in range(N_KV_HEADS):
        v_ref[kv] = jnp.concatenate(
            [zv[:, kv * HEAD_DIM:(kv + 1) * HEAD_DIM], ones_cols], axis=-1).astype(BF16)

    f_ref[...] = _dot(hb, w_ref[:, _O_F:_O_GA]).astype(BF16)

    ta = _dot(hb, w_ref[:, _O_GA:_O_GF]) + bg_ref[:, 0:D_MODEL]
    ga_ref[...] = (1.0 / (1.0 + jnp.exp(-ta))).astype(BF16)
    tf = _dot(hb, w_ref[:, _O_GF:IN_WIDTH]) + bg_ref[:, D_MODEL:2 * D_MODEL]
    gf_ref[...] = (1.0 / (1.0 + jnp.exp(-tf))).astype(BF16)


def _in_proj(x2, mod3, g_pre, w_in, b_gate, gq, gk, cos_t, sin_t, ones_blk, batch, seq, tm):
    t = batch * seq
    per_b = seq // tm
    row = lambda i: (i, 0)
    const = lambda i: (0, 0)
    return pl.pallas_call(
        _in_proj_kernel,
        grid=(t // tm,),
        in_specs=[
            pl.BlockSpec((tm, D_MODEL), row),
            pl.BlockSpec((None, 6, D_MODEL), lambda i: (i // per_b, 0, 0)),
            pl.BlockSpec((1, D_MODEL), const),
            pl.BlockSpec((D_MODEL, IN_WIDTH), const),
            pl.BlockSpec((1, 2 * D_MODEL), const),
            pl.BlockSpec((1, ATTN_WIDTH), const),
            pl.BlockSpec((1, KV_WIDTH), const),
            pl.BlockSpec((tm, LANES), lambda i: (i % per_b, 0)),
            pl.BlockSpec((tm, LANES), lambda i: (i % per_b, 0)),
            pl.BlockSpec((ATTN_WIDTH, ATTN_WIDTH), const),
        ],
        out_specs=[
            pl.BlockSpec((tm, ATTN_WIDTH), row),
            pl.BlockSpec((None, KV_WIDTH, tm), lambda i: (i // per_b, 0, i % per_b)),
            pl.BlockSpec((None, N_KV_HEADS, tm, LANES), lambda i: (i // per_b, 0, i % per_b, 0)),
            pl.BlockSpec((tm, FOURIER_WIDTH), row),
            pl.BlockSpec((tm, D_MODEL), row),
            pl.BlockSpec((tm, D_MODEL), row),
        ],
        out_shape=[
            jax.ShapeDtypeStruct((t, ATTN_WIDTH), BF16),
            jax.ShapeDtypeStruct((batch, KV_WIDTH, seq), BF16),
            jax.ShapeDtypeStruct((batch, N_KV_HEADS, seq, LANES), BF16),
            jax.ShapeDtypeStruct((t, FOURIER_WIDTH), BF16),
            jax.ShapeDtypeStruct((t, D_MODEL), BF16),
            jax.ShapeDtypeStruct((t, D_MODEL), BF16),
        ],
        compiler_params=_params(1),
        name="in_proj",
    )(x2, mod3, g_pre, w_in, b_gate, gq, gk, cos_t, sin_t, ones_blk)


def _attn_kernel(q_ref, kt_ref, v_ref, o_ref):
    tq = q_ref.shape[0]
    outs = []
    for kv in range(N_KV_HEADS):
        kt = kt_ref[kv * HEAD_DIM:(kv + 1) * HEAD_DIM, :]
        vv = v_ref[kv]
        qg = jnp.concatenate(
            [q_ref[:, (kv * Q_PER_KV + g) * HEAD_DIM:(kv * Q_PER_KV + g + 1) * HEAD_DIM]
             for g in range(Q_PER_KV)], axis=0)
        s = _dot(qg, kt)
        m = jnp.max(s, axis=-1, keepdims=True)
        p = jnp.exp(s - m).astype(BF16)
        ov = _dot(p, vv)
        on = ov / pltpu.roll(ov, HEAD_DIM, 1)
        for g in range(Q_PER_KV):
            outs.append(on[g * tq:(g + 1) * tq, 0:HEAD_DIM])
    o_ref[...] = jnp.concatenate(outs, axis=-1).astype(BF16)


def _attention(q, kt, v, batch, seq, tq):
    t = batch * seq
    per_b = seq // tq
    return pl.pallas_call(
        _attn_kernel,
        grid=(batch, per_b),
        in_specs=[
            pl.BlockSpec((tq, ATTN_WIDTH), lambda b, i: (b * per_b + i, 0)),
            pl.BlockSpec((None, KV_WIDTH, seq), lambda b, i: (b, 0, 0)),
            pl.BlockSpec((None, N_KV_HEADS, seq, LANES), lambda b, i: (b, 0, 0, 0)),
        ],
        out_specs=pl.BlockSpec((tq, ATTN_WIDTH), lambda b, i: (b * per_b + i, 0)),
        out_shape=jax.ShapeDtypeStruct((t, ATTN_WIDTH), BF16),
        compiler_params=_params(2),
        name="attn",
    )(q, kt, v)


_DFT_SPLIT = 32


def _fourier_kernel(f_ref, cs_ref, ca_ref, sa_ref, cb_ref, sb_ref, o_ref, w_ref, r_ref):
    seq = f_ref.shape[0]
    gd = FOURIER_GROUP_DIM

    @pl.when(pl.program_id(0) == 0)
    def _():
        cb = cb_ref[...]
        sb = sb_ref[...]

        def body(s1, carry):
            ca = ca_ref[pl.ds(s1, 1), :]
            sa = sa_ref[pl.ds(s1, 1), :]
            rows = pl.ds(pl.multiple_of(s1 * _DFT_SPLIT, _DFT_SPLIT), _DFT_SPLIT)
            w_ref[rows, 0:seq] = (ca * cb - sa * sb).astype(BF16)
            w_ref[rows, seq:2 * seq] = (-(sa * cb + ca * sb)).astype(BF16)
            return carry

        lax.fori_loop(0, seq // _DFT_SPLIT, body, 0)

    cs = cs_ref[...].astype(BF16)
    for g in range(N_FOURIER_GROUPS):
        xcs = _dot(f_ref[:, g * gd:(g + 1) * gd], cs)
        r_ref[0:seq, g * gd:(g + 1) * gd] = xcs[:, 0:gd].astype(BF16)
        r_ref[seq:2 * seq, g * gd:(g + 1) * gd] = xcs[:, gd:2 * gd].astype(BF16)
    tm = 512
    for mi in range(seq // tm):
        o_ref[mi * tm:(mi + 1) * tm, :] = _dot(
            w_ref[mi * tm:(mi + 1) * tm, :], r_ref[...]).astype(BF16)


def _fourier(f, cs_c, ca, sa, cb, sb, batch, seq):
    t = batch * seq
    const = lambda b: (0, 0)
    return pl.pallas_call(
        _fourier_kernel,
        grid=(batch,),
        in_specs=[
            pl.BlockSpec((seq, FOURIER_WIDTH), lambda b: (b, 0)),
            pl.BlockSpec(cs_c.shape, const),
            pl.BlockSpec(ca.shape, const),
            pl.BlockSpec(sa.shape, const),
            pl.BlockSpec(cb.shape, const),
            pl.BlockSpec(sb.shape, const),
        ],
        out_specs=pl.BlockSpec((seq, FOURIER_WIDTH), lambda b: (b, 0)),
        out_shape=jax.ShapeDtypeStruct((t, FOURIER_WIDTH), BF16),
        scratch_shapes=[pltpu.VMEM((seq, 2 * seq), BF16),
                        pltpu.VMEM((2 * seq, FOURIER_WIDTH), BF16)],
        compiler_params=_params(1),
        name="fourier",
    )(f, cs_c, ca, sa, cb, sb)


def _merge_kernel(o_ref, yf_ref, ga_ref, gf_ref, x_ref, mod_ref, wa_ref, wf_ref, wo_ref,
                  gpost_ref, gpre2_ref, x1_ref, h2_ref):
    ya = _dot(o_ref[...], wa_ref[...])
    yf = _dot(yf_ref[...], wf_ref[...])
    y = (ga_ref[...].astype(F32) * ya + gf_ref[...].astype(F32) * yf).astype(BF16)
    y2 = _dot(y, wo_ref[...])
    gate1 = mod_ref[2:3, :]
    x1 = x_ref[...] + gate1 * ((y2 * _rms_scale(y2)) * gpost_ref[...])
    x1_ref[...] = x1
    shift2 = mod_ref[3:4, :]
    scale2 = mod_ref[4:5, :]
    h2 = (x1 * _rms_scale(x1)) * (gpre2_ref[...] * (1.0 + scale2)) + shift2
    h2_ref[...] = h2.astype(BF16)


def _merge(o, yf, ga, gf, x2, mod3, wa, wf, wo, g_post, g_pre2, seq, tm):
    t = x2.shape[0]
    per_b = seq // tm
    row = lambda i: (i, 0)
    const = lambda i: (0, 0)
    return pl.pallas_call(
        _merge_kernel,
        grid=(t // tm,),
        in_specs=[
            pl.BlockSpec((tm, ATTN_WIDTH), row),
            pl.BlockSpec((tm, FOURIER_WIDTH), row),
            pl.BlockSpec((tm, D_MODEL), row),
            pl.BlockSpec((tm, D_MODEL), row),
            pl.BlockSpec((tm, D_MODEL), row),
            pl.BlockSpec((None, 6, D_MODEL), lambda i: (i // per_b, 0, 0)),
            pl.BlockSpec((ATTN_WIDTH, D_MODEL), const),
            pl.BlockSpec((FOURIER_WIDTH, D_MODEL), const),
            pl.BlockSpec((D_MODEL, D_MODEL), const),
            pl.BlockSpec((1, D_MODEL), const),
            pl.BlockSpec((1, D_MODEL), const),
        ],
        out_specs=[pl.BlockSpec((tm, D_MODEL), row), pl.BlockSpec((tm, D_MODEL), row)],
        out_shape=[jax.ShapeDtypeStruct((t, D_MODEL), F32),
                   jax.ShapeDtypeStruct((t, D_MODEL), BF16)],
        compiler_params=_params(1),
        name="merge",
    )(o, yf, ga, gf, x2, mod3, wa, wf, wo, g_post, g_pre2)


_HALO = BF16_SUBLANES
_PAD = 8


def _gelu_tanh(v):
    return 0.5 * v * (1.0 + jnp.tanh(math.sqrt(2.0 / math.pi) * (v + 0.044715 * (v * v * v))))


def _ffn_up_kernel(h_ref, hp_ref, hn_ref, wa_ref, wb_ref, cwa_ref, cwb_ref, cba_ref, cbb_ref,
                   o_ref, ua_ref, ub_ref, *, tiles_per_seq):
    tm = h_ref.shape[0]
    i = pl.program_id(1)
    has_prev = (i % tiles_per_seq != 0).astype(F32)
    has_next = (i % tiles_per_seq != tiles_per_seq - 1).astype(F32)
    h = h_ref[...]
    hp = hp_ref[...]
    hn = hn_ref[...]

    def conv(w_ref, cw_ref, cb_ref, u_ref):
        w = w_ref[...]
        u_ref[_PAD:_PAD + tm, :] = _dot(h, w)
        u_ref[_PAD - 1:_PAD, :] = _dot(hp, w)[_HALO - 1:_HALO, :] * has_prev
        u_ref[_PAD + tm:_PAD + tm + 1, :] = _dot(hn, w)[0:1, :] * has_next
        return (u_ref[_PAD - 1:_PAD - 1 + tm, :] * cw_ref[0:1, :]
                + u_ref[_PAD:_PAD + tm, :] * cw_ref[1:2, :]
                + u_ref[_PAD + 1:_PAD + 1 + tm, :] * cw_ref[2:3, :]
                + cb_ref[...])

    a = conv(wa_ref, cwa_ref, cba_ref, ua_ref)
    b = conv(wb_ref, cwb_ref, cbb_ref, ub_ref)
    o_ref[...] = (_gelu_tanh(a) * b).astype(BF16)


def _ffn_up(h2, w_up, conv_w, conv_b, seq, tm, tn):
    t = h2.shape[0]
    nj = D_FF // tn
    tiles_per_seq = seq // tm
    halo_per_tile = tm // _HALO
    n_halo = t // _HALO
    return pl.pallas_call(
        functools.partial(_ffn_up_kernel, tiles_per_seq=tiles_per_seq),
        grid=(nj, t // tm),
        in_specs=[
            pl.BlockSpec((tm, D_MODEL), lambda j, i: (i, 0)),
            pl.BlockSpec((_HALO, D_MODEL), lambda j, i: (jnp.maximum(i * halo_per_tile - 1, 0), 0)),
            pl.BlockSpec((_HALO, D_MODEL),
                         lambda j, i: (jnp.minimum((i + 1) * halo_per_tile, n_halo - 1), 0)),
            pl.BlockSpec((D_MODEL, tn), lambda j, i: (0, j)),
            pl.BlockSpec((D_MODEL, tn), lambda j, i: (0, nj + j)),
            pl.BlockSpec((CONV_WIDTH, tn), lambda j, i: (0, j)),
            pl.BlockSpec((CONV_WIDTH, tn), lambda j, i: (0, nj + j)),
            pl.BlockSpec((1, tn), lambda j, i: (0, j)),
            pl.BlockSpec((1, tn), lambda j, i: (0, nj + j)),
        ],
        out_specs=pl.BlockSpec((tm, tn), lambda j, i: (i, j)),
        out_shape=jax.ShapeDtypeStruct((t, D_FF), BF16),
        scratch_shapes=[pltpu.VMEM((tm + 2 * _PAD, tn), F32),
                        pltpu.VMEM((tm + 2 * _PAD, tn), F32)],
        compiler_params=_params(2),
        name="ffn_up",
    )(h2, h2, h2, w_up, w_up, conv_w, conv_w, conv_b, conv_b)


def _ffn_down_kernel(a_ref, w_ref, x1_ref, mod_ref, g_ref, o_ref):
    y = _dot(a_ref[...], w_ref[...])
    gate2 = mod_ref[5:6, :]
    o_ref[...] = x1_ref[...] + gate2 * ((y * _rms_scale(y)) * g_ref[...])


def _ffn_down(act, w_down, x1, mod3, g_post, seq, tm):
    t = x1.shape[0]
    per_b = seq // tm
    row = lambda i: (i, 0)
    const = lambda i: (0, 0)
    return pl.pallas_call(
        _ffn_down_kernel,
        grid=(t // tm,),
        in_specs=[
            pl.BlockSpec((tm, D_FF), row),
            pl.BlockSpec((D_FF, D_MODEL), const),
            pl.BlockSpec((tm, D_MODEL), row),
            pl.BlockSpec((None, 6, D_MODEL), lambda i: (i // per_b, 0, 0)),
            pl.BlockSpec((1, D_MODEL), const),
        ],
        out_specs=pl.BlockSpec((tm, D_MODEL), row),
        out_shape=jax.ShapeDtypeStruct((t, D_MODEL), F32),
        compiler_params=_params(1),
        name="ffn_down",
    )(act, w_down, x1, mod3, g_post)


@functools.lru_cache(maxsize=None)
def _rope_tables(seq):
    pos = np.arange(seq)
    row = (pos // GRID_W).astype(np.float64)
    col = (pos % GRID_W).astype(np.float64)
    inv = ROPE_THETA ** (-np.arange(ROPE_PAIRS_PER_AXIS, dtype=np.float64) / ROPE_PAIRS_PER_AXIS)
    ang = np.concatenate([row[:, None] * inv, col[:, None] * inv], axis=-1)
    cos = np.repeat(np.cos(ang), 2, axis=-1)
    sin = np.repeat(np.sin(ang), 2, axis=-1)
    sign = np.tile(np.array([-1.0, 1.0]), HEAD_DIM // 2)
    reps = LANES // HEAD_DIM
    return (np.tile(cos, (1, reps)).astype(np.float32),
            np.tile(sin * sign, (1, reps)).astype(np.float32))


@functools.lru_cache(maxsize=None)
def _dft_tables(seq):
    gd = FOURIER_GROUP_DIM
    kc = (np.outer(np.arange(gd), np.arange(gd)) % gd).astype(np.float64) * (2.0 * np.pi / gd)
    norm = 1.0 / math.sqrt(seq * gd)
    cs_c = np.concatenate([np.cos(kc), np.sin(kc)], axis=1) * norm
    n1 = seq // _DFT_SPLIT
    t_idx = np.arange(seq)
    ang_a = (np.outer(np.arange(n1), t_idx) % n1).astype(np.float64) * (2.0 * np.pi / n1)
    ang_b = (np.outer(np.arange(_DFT_SPLIT), t_idx) % seq).astype(np.float64) * (2.0 * np.pi / seq)
    f32 = lambda a: a.astype(np.float32)
    return f32(cs_c), f32(np.cos(ang_a)), f32(np.sin(ang_a)), f32(np.cos(ang_b)), f32(np.sin(ang_b))


def kernel(x, c, w_ada, b_ada, mix_pre_g, w_in, q_norm_g, k_norm_g, b_gate, w_attn_branch,
           w_fourier_branch, w_out, mix_post_g, ffn_pre_g, w_up, conv_w, conv_b, w_down,
           ffn_post_g):
    batch, seq, _ = x.shape
    depth = w_ada.shape[0]
    t = batch * seq
    tm = 512

    cos_np, sin_np = _rope_tables(seq)
    cos_t = jnp.asarray(cos_np)
    sin_t = jnp.asarray(sin_np)
    dft_tabs = [jnp.asarray(a) for a in _dft_tables(seq)]
    seg = np.arange(ATTN_WIDTH) // HEAD_DIM
    ones_blk = jnp.asarray((seg[:, None] == seg[None, :]).astype(np.float32), dtype=BF16)

    x2 = x.reshape(t, D_MODEL)
    for l in range(depth):
        mod3 = _adaln(c, w_ada[l], b_ada[l]).reshape(batch, 6, D_MODEL)
        gq = (jnp.tile(q_norm_g[l], N_Q_HEADS) * (HEAD_DIM ** -0.5)).reshape(1, ATTN_WIDTH)
        gk = jnp.tile(k_norm_g[l], N_KV_HEADS).reshape(1, KV_WIDTH)
        q, kt, v, f, ga, gf = _in_proj(
            x2, mod3, mix_pre_g[l].reshape(1, D_MODEL), w_in[l].astype(BF16),
            b_gate[l].reshape(1, 2 * D_MODEL), gq, gk, cos_t, sin_t, ones_blk, batch, seq, tm)
        o = _attention(q, kt, v, batch, seq, tq=256)
        yf = _fourier(f, *dft_tabs, batch, seq)
        x1, h2 = _merge(
            o, yf, ga, gf, x2, mod3, w_attn_branch[l].astype(BF16),
            w_fourier_branch[l].astype(BF16), w_out[l].astype(BF16),
            mix_post_g[l].reshape(1, D_MODEL), ffn_pre_g[l].reshape(1, D_MODEL), seq, tm)
        act = _ffn_up(h2, w_up[l].astype(BF16), conv_w[l], conv_b[l].reshape(1, 2 * D_FF),
                      seq, tm, tn=D_FF // 2)
        x2 = _ffn_down(act, w_down[l].astype(BF16), x1, mod3,
                       ffn_post_g[l].reshape(1, D_MODEL), seq, tm)
    return x2.reshape(batch, seq, D_MODEL)
```

```python
import functools
import math

import numpy as np
import jax
import jax.numpy as jnp
from jax import lax
from jax.experimental import pallas as pl
from jax.experimental.pallas import tpu as pltpu

D_MODEL = 1024
GRID_W = 64
HEAD_DIM = 64
N_Q_HEADS = 8
N_KV_HEADS = 2
Q_PER_KV = N_Q_HEADS // N_KV_HEADS
ATTN_WIDTH = N_Q_HEADS * HEAD_DIM
KV_WIDTH = N_KV_HEADS * HEAD_DIM
N_FOURIER_GROUPS = 4
FOURIER_GROUP_DIM = 128
FOURIER_WIDTH = N_FOURIER_GROUPS * FOURIER_GROUP_DIM
IN_WIDTH = ATTN_WIDTH + 2 * KV_WIDTH + FOURIER_WIDTH + 2 * D_MODEL
ROPE_THETA = 10000.0
ROPE_PAIRS_PER_AXIS = HEAD_DIM // 4
D_FF = 2816
CONV_WIDTH = 3
NORM_EPS = 1e-6

LANES = 128
BF16_SUBLANES = 16
VMEM_LIMIT_BYTES = 56 * 1024 * 1024

_LOG2_E = math.log2(math.e)

F32 = jnp.float32
BF16 = jnp.bfloat16

_O_K = ATTN_WIDTH
_O_V = _O_K + KV_WIDTH
_O_F = _O_V + KV_WIDTH
_O_GA = _O_F + FOURIER_WIDTH
_O_GF = _O_GA + D_MODEL


def _params(n_grid_dims, flags=None):
    return pltpu.CompilerParams(
        dimension_semantics=("arbitrary",) * n_grid_dims,
        vmem_limit_bytes=VMEM_LIMIT_BYTES,
        flags=flags,
    )


def _dot(a, b):
    return jnp.dot(a, b, preferred_element_type=F32)


def _rms_scale(v):
    return lax.rsqrt(jnp.mean(v * v, axis=-1, keepdims=True) + NORM_EPS)


def _adaln_kernel(c_ref, w_ref, b_ref, o_ref):
    c = c_ref[...]
    a = c / (1.0 + jnp.exp(-c))
    a_hi = a.astype(BF16)
    a_lo = (a - a_hi.astype(F32)).astype(BF16)
    w = w_ref[...]
    w_hi = w.astype(BF16)
    w_lo = (w - w_hi.astype(F32)).astype(BF16)
    o_ref[...] = _dot(a_hi, w_hi) + (_dot(a_lo, w_hi) + _dot(a_hi, w_lo)) + b_ref[...]


def _adaln(c, w_ada, b_ada):
    batch = c.shape[0]
    n = w_ada.shape[1]
    tn = 1536
    return pl.pallas_call(
        _adaln_kernel,
        grid=(n // tn,),
        in_specs=[
            pl.BlockSpec((batch, D_MODEL), lambda j: (0, 0)),
            pl.BlockSpec((D_MODEL, tn), lambda j: (0, j)),
            pl.BlockSpec((1, tn), lambda j: (0, j)),
        ],
        out_specs=pl.BlockSpec((batch, tn), lambda j: (0, j)),
        out_shape=jax.ShapeDtypeStruct((batch, n), F32),
        compiler_params=_params(1),
        name="adaln",
    )(c, w_ada, b_ada.reshape(1, n))


def _swap_pairs(v):
    n = v.shape[-1]
    lane = lax.broadcasted_iota(jnp.int32, v.shape, v.ndim - 1)
    from_right = pltpu.roll(v, n - 1, v.ndim - 1)
    from_left = pltpu.roll(v, 1, v.ndim - 1)
    return jnp.where((lane & 1) == 0, from_right, from_left)


def _head_norm_rope(z, gain, ones_blk, cos, sin):
    reps = z.shape[-1] // LANES
    ssq = _dot((z * z).astype(BF16), ones_blk)
    r = lax.rsqrt(ssq * (1.0 / HEAD_DIM) + NORM_EPS)
    zg = z * gain
    cos_t = jnp.concatenate([cos] * reps, axis=-1) if reps > 1 else cos
    sin_t = jnp.concatenate([sin] * reps, axis=-1) if reps > 1 else sin
    return (zg * cos_t + _swap_pairs(zg) * sin_t) * r


def _in_proj_kernel(x_ref, mod_ref, g_ref, w_ref, bg_ref, gq_ref, gk_ref,
                    cos_ref, sin_ref, ones_ref,
                    q_ref, kt_ref, v_ref, f_ref, ga_ref, gf_ref):
    x = x_ref[...]
    shift = mod_ref[0:1, :]
    scale = mod_ref[1:2, :]
    h = (x * _rms_scale(x)) * (g_ref[...] * (1.0 + scale)) + shift
    hb = h.astype(BF16)
    cos = cos_ref[...]
    sin = sin_ref[...]

    zq = _dot(hb, w_ref[:, 0:_O_K])
    q_ref[...] = _head_norm_rope(zq, gq_ref[...], ones_ref[...], cos, sin).astype(BF16)

    zkv = _dot(hb, w_ref[:, _O_K:_O_F])
    k = _head_norm_rope(zkv[:, 0:KV_WIDTH], gk_ref[...],
                        ones_ref[0:KV_WIDTH, 0:KV_WIDTH], cos, sin)
    kt_ref[...] = k.T.astype(BF16)
    zv = zkv[:, KV_WIDTH:2 * KV_WIDTH]
    ones_cols = jnp.ones((zv.shape[0], HEAD_DIM), F32)
    v_ref[...] = jnp.concatenate(
        [piece for kv in range(N_KV_HEADS)
         for piece in (zv[:, kv * HEAD_DIM:(kv + 1) * HEAD_DIM], ones_cols)],
        axis=-1).astype(BF16)

    f_ref[...] = _dot(hb, w_ref[:, _O_F:_O_GA]).astype(BF16)

    ta = _dot(hb, w_ref[:, _O_GA:_O_GF]) + bg_ref[:, 0:D_MODEL]
    ga_ref[...] = (1.0 / (1.0 + jnp.exp(-ta))).astype(BF16)
    tf = _dot(hb, w_ref[:, _O_GF:IN_WIDTH]) + bg_ref[:, D_MODEL:2 * D_MODEL]
    gf_ref[...] = (1.0 / (1.0 + jnp.exp(-tf))).astype(BF16)


def _in_proj(x2, mod3, g_pre, w_in, b_gate, gq, gk, cos_t, sin_t, ones_blk, batch, seq, tm):
    t = batch * seq
    per_b = seq // tm
    row = lambda i: (i, 0)
    const = lambda i: (0, 0)
    return pl.pallas_call(
        _in_proj_kernel,
        grid=(t // tm,),
        in_specs=[
            pl.BlockSpec((tm, D_MODEL), row),
            pl.BlockSpec((None, 6, D_MODEL), lambda i: (i // per_b, 0, 0)),
            pl.BlockSpec((1, D_MODEL), const),
            pl.BlockSpec((D_MODEL, IN_WIDTH), const),
            pl.BlockSpec((1, 2 * D_MODEL), const),
            pl.BlockSpec((1, ATTN_WIDTH), const),
            pl.BlockSpec((1, KV_WIDTH), const),
            pl.BlockSpec((tm, LANES), lambda i: (i % per_b, 0)),
            pl.BlockSpec((tm, LANES), lambda i: (i % per_b, 0)),
            pl.BlockSpec((ATTN_WIDTH, ATTN_WIDTH), const),
        ],
        out_specs=[
            pl.BlockSpec((tm, ATTN_WIDTH), row),
            pl.BlockSpec((None, KV_WIDTH, tm), lambda i: (i // per_b, 0, i % per_b)),
            pl.BlockSpec((tm, N_KV_HEADS * LANES), row),
            pl.BlockSpec((tm, FOURIER_WIDTH), row),
            pl.BlockSpec((tm, D_MODEL), row),
            pl.BlockSpec((tm, D_MODEL), row),
        ],
        out_shape=[
            jax.ShapeDtypeStruct((t, ATTN_WIDTH), BF16),
            jax.ShapeDtypeStruct((batch, KV_WIDTH, seq), BF16),
            jax.ShapeDtypeStruct((t, N_KV_HEADS * LANES), BF16),
            jax.ShapeDtypeStruct((t, FOURIER_WIDTH), BF16),
            jax.ShapeDtypeStruct((t, D_MODEL), BF16),
            jax.ShapeDtypeStruct((t, D_MODEL), BF16),
        ],
        compiler_params=_params(1),
        name="in_proj",
    )(x2, mod3, g_pre, w_in, b_gate, gq, gk, cos_t, sin_t, ones_blk)


_ATTN_SLOTS = 2


def _attn_step(q_ref, kt_ref, v_ref, o_ref, s_ref, p_new_ref, p_old_ref):
    outs = []
    vcat = v_ref[...]
    for h in range(N_Q_HEADS):
        kv = h // Q_PER_KV
        slot = h % _ATTN_SLOTS
        s_ref[slot] = _dot(q_ref[:, h * HEAD_DIM:(h + 1) * HEAD_DIM],
                           kt_ref[kv * HEAD_DIM:(kv + 1) * HEAD_DIM, :])
        s = s_ref[slot]
        m = jnp.max(s, axis=-1, keepdims=True)
        p_new_ref[h] = jnp.exp2(s - m).astype(BF16)
        ov = _dot(p_old_ref[h], vcat)[:, kv * LANES:(kv + 1) * LANES]
        on = ov / pltpu.roll(ov, HEAD_DIM, 1)
        outs.append(on[:, 0:HEAD_DIM])
    o_ref[...] = jnp.concatenate(outs, axis=-1).astype(BF16)


def _attn_kernel(q_ref, kt_ref, v_ref, o_ref, s_ref, pa_ref, pb_ref):
    g = pl.program_id(0)

    @pl.when(g == 0)
    def _():
        pb_ref[...] = jnp.ones(pb_ref.shape, BF16)

    @pl.when(g % 2 == 0)
    def _():
        _attn_step(q_ref, kt_ref, v_ref, o_ref, s_ref, pa_ref, pb_ref)

    @pl.when(g % 2 == 1)
    def _():
        _attn_step(q_ref, kt_ref, v_ref, o_ref, s_ref, pb_ref, pa_ref)


def _attention(q, kt, v, batch, seq, tq):
    t = batch * seq
    per_b = seq // tq
    n_blk = t // tq
    cur = lambda g: jnp.minimum(g, n_blk - 1)
    prev = lambda g: jnp.maximum(g - 1, 0)
    return pl.pallas_call(
        _attn_kernel,
        grid=(n_blk + 1,),
        in_specs=[
            pl.BlockSpec((tq, ATTN_WIDTH), lambda g: (cur(g), 0)),
            pl.BlockSpec((None, KV_WIDTH, seq), lambda g: (cur(g) // per_b, 0, 0)),
            pl.BlockSpec((seq, N_KV_HEADS * LANES), lambda g: (prev(g) // per_b, 0)),
        ],
        out_specs=pl.BlockSpec((tq, ATTN_WIDTH), lambda g: (prev(g), 0)),
        out_shape=jax.ShapeDtypeStruct((t, ATTN_WIDTH), BF16),
        scratch_shapes=[pltpu.VMEM((_ATTN_SLOTS, tq, seq), F32),
                        pltpu.VMEM((N_Q_HEADS, tq, seq), BF16),
                        pltpu.VMEM((N_Q_HEADS, tq, seq), BF16)],
        compiler_params=_params(1),
        name="attn",
    )(q, kt, v)


_DFT_SPLIT = 32


def _fourier_kernel(f_ref, cs_ref, ca_ref, sa_ref, cb_ref, sb_ref, o_ref, w_ref, r_ref):
    seq = f_ref.shape[0]
    gd = FOURIER_GROUP_DIM

    @pl.when(pl.program_id(0) == 0)
    def _():
        cb = cb_ref[...]
        sb = sb_ref[...]

        def body(s1, carry):
            ca = ca_ref[pl.ds(s1, 1), :]
            sa = sa_ref[pl.ds(s1, 1), :]
            rows = pl.ds(pl.multiple_of(s1 * _DFT_SPLIT, _DFT_SPLIT), _DFT_SPLIT)
            w_ref[rows, 0:seq] = (ca * cb - sa * sb).astype(BF16)
            w_ref[rows, seq:2 * seq] = (-(sa * cb + ca * sb)).astype(BF16)
            return carry

        lax.fori_loop(0, seq // _DFT_SPLIT, body, 0)

    cs = cs_ref[...].astype(BF16)
    for g in range(N_FOURIER_GROUPS):
        xcs = _dot(f_ref[:, g * gd:(g + 1) * gd], cs)
        r_ref[0:seq, g * gd:(g + 1) * gd] = xcs[:, 0:gd].astype(BF16)
        r_ref[seq:2 * seq, g * gd:(g + 1) * gd] = xcs[:, gd:2 * gd].astype(BF16)
    tm = 512
    for mi in range(seq // tm):
        o_ref[mi * tm:(mi + 1) * tm, :] = _dot(
            w_ref[mi * tm:(mi + 1) * tm, :], r_ref[...]).astype(BF16)


def _fourier(f, cs_c, ca, sa, cb, sb, batch, seq):
    t = batch * seq
    const = lambda b: (0, 0)
    return pl.pallas_call(
        _fourier_kernel,
        grid=(batch,),
        in_specs=[
            pl.BlockSpec((seq, FOURIER_WIDTH), lambda b: (b, 0)),
            pl.BlockSpec(cs_c.shape, const),
            pl.BlockSpec(ca.shape, const),
            pl.BlockSpec(sa.shape, const),
            pl.BlockSpec(cb.shape, const),
            pl.BlockSpec(sb.shape, const),
        ],
        out_specs=pl.BlockSpec((seq, FOURIER_WIDTH), lambda b: (b, 0)),
        out_shape=jax.ShapeDtypeStruct((t, FOURIER_WIDTH), BF16),
        scratch_shapes=[pltpu.VMEM((seq, 2 * seq), BF16),
                        pltpu.VMEM((2 * seq, FOURIER_WIDTH), BF16)],
        compiler_params=_params(1),
        name="fourier",
    )(f, cs_c, ca, sa, cb, sb)


def _merge_kernel(o_ref, yf_ref, ga_ref, gf_ref, x_ref, mod_ref, wa_ref, wf_ref, wo_ref,
                  gpost_ref, gpre2_ref, x1_ref, h2_ref):
    ya = _dot(o_ref[...], wa_ref[...])
    yf = _dot(yf_ref[...], wf_ref[...])
    y = (ga_ref[...].astype(F32) * ya + gf_ref[...].astype(F32) * yf).astype(BF16)
    y2 = _dot(y, wo_ref[...])
    gate1 = mod_ref[2:3, :]
    x1 = x_ref[...] + gate1 * ((y2 * _rms_scale(y2)) * gpost_ref[...])
    x1_ref[...] = x1
    shift2 = mod_ref[3:4, :]
    scale2 = mod_ref[4:5, :]
    h2 = (x1 * _rms_scale(x1)) * (gpre2_ref[...] * (1.0 + scale2)) + shift2
    h2_ref[...] = h2.astype(BF16)


def _merge(o, yf, ga, gf, x2, mod3, wa, wf, wo, g_post, g_pre2, seq, tm):
    t = x2.shape[0]
    per_b = seq // tm
    row = lambda i: (i, 0)
    const = lambda i: (0, 0)
    return pl.pallas_call(
        _merge_kernel,
        grid=(t // tm,),
        in_specs=[
            pl.BlockSpec((tm, ATTN_WIDTH), row),
            pl.BlockSpec((tm, FOURIER_WIDTH), row),
            pl.BlockSpec((tm, D_MODEL), row),
            pl.BlockSpec((tm, D_MODEL), row),
            pl.BlockSpec((tm, D_MODEL), row),
            pl.BlockSpec((None, 6, D_MODEL), lambda i: (i // per_b, 0, 0)),
            pl.BlockSpec((ATTN_WIDTH, D_MODEL), const),
            pl.BlockSpec((FOURIER_WIDTH, D_MODEL), const),
            pl.BlockSpec((D_MODEL, D_MODEL), const),
            pl.BlockSpec((1, D_MODEL), const),
            pl.BlockSpec((1, D_MODEL), const),
        ],
        out_specs=[pl.BlockSpec((tm, D_MODEL), row), pl.BlockSpec((tm, D_MODEL), row)],
        out_shape=[jax.ShapeDtypeStruct((t, D_MODEL), F32),
                   jax.ShapeDtypeStruct((t, D_MODEL), BF16)],
        compiler_params=_params(1),
        name="merge",
    )(o, yf, ga, gf, x2, mod3, wa, wf, wo, g_post, g_pre2)


_HALO = BF16_SUBLANES
_MXU_COLS = 256
_GELU_K1 = math.sqrt(2.0 / math.pi)
_GELU_K2 = _GELU_K1 * 0.044715


def _ffn_up_kernel(h_ref, hp_ref, hn_ref, w_ref, cw_ref, cb_ref, o_ref,
                   hx_ref, hperm_ref, st_ref, *, tiles_per_seq):
    tm = h_ref.shape[0]
    ext = tm + 2 * _HALO
    pitch = ext // 8
    i = pl.program_id(0)
    keep_prev = jnp.where(i % tiles_per_seq != 0, 1.0, 0.0)
    keep_next = jnp.where(i % tiles_per_seq != tiles_per_seq - 1, 1.0, 0.0)

    def rows(b):
        return pl.ds(b, 8, stride=pitch)

    for c in range(D_MODEL // LANES):
        lc = slice(c * LANES, (c + 1) * LANES)
        hx_ref[c, 0:_HALO, :] = hp_ref[:, lc].astype(F32) * keep_prev
        hx_ref[c, _HALO:_HALO + tm, :] = h_ref[:, lc].astype(F32)
        hx_ref[c, _HALO + tm:ext, :] = hn_ref[:, lc].astype(F32) * keep_next
    for c in range(D_MODEL // LANES):
        lc = slice(c * LANES, (c + 1) * LANES)
        for b in range(0, pitch, 2):
            pair = jnp.concatenate([hx_ref[c, rows(b), :], hx_ref[c, rows(b + 1), :]], axis=0)
            hperm_ref[8 * b:8 * b + 16, lc] = pair.astype(BF16)
    hperm = hperm_ref[...]

    for jj in range(D_FF // _MXU_COLS):
        res_a = _dot(hperm, w_ref[:, jj * _MXU_COLS:(jj + 1) * _MXU_COLS])
        res_b = _dot(hperm, w_ref[:, D_FF + jj * _MXU_COLS:D_FF + (jj + 1) * _MXU_COLS])
        for k in range(_MXU_COLS // LANES):
            j = 2 * jj + k
            la = slice(j * LANES, (j + 1) * LANES)
            lb = slice(D_FF + j * LANES, D_FF + (j + 1) * LANES)
            lk = slice(k * LANES, (k + 1) * LANES)
            bc = lambda ref, r, l: jnp.broadcast_to(ref[r:r + 1, l], (8, LANES))
            wa = [bc(cw_ref, r, la) for r in range(CONV_WIDTH)]
            wb = [bc(cw_ref, r, lb) for r in range(CONV_WIDTH)]
            ba = bc(cb_ref, 0, la)
            bb = bc(cb_ref, 0, lb)

            def vreg(res, b):
                if b < 0:
                    return pltpu.roll(res[8 * (pitch - 1):8 * pitch, lk], 1, 0)
                if b >= pitch:
                    return pltpu.roll(res[0:8, lk], 7, 0)
                return res[8 * b:8 * b + 8, lk]

            for b in range(pitch):
                ca = (vreg(res_a, b - 1) * wa[0] + vreg(res_a, b) * wa[1]
                      + vreg(res_a, b + 1) * wa[2] + ba)
                cv = (vreg(res_b, b - 1) * wb[0] + vreg(res_b, b) * wb[1]
                      + vreg(res_b, b + 1) * wb[2] + bb)
                t = jnp.tanh(ca * (_GELU_K1 + _GELU_K2 * (ca * ca)))
                st_ref[j % 2, rows(b), :] = (ca * cv) * (1.0 + t)
            o_ref[:, la] = st_ref[j % 2, _HALO:_HALO + tm, :].astype(BF16)


def _ffn_up(h2, w_up, conv_w, conv_b, seq, tm):
    t = h2.shape[0]
    tiles_per_seq = seq // tm
    halo_per_tile = tm // _HALO
    n_halo = t // _HALO
    ext = tm + 2 * _HALO
    const = lambda i: (0, 0)
    return pl.pallas_call(
        functools.partial(_ffn_up_kernel, tiles_per_seq=tiles_per_seq),
        grid=(t // tm,),
        in_specs=[
            pl.BlockSpec((tm, D_MODEL), lambda i: (i, 0)),
            pl.BlockSpec((_HALO, D_MODEL), lambda i: (jnp.maximum(i * halo_per_tile - 1, 0), 0)),
            pl.BlockSpec((_HALO, D_MODEL),
                         lambda i: (jnp.minimum((i + 1) * halo_per_tile, n_halo - 1), 0)),
            pl.BlockSpec((D_MODEL, 2 * D_FF), const),
            pl.BlockSpec((CONV_WIDTH, 2 * D_FF), const),
            pl.BlockSpec((1, 2 * D_FF), const),
        ],
        out_specs=pl.BlockSpec((tm, D_FF), lambda i: (i, 0)),
        out_shape=jax.ShapeDtypeStruct((t, D_FF), BF16),
        scratch_shapes=[pltpu.VMEM((D_MODEL // LANES, ext, LANES), F32),
                        pltpu.VMEM((ext, D_MODEL), BF16),
                        pltpu.VMEM((2, ext, LANES), F32)],
        compiler_params=_params(1),
        name="ffn_up",
    )(h2, h2, h2, w_up, conv_w, conv_b)


def _ffn_down_kernel(a_ref, w_ref, x1_ref, mod_ref, g_ref, o_ref):
    y = _dot(a_ref[...], w_ref[...])
    gate2 = mod_ref[5:6, :]
    o_ref[...] = x1_ref[...] + gate2 * ((y * _rms_scale(y)) * g_ref[...])


def _ffn_down(act, w_down, x1, mod3, g_post, seq, tm):
    t = x1.shape[0]
    per_b = seq // tm
    row = lambda i: (i, 0)
    const = lambda i: (0, 0)
    return pl.pallas_call(
        _ffn_down_kernel,
        grid=(t // tm,),
        in_specs=[
            pl.BlockSpec((tm, D_FF), row),
            pl.BlockSpec((D_FF, D_MODEL), const),
            pl.BlockSpec((tm, D_MODEL), row),
            pl.BlockSpec((None, 6, D_MODEL), lambda i: (i // per_b, 0, 0)),
            pl.BlockSpec((1, D_MODEL), const),
        ],
        out_specs=pl.BlockSpec((tm, D_MODEL), row),
        out_shape=jax.ShapeDtypeStruct((t, D_MODEL), F32),
        compiler_params=_params(1),
        name="ffn_down",
    )(act, w_down, x1, mod3, g_post)


@functools.lru_cache(maxsize=None)
def _rope_tables(seq):
    pos = np.arange(seq)
    row = (pos // GRID_W).astype(np.float64)
    col = (pos % GRID_W).astype(np.float64)
    inv = ROPE_THETA ** (-np.arange(ROPE_PAIRS_PER_AXIS, dtype=np.float64) / ROPE_PAIRS_PER_AXIS)
    ang = np.concatenate([row[:, None] * inv, col[:, None] * inv], axis=-1)
    cos = np.repeat(np.cos(ang), 2, axis=-1)
    sin = np.repeat(np.sin(ang), 2, axis=-1)
    sign = np.tile(np.array([-1.0, 1.0]), HEAD_DIM // 2)
    reps = LANES // HEAD_DIM
    return (np.tile(cos, (1, reps)).astype(np.float32),
            np.tile(sin * sign, (1, reps)).astype(np.float32))


@functools.lru_cache(maxsize=None)
def _dft_tables(seq):
    gd = FOURIER_GROUP_DIM
    kc = (np.outer(np.arange(gd), np.arange(gd)) % gd).astype(np.float64) * (2.0 * np.pi / gd)
    norm = 1.0 / math.sqrt(seq * gd)
    cs_c = np.concatenate([np.cos(kc), np.sin(kc)], axis=1) * norm
    n1 = seq // _DFT_SPLIT
    t_idx = np.arange(seq)
    ang_a = (np.outer(np.arange(n1), t_idx) % n1).astype(np.float64) * (2.0 * np.pi / n1)
    ang_b = (np.outer(np.arange(_DFT_SPLIT), t_idx) % seq).astype(np.float64) * (2.0 * np.pi / seq)
    f32 = lambda a: a.astype(np.float32)
    return f32(cs_c), f32(np.cos(ang_a)), f32(np.sin(ang_a)), f32(np.cos(ang_b)), f32(np.sin(ang_b))


def kernel(x, c, w_ada, b_ada, mix_pre_g, w_in, q_norm_g, k_norm_g, b_gate, w_attn_branch,
           w_fourier_branch, w_out, mix_post_g, ffn_pre_g, w_up, conv_w, conv_b, w_down,
           ffn_post_g):
    batch, seq, _ = x.shape
    depth = w_ada.shape[0]
    t = batch * seq
    tm = 512

    cos_np, sin_np = _rope_tables(seq)
    cos_t = jnp.asarray(cos_np)
    sin_t = jnp.asarray(sin_np)
    dft_tabs = [jnp.asarray(a) for a in _dft_tables(seq)]
    seg = np.arange(ATTN_WIDTH) // HEAD_DIM
    ones_blk = jnp.asarray((seg[:, None] == seg[None, :]).astype(np.float32), dtype=BF16)

    x2 = x.reshape(t, D_MODEL)
    for l in range(depth):
        mod3 = _adaln(c, w_ada[l], b_ada[l]).reshape(batch, 6, D_MODEL)
        gq = (jnp.tile(q_norm_g[l], N_Q_HEADS) * (HEAD_DIM ** -0.5 * _LOG2_E)).reshape(1, ATTN_WIDTH)
        gk = jnp.tile(k_norm_g[l], N_KV_HEADS).reshape(1, KV_WIDTH)
        q, kt, v, f, ga, gf = _in_proj(
            x2, mod3, mix_pre_g[l].reshape(1, D_MODEL), w_in[l].astype(BF16),
            b_gate[l].reshape(1, 2 * D_MODEL), gq, gk, cos_t, sin_t, ones_blk, batch, seq, tm)
        o = _attention(q, kt, v, batch, seq, tq=256)
        yf = _fourier(f, *dft_tabs, batch, seq)
        x1, h2 = _merge(
            o, yf, ga, gf, x2, mod3, w_attn_branch[l].astype(BF16),
            w_fourier_branch[l].astype(BF16), w_out[l].astype(BF16),
            mix_post_g[l].reshape(1, D_MODEL), ffn_pre_g[l].reshape(1, D_MODEL), seq, tm)
        glu_half = jnp.concatenate([jnp.ones((D_FF,), F32), jnp.full((D_FF,), 0.5, F32)])
        act = _ffn_up(h2, w_up[l].astype(BF16), conv_w[l] * glu_half,
                      (conv_b[l] * glu_half).reshape(1, 2 * D_FF), seq, tm)
        x2 = _ffn_down(act, w_down[l].astype(BF16), x1, mod3,
                       ffn_post_g[l].reshape(1, D_MODEL), seq, tm)
    return x2.reshape(batch, seq, D_MODEL)
```

```python
import functools
import math

import numpy as np
import jax
import jax.numpy as jnp
from jax import lax
from jax.experimental import pallas as pl
from jax.experimental.pallas import tpu as pltpu

D_MODEL = 1024
GRID_W = 64
HEAD_DIM = 64
N_Q_HEADS = 8
N_KV_HEADS = 2
Q_PER_KV = N_Q_HEADS // N_KV_HEADS
ATTN_WIDTH = N_Q_HEADS * HEAD_DIM
KV_WIDTH = N_KV_HEADS * HEAD_DIM
N_FOURIER_GROUPS = 4
FOURIER_GROUP_DIM = 128
FOURIER_WIDTH = N_FOURIER_GROUPS * FOURIER_GROUP_DIM
IN_WIDTH = ATTN_WIDTH + 2 * KV_WIDTH + FOURIER_WIDTH + 2 * D_MODEL
ROPE_THETA = 10000.0
ROPE_PAIRS_PER_AXIS = HEAD_DIM // 4
D_FF = 2816
CONV_WIDTH = 3
NORM_EPS = 1e-6

LANES = 128
BF16_SUBLANES = 16
VMEM_LIMIT_BYTES = 56 * 1024 * 1024
_ROW_CHUNK = 256
_TILES = {"in_proj": 512, "attn": 256, "merge": 1024, "ffn_up": 512, "ffn_down": 1024}

_LOG2_E = math.log2(math.e)

F32 = jnp.float32
BF16 = jnp.bfloat16

_O_K = ATTN_WIDTH
_O_V = _O_K + KV_WIDTH
_O_F = _O_V + KV_WIDTH
_O_GA = _O_F + FOURIER_WIDTH
_O_GF = _O_GA + D_MODEL


def _params(n_grid_dims, flags=None):
    return pltpu.CompilerParams(
        dimension_semantics=("arbitrary",) * n_grid_dims,
        vmem_limit_bytes=VMEM_LIMIT_BYTES,
        flags=flags,
    )


def _dot(a, b):
    return jnp.dot(a, b, preferred_element_type=F32)


def _rms_scale(v):
    return lax.rsqrt(jnp.mean(v * v, axis=-1, keepdims=True) + NORM_EPS)


def _adaln_kernel(c_ref, w_ref, b_ref, o_ref):
    c = c_ref[...]
    a = c / (1.0 + jnp.exp(-c))
    a_hi = a.astype(BF16)
    a_lo = (a - a_hi.astype(F32)).astype(BF16)
    w = w_ref[...]
    w_hi = w.astype(BF16)
    w_lo = (w - w_hi.astype(F32)).astype(BF16)
    o_ref[...] = _dot(a_hi, w_hi) + (_dot(a_lo, w_hi) + _dot(a_hi, w_lo)) + b_ref[...]


def _adaln(c, w_ada, b_ada):
    batch = c.shape[0]
    n = w_ada.shape[1]
    tn = 1536
    return pl.pallas_call(
        _adaln_kernel,
        grid=(n // tn,),
        in_specs=[
            pl.BlockSpec((batch, D_MODEL), lambda j: (0, 0)),
            pl.BlockSpec((D_MODEL, tn), lambda j: (0, j)),
            pl.BlockSpec((1, tn), lambda j: (0, j)),
        ],
        out_specs=pl.BlockSpec((batch, tn), lambda j: (0, j)),
        out_shape=jax.ShapeDtypeStruct((batch, n), F32),
        compiler_params=_params(1),
        name="adaln",
    )(c, w_ada, b_ada.reshape(1, n))


def _swap_pairs(v):
    n = v.shape[-1]
    lane = lax.broadcasted_iota(jnp.int32, v.shape, v.ndim - 1)
    from_right = pltpu.roll(v, n - 1, v.ndim - 1)
    from_left = pltpu.roll(v, 1, v.ndim - 1)
    return jnp.where((lane & 1) == 0, from_right, from_left)


def _head_norm_rope(z, gain, ones_blk, cos, sin):
    reps = z.shape[-1] // LANES
    ssq = _dot((z * z).astype(BF16), ones_blk)
    r = lax.rsqrt(ssq * (1.0 / HEAD_DIM) + NORM_EPS)
    zg = z * gain
    cos_t = jnp.concatenate([cos] * reps, axis=-1) if reps > 1 else cos
    sin_t = jnp.concatenate([sin] * reps, axis=-1) if reps > 1 else sin
    return (zg * cos_t + _swap_pairs(zg) * sin_t) * r


def _in_proj_kernel(x_ref, mod_ref, g_ref, w_ref, bg_ref, gq_ref, gk_ref,
                    cos_ref, sin_ref, ones_ref,
                    q_ref, kt_ref, v_ref, f_ref, ga_ref, gf_ref):
    shift = mod_ref[0:1, :]
    scale = mod_ref[1:2, :]
    gmod = g_ref[...] * (1.0 + scale)
    for r in range(x_ref.shape[0] // _ROW_CHUNK):
        rows = slice(r * _ROW_CHUNK, (r + 1) * _ROW_CHUNK)
        x = x_ref[rows, :]
        hb = ((x * _rms_scale(x)) * gmod + shift).astype(BF16)
        cos = cos_ref[rows, :]
        sin = sin_ref[rows, :]

        zq = _dot(hb, w_ref[:, 0:_O_K])
        q_ref[rows, :] = _head_norm_rope(zq, gq_ref[...], ones_ref[...], cos, sin).astype(BF16)

        zkv = _dot(hb, w_ref[:, _O_K:_O_F])
        k = _head_norm_rope(zkv[:, 0:KV_WIDTH], gk_ref[...],
                            ones_ref[0:KV_WIDTH, 0:KV_WIDTH], cos, sin)
        kt_ref[:, rows] = k.T.astype(BF16)
        zv = zkv[:, KV_WIDTH:2 * KV_WIDTH]
        ones_cols = jnp.ones((zv.shape[0], HEAD_DIM), F32)
        v_ref[rows, :] = jnp.concatenate(
            [piece for kv in range(N_KV_HEADS)
             for piece in (zv[:, kv * HEAD_DIM:(kv + 1) * HEAD_DIM], ones_cols)],
            axis=-1).astype(BF16)

        f_ref[rows, :] = _dot(hb, w_ref[:, _O_F:_O_GA]).astype(BF16)

        ta = _dot(hb, w_ref[:, _O_GA:_O_GF]) + bg_ref[:, 0:D_MODEL]
        ga_ref[rows, :] = (1.0 / (1.0 + jnp.exp(-ta))).astype(BF16)
        tf = _dot(hb, w_ref[:, _O_GF:IN_WIDTH]) + bg_ref[:, D_MODEL:2 * D_MODEL]
        gf_ref[rows, :] = (1.0 / (1.0 + jnp.exp(-tf))).astype(BF16)


def _in_proj(x2, mod3, g_pre, w_in, b_gate, gq, gk, cos_t, sin_t, ones_blk, batch, seq, tm):
    t = batch * seq
    per_b = seq // tm
    row = lambda i: (i, 0)
    const = lambda i: (0, 0)
    return pl.pallas_call(
        _in_proj_kernel,
        grid=(t // tm,),
        in_specs=[
            pl.BlockSpec((tm, D_MODEL), row),
            pl.BlockSpec((None, 6, D_MODEL), lambda i: (i // per_b, 0, 0)),
            pl.BlockSpec((1, D_MODEL), const),
            pl.BlockSpec((D_MODEL, IN_WIDTH), const),
            pl.BlockSpec((1, 2 * D_MODEL), const),
            pl.BlockSpec((1, ATTN_WIDTH), const),
            pl.BlockSpec((1, KV_WIDTH), const),
            pl.BlockSpec((tm, LANES), lambda i: (i % per_b, 0)),
            pl.BlockSpec((tm, LANES), lambda i: (i % per_b, 0)),
            pl.BlockSpec((ATTN_WIDTH, ATTN_WIDTH), const),
        ],
        out_specs=[
            pl.BlockSpec((tm, ATTN_WIDTH), row),
            pl.BlockSpec((None, KV_WIDTH, tm), lambda i: (i // per_b, 0, i % per_b)),
            pl.BlockSpec((tm, N_KV_HEADS * LANES), row),
            pl.BlockSpec((tm, FOURIER_WIDTH), row),
            pl.BlockSpec((tm, D_MODEL), row),
            pl.BlockSpec((tm, D_MODEL), row),
        ],
        out_shape=[
            jax.ShapeDtypeStruct((t, ATTN_WIDTH), BF16),
            jax.ShapeDtypeStruct((batch, KV_WIDTH, seq), BF16),
            jax.ShapeDtypeStruct((t, N_KV_HEADS * LANES), BF16),
            jax.ShapeDtypeStruct((t, FOURIER_WIDTH), BF16),
            jax.ShapeDtypeStruct((t, D_MODEL), BF16),
            jax.ShapeDtypeStruct((t, D_MODEL), BF16),
        ],
        compiler_params=_params(1),
        name="in_proj",
    )(x2, mod3, g_pre, w_in, b_gate, gq, gk, cos_t, sin_t, ones_blk)


_ATTN_SLOTS = 2


def _attn_step(q_ref, kt_ref, v_ref, o_ref, s_ref, p_new_ref, p_old_ref):
    outs = []
    vcat = v_ref[...]
    for h in range(N_Q_HEADS):
        kv = h // Q_PER_KV
        slot = h % _ATTN_SLOTS
        s_ref[slot] = _dot(q_ref[:, h * HEAD_DIM:(h + 1) * HEAD_DIM],
                           kt_ref[kv * HEAD_DIM:(kv + 1) * HEAD_DIM, :])
        s = s_ref[slot]
        m = jnp.max(s, axis=-1, keepdims=True)
        p_new_ref[h] = jnp.exp2(s - m).astype(BF16)
        ov = _dot(p_old_ref[h], vcat)[:, kv * LANES:(kv + 1) * LANES]
        on = ov / pltpu.roll(ov, HEAD_DIM, 1)
        outs.append(on[:, 0:HEAD_DIM])
    o_ref[...] = jnp.concatenate(outs, axis=-1).astype(BF16)


def _attn_kernel(q_ref, kt_ref, v_ref, o_ref, s_ref, pa_ref, pb_ref):
    g = pl.program_id(0)

    @pl.when(g == 0)
    def _():
        pb_ref[...] = jnp.ones(pb_ref.shape, BF16)

    @pl.when(g % 2 == 0)
    def _():
        _attn_step(q_ref, kt_ref, v_ref, o_ref, s_ref, pa_ref, pb_ref)

    @pl.when(g % 2 == 1)
    def _():
        _attn_step(q_ref, kt_ref, v_ref, o_ref, s_ref, pb_ref, pa_ref)


def _attention(q, kt, v, batch, seq, tq):
    t = batch * seq
    per_b = seq // tq
    n_blk = t // tq
    cur = lambda g: jnp.minimum(g, n_blk - 1)
    prev = lambda g: jnp.maximum(g - 1, 0)
    return pl.pallas_call(
        _attn_kernel,
        grid=(n_blk + 1,),
        in_specs=[
            pl.BlockSpec((tq, ATTN_WIDTH), lambda g: (cur(g), 0)),
            pl.BlockSpec((None, KV_WIDTH, seq), lambda g: (cur(g) // per_b, 0, 0)),
            pl.BlockSpec((seq, N_KV_HEADS * LANES), lambda g: (prev(g) // per_b, 0)),
        ],
        out_specs=pl.BlockSpec((tq, ATTN_WIDTH), lambda g: (prev(g), 0)),
        out_shape=jax.ShapeDtypeStruct((t, ATTN_WIDTH), BF16),
        scratch_shapes=[pltpu.VMEM((_ATTN_SLOTS, tq, seq), F32),
                        pltpu.VMEM((N_Q_HEADS, tq, seq), BF16),
                        pltpu.VMEM((N_Q_HEADS, tq, seq), BF16)],
        compiler_params=_params(1),
        name="attn",
    )(q, kt, v)


_DFT_SPLIT = 32


def _fourier_kernel(f_ref, cs_ref, ca_ref, sa_ref, cb_ref, sb_ref, o_ref, w_ref, r_ref):
    seq = f_ref.shape[0]
    gd = FOURIER_GROUP_DIM

    @pl.when(pl.program_id(0) == 0)
    def _():
        cb = cb_ref[...]
        sb = sb_ref[...]

        def body(s1, carry):
            ca = ca_ref[pl.ds(s1, 1), :]
            sa = sa_ref[pl.ds(s1, 1), :]
            rows = pl.ds(pl.multiple_of(s1 * _DFT_SPLIT, _DFT_SPLIT), _DFT_SPLIT)
            w_ref[rows, 0:seq] = (ca * cb - sa * sb).astype(BF16)
            w_ref[rows, seq:2 * seq] = (-(sa * cb + ca * sb)).astype(BF16)
            return carry

        lax.fori_loop(0, seq // _DFT_SPLIT, body, 0)

    cs = cs_ref[...].astype(BF16)
    for g in range(N_FOURIER_GROUPS):
        xcs = _dot(f_ref[:, g * gd:(g + 1) * gd], cs)
        r_ref[0:seq, g * gd:(g + 1) * gd] = xcs[:, 0:gd].astype(BF16)
        r_ref[seq:2 * seq, g * gd:(g + 1) * gd] = xcs[:, gd:2 * gd].astype(BF16)
    tm = 512
    for mi in range(seq // tm):
        o_ref[mi * tm:(mi + 1) * tm, :] = _dot(
            w_ref[mi * tm:(mi + 1) * tm, :], r_ref[...]).astype(BF16)


def _fourier(f, cs_c, ca, sa, cb, sb, batch, seq):
    t = batch * seq
    const = lambda b: (0, 0)
    return pl.pallas_call(
        _fourier_kernel,
        grid=(batch,),
        in_specs=[
            pl.BlockSpec((seq, FOURIER_WIDTH), lambda b: (b, 0)),
            pl.BlockSpec(cs_c.shape, const),
            pl.BlockSpec(ca.shape, const),
            pl.BlockSpec(sa.shape, const),
            pl.BlockSpec(cb.shape, const),
            pl.BlockSpec(sb.shape, const),
        ],
        out_specs=pl.BlockSpec((seq, FOURIER_WIDTH), lambda b: (b, 0)),
        out_shape=jax.ShapeDtypeStruct((t, FOURIER_WIDTH), BF16),
        scratch_shapes=[pltpu.VMEM((seq, 2 * seq), BF16),
                        pltpu.VMEM((2 * seq, FOURIER_WIDTH), BF16)],
        compiler_params=_params(1),
        name="fourier",
    )(f, cs_c, ca, sa, cb, sb)


def _merge_kernel(o_ref, yf_ref, ga_ref, gf_ref, x_ref, mod_ref, wa_ref, wf_ref, wo_ref,
                  gpost_ref, gpre2_ref, x1_ref, h2_ref):
    gate1 = mod_ref[2:3, :]
    shift2 = mod_ref[3:4, :]
    scale2 = mod_ref[4:5, :]
    gpost = gate1 * gpost_ref[...]
    gmod2 = gpre2_ref[...] * (1.0 + scale2)
    for r in range(x_ref.shape[0] // _ROW_CHUNK):
        rows = slice(r * _ROW_CHUNK, (r + 1) * _ROW_CHUNK)
        ya = _dot(o_ref[rows, :], wa_ref[...])
        yf = _dot(yf_ref[rows, :], wf_ref[...])
        y = (ga_ref[rows, :].astype(F32) * ya + gf_ref[rows, :].astype(F32) * yf).astype(BF16)
        y2 = _dot(y, wo_ref[...])
        x1 = x_ref[rows, :] + (y2 * _rms_scale(y2)) * gpost
        x1_ref[rows, :] = x1
        h2_ref[rows, :] = ((x1 * _rms_scale(x1)) * gmod2 + shift2).astype(BF16)


def _merge(o, yf, ga, gf, x2, mod3, wa, wf, wo, g_post, g_pre2, seq, tm):
    t = x2.shape[0]
    per_b = seq // tm
    row = lambda i: (i, 0)
    const = lambda i: (0, 0)
    return pl.pallas_call(
        _merge_kernel,
        grid=(t // tm,),
        in_specs=[
            pl.BlockSpec((tm, ATTN_WIDTH), row),
            pl.BlockSpec((tm, FOURIER_WIDTH), row),
            pl.BlockSpec((tm, D_MODEL), row),
            pl.BlockSpec((tm, D_MODEL), row),
            pl.BlockSpec((tm, D_MODEL), row),
            pl.BlockSpec((None, 6, D_MODEL), lambda i: (i // per_b, 0, 0)),
            pl.BlockSpec((ATTN_WIDTH, D_MODEL), const),
            pl.BlockSpec((FOURIER_WIDTH, D_MODEL), const),
            pl.BlockSpec((D_MODEL, D_MODEL), const),
            pl.BlockSpec((1, D_MODEL), const),
            pl.BlockSpec((1, D_MODEL), const),
        ],
        out_specs=[pl.BlockSpec((tm, D_MODEL), row), pl.BlockSpec((tm, D_MODEL), row)],
        out_shape=[jax.ShapeDtypeStruct((t, D_MODEL), F32),
                   jax.ShapeDtypeStruct((t, D_MODEL), BF16)],
        compiler_params=_params(1),
        name="merge",
    )(o, yf, ga, gf, x2, mod3, wa, wf, wo, g_post, g_pre2)


_HALO = BF16_SUBLANES
_MXU_COLS = 256
_GELU_K1 = math.sqrt(2.0 / math.pi)
_GELU_K2 = _GELU_K1 * 0.044715


def _ffn_up_proj(h_ref, hp_ref, hn_ref, w_ref, hx_ref, hperm_ref, tiles_per_seq):
    tm = h_ref.shape[0]
    ext = tm + 2 * _HALO
    pitch = ext // 8
    i = pl.program_id(0)
    keep_prev = jnp.where(i % tiles_per_seq != 0, 1.0, 0.0)
    keep_next = jnp.where(i % tiles_per_seq != tiles_per_seq - 1, 1.0, 0.0)

    for c in range(D_MODEL // LANES):
        lc = slice(c * LANES, (c + 1) * LANES)
        hx_ref[c, 0:_HALO, :] = hp_ref[:, lc].astype(F32) * keep_prev
        hx_ref[c, _HALO:_HALO + tm, :] = h_ref[:, lc].astype(F32)
        hx_ref[c, _HALO + tm:ext, :] = hn_ref[:, lc].astype(F32) * keep_next
    for c in range(D_MODEL // LANES):
        lc = slice(c * LANES, (c + 1) * LANES)
        for b in range(0, pitch, 2):
            pair = jnp.concatenate([hx_ref[c, pl.ds(b, 8, stride=pitch), :],
                                    hx_ref[c, pl.ds(b + 1, 8, stride=pitch), :]], axis=0)
            hperm_ref[8 * b:8 * b + 16, lc] = pair.astype(BF16)
    hperm = hperm_ref[...]

    def pair_thunk(jj):
        def run(u_ref):
            for half in range(2):
                cols = slice(half * D_FF + jj * _MXU_COLS, half * D_FF + (jj + 1) * _MXU_COLS)
                u_ref[:, cols] = _dot(hperm, w_ref[:, cols])
        return run

    return [pair_thunk(jj) for jj in range(D_FF // _MXU_COLS)]


def _ffn_up_conv(u_ref, cw_ref, cb_ref, o_ref, st_ref, jj):
    tm = o_ref.shape[0]
    ext = tm + 2 * _HALO
    pitch = ext // 8
    for k in range(_MXU_COLS // LANES):
        j = 2 * jj + k
        la = slice(j * LANES, (j + 1) * LANES)
        lb = slice(D_FF + j * LANES, D_FF + (j + 1) * LANES)
        bc = lambda ref, r, l: jnp.broadcast_to(ref[r:r + 1, l], (8, LANES))
        wa = [bc(cw_ref, r, la) for r in range(CONV_WIDTH)]
        wb = [bc(cw_ref, r, lb) for r in range(CONV_WIDTH)]
        ba = bc(cb_ref, 0, la)
        bb = bc(cb_ref, 0, lb)

        def vreg(lanes, b):
            if b < 0:
                return pltpu.roll(u_ref[8 * (pitch - 1):8 * pitch, lanes], 1, 0)
            if b >= pitch:
                return pltpu.roll(u_ref[0:8, lanes], 7, 0)
            return u_ref[8 * b:8 * b + 8, lanes]

        for b in range(pitch):
            ca = vreg(la, b - 1) * wa[0] + vreg(la, b) * wa[1] + vreg(la, b + 1) * wa[2] + ba
            cv = vreg(lb, b - 1) * wb[0] + vreg(lb, b) * wb[1] + vreg(lb, b + 1) * wb[2] + bb
            t = jnp.tanh(ca * (_GELU_K1 + _GELU_K2 * (ca * ca)))
            st_ref[j % 2, pl.ds(b, 8, stride=pitch), :] = (ca * cv) * (1.0 + t)
        o_ref[:, la] = st_ref[j % 2, _HALO:_HALO + tm, :].astype(BF16)


def _ffn_up_kernel(h_ref, hp_ref, hn_ref, w_ref, cw_ref, cb_ref, o_ref,
                   hx_ref, hperm_ref, st_ref, u0_ref, u1_ref, *, tiles_per_seq, n_tiles):
    i = pl.program_id(0)
    u_refs = (u0_ref, u1_ref)
    n_pairs = D_FF // _MXU_COLS

    @pl.when(i == 0)
    def _():
        for thunk in _ffn_up_proj(h_ref, hp_ref, hn_ref, w_ref, hx_ref, hperm_ref, tiles_per_seq):
            thunk(u0_ref)

    for par in range(2):
        @pl.when((i > 0) & (i < n_tiles) & (i % 2 == par))
        def _(par=par):
            thunks = _ffn_up_proj(h_ref, hp_ref, hn_ref, w_ref, hx_ref, hperm_ref, tiles_per_seq)
            for jj in range(n_pairs):
                thunks[jj](u_refs[par])
                _ffn_up_conv(u_refs[1 - par], cw_ref, cb_ref, o_ref, st_ref, jj)

    @pl.when(i == n_tiles)
    def _():
        for jj in range(n_pairs):
            _ffn_up_conv(u_refs[(n_tiles - 1) % 2], cw_ref, cb_ref, o_ref, st_ref, jj)


def _ffn_up(h2, w_up, conv_w, conv_b, seq, tm):
    t = h2.shape[0]
    n_tiles = t // tm
    tiles_per_seq = seq // tm
    halo_per_tile = tm // _HALO
    n_halo = t // _HALO
    ext = tm + 2 * _HALO
    const = lambda i: (0, 0)
    cur = lambda i: jnp.minimum(i, n_tiles - 1)
    return pl.pallas_call(
        functools.partial(_ffn_up_kernel, tiles_per_seq=tiles_per_seq, n_tiles=n_tiles),
        grid=(n_tiles + 1,),
        in_specs=[
            pl.BlockSpec((tm, D_MODEL), lambda i: (cur(i), 0)),
            pl.BlockSpec((_HALO, D_MODEL),
                         lambda i: (jnp.maximum(cur(i) * halo_per_tile - 1, 0), 0)),
            pl.BlockSpec((_HALO, D_MODEL),
                         lambda i: (jnp.minimum((cur(i) + 1) * halo_per_tile, n_halo - 1), 0)),
            pl.BlockSpec((D_MODEL, 2 * D_FF), const),
            pl.BlockSpec((CONV_WIDTH, 2 * D_FF), const),
            pl.BlockSpec((1, 2 * D_FF), const),
        ],
        out_specs=pl.BlockSpec((tm, D_FF), lambda i: (jnp.maximum(i - 1, 0), 0)),
        out_shape=jax.ShapeDtypeStruct((t, D_FF), BF16),
        scratch_shapes=[pltpu.VMEM((D_MODEL // LANES, ext, LANES), F32),
                        pltpu.VMEM((ext, D_MODEL), BF16),
                        pltpu.VMEM((2, ext, LANES), F32),
                        pltpu.VMEM((ext, 2 * D_FF), F32),
                        pltpu.VMEM((ext, 2 * D_FF), F32)],
        compiler_params=_params(1),
        name="ffn_up",
    )(h2, h2, h2, w_up, conv_w, conv_b)


def _ffn_down_kernel(a_ref, w_ref, x1_ref, mod_ref, g_ref, o_ref):
    gpost = mod_ref[5:6, :] * g_ref[...]
    for r in range(a_ref.shape[0] // _ROW_CHUNK):
        rows = slice(r * _ROW_CHUNK, (r + 1) * _ROW_CHUNK)
        y = _dot(a_ref[rows, :], w_ref[...])
        o_ref[rows, :] = x1_ref[rows, :] + (y * _rms_scale(y)) * gpost


def _ffn_down(act, w_down, x1, mod3, g_post, seq, tm):
    t = x1.shape[0]
    per_b = seq // tm
    row = lambda i: (i, 0)
    const = lambda i: (0, 0)
    return pl.pallas_call(
        _ffn_down_kernel,
        grid=(t // tm,),
        in_specs=[
            pl.BlockSpec((tm, D_FF), row),
            pl.BlockSpec((D_FF, D_MODEL), const),
            pl.BlockSpec((tm, D_MODEL), row),
            pl.BlockSpec((None, 6, D_MODEL), lambda i: (i // per_b, 0, 0)),
            pl.BlockSpec((1, D_MODEL), const),
        ],
        out_specs=pl.BlockSpec((tm, D_MODEL), row),
        out_shape=jax.ShapeDtypeStruct((t, D_MODEL), F32),
        compiler_params=_params(1),
        name="ffn_down",
    )(act, w_down, x1, mod3, g_post)


@functools.lru_cache(maxsize=None)
def _rope_tables(seq):
    pos = np.arange(seq)
    row = (pos // GRID_W).astype(np.float64)
    col = (pos % GRID_W).astype(np.float64)
    inv = ROPE_THETA ** (-np.arange(ROPE_PAIRS_PER_AXIS, dtype=np.float64) / ROPE_PAIRS_PER_AXIS)
    ang = np.concatenate([row[:, None] * inv, col[:, None] * inv], axis=-1)
    cos = np.repeat(np.cos(ang), 2, axis=-1)
    sin = np.repeat(np.sin(ang), 2, axis=-1)
    sign = np.tile(np.array([-1.0, 1.0]), HEAD_DIM // 2)
    reps = LANES // HEAD_DIM
    return (np.tile(cos, (1, reps)).astype(np.float32),
            np.tile(sin * sign, (1, reps)).astype(np.float32))


@functools.lru_cache(maxsize=None)
def _dft_tables(seq):
    gd = FOURIER_GROUP_DIM
    kc = (np.outer(np.arange(gd), np.arange(gd)) % gd).astype(np.float64) * (2.0 * np.pi / gd)
    norm = 1.0 / math.sqrt(seq * gd)
    cs_c = np.concatenate([np.cos(kc), np.sin(kc)], axis=1) * norm
    n1 = seq // _DFT_SPLIT
    t_idx = np.arange(seq)
    ang_a = (np.outer(np.arange(n1), t_idx) % n1).astype(np.float64) * (2.0 * np.pi / n1)
    ang_b = (np.outer(np.arange(_DFT_SPLIT), t_idx) % seq).astype(np.float64) * (2.0 * np.pi / seq)
    f32 = lambda a: a.astype(np.float32)
    return f32(cs_c), f32(np.cos(ang_a)), f32(np.sin(ang_a)), f32(np.cos(ang_b)), f32(np.sin(ang_b))


def kernel(x, c, w_ada, b_ada, mix_pre_g, w_in, q_norm_g, k_norm_g, b_gate, w_attn_branch,
           w_fourier_branch, w_out, mix_post_g, ffn_pre_g, w_up, conv_w, conv_b, w_down,
           ffn_post_g):
    batch, seq, _ = x.shape
    depth = w_ada.shape[0]
    t = batch * seq
    tiles = _TILES

    cos_np, sin_np = _rope_tables(seq)
    cos_t = jnp.asarray(cos_np)
    sin_t = jnp.asarray(sin_np)
    dft_tabs = [jnp.asarray(a) for a in _dft_tables(seq)]
    seg = np.arange(ATTN_WIDTH) // HEAD_DIM
    ones_blk = jnp.asarray((seg[:, None] == seg[None, :]).astype(np.float32), dtype=BF16)

    x2 = x.reshape(t, D_MODEL)
    for l in range(depth):
        mod3 = _adaln(c, w_ada[l], b_ada[l]).reshape(batch, 6, D_MODEL)
        gq = (jnp.tile(q_norm_g[l], N_Q_HEADS) * (HEAD_DIM ** -0.5 * _LOG2_E)).reshape(1, ATTN_WIDTH)
        gk = jnp.tile(k_norm_g[l], N_KV_HEADS).reshape(1, KV_WIDTH)
        q, kt, v, f, ga, gf = _in_proj(
            x2, mod3, mix_pre_g[l].reshape(1, D_MODEL), w_in[l].astype(BF16),
            b_gate[l].reshape(1, 2 * D_MODEL), gq, gk, cos_t, sin_t, ones_blk, batch, seq,
            tiles["in_proj"])
        o = _attention(q, kt, v, batch, seq, tiles["attn"])
        yf = _fourier(f, *dft_tabs, batch, seq)
        x1, h2 = _merge(
            o, yf, ga, gf, x2, mod3, w_attn_branch[l].astype(BF16),
            w_fourier_branch[l].astype(BF16), w_out[l].astype(BF16),
            mix_post_g[l].reshape(1, D_MODEL), ffn_pre_g[l].reshape(1, D_MODEL), seq,
            tiles["merge"])
        glu_half = jnp.concatenate([jnp.ones((D_FF,), F32), jnp.full((D_FF,), 0.5, F32)])
        act = _ffn_up(h2, w_up[l].astype(BF16), conv_w[l] * glu_half,
                      (conv_b[l] * glu_half).reshape(1, 2 * D_FF), seq, tiles["ffn_up"])
        x2 = _ffn_down(act, w_down[l].astype(BF16), x1, mod3,
                       ffn_post_g[l].reshape(1, D_MODEL), seq, tiles["ffn_down"])
    return x2.reshape(batch, seq, D_MODEL)
```

```python
import functools
import math

import numpy as np
import jax
import jax.numpy as jnp
from jax import lax
from jax.experimental import pallas as pl
from jax.experimental.pallas import tpu as pltpu

D_MODEL = 1024
GRID_W = 64
HEAD_DIM = 64
N_Q_HEADS = 8
N_KV_HEADS = 2
Q_PER_KV = N_Q_HEADS // N_KV_HEADS
ATTN_WIDTH = N_Q_HEADS * HEAD_DIM
KV_WIDTH = N_KV_HEADS * HEAD_DIM
N_FOURIER_GROUPS = 4
FOURIER_GROUP_DIM = 128
FOURIER_WIDTH = N_FOURIER_GROUPS * FOURIER_GROUP_DIM
IN_WIDTH = ATTN_WIDTH + 2 * KV_WIDTH + FOURIER_WIDTH + 2 * D_MODEL
ROPE_THETA = 10000.0
ROPE_PAIRS_PER_AXIS = HEAD_DIM // 4
D_FF = 2816
CONV_WIDTH = 3
NORM_EPS = 1e-6

LANES = 128
BF16_SUBLANES = 16
VMEM_LIMIT_BYTES = 56 * 1024 * 1024
_ROW_CHUNK = 256
_TILES = {"in_proj": 512, "attn": 256, "merge": 1024, "ffn_up": 512, "ffn_down": 1024}

_LOG2_E = math.log2(math.e)

F32 = jnp.float32
BF16 = jnp.bfloat16

_O_K = ATTN_WIDTH
_O_V = _O_K + KV_WIDTH
_O_F = _O_V + KV_WIDTH
_O_GA = _O_F + FOURIER_WIDTH
_O_GF = _O_GA + D_MODEL


def _params(n_grid_dims, flags=None):
    return pltpu.CompilerParams(
        dimension_semantics=("arbitrary",) * n_grid_dims,
        vmem_limit_bytes=VMEM_LIMIT_BYTES,
        flags=flags,
    )


def _dot(a, b):
    return jnp.dot(a, b, preferred_element_type=F32)


def _rms_scale(v):
    return lax.rsqrt(jnp.mean(v * v, axis=-1, keepdims=True) + NORM_EPS)


def _adaln_kernel(c_ref, w_ref, b_ref, o_ref):
    c = c_ref[...]
    a = c / (1.0 + jnp.exp(-c))
    a_hi = a.astype(BF16)
    a_lo = (a - a_hi.astype(F32)).astype(BF16)
    w = w_ref[...]
    w_hi = w.astype(BF16)
    w_lo = (w - w_hi.astype(F32)).astype(BF16)
    o_ref[...] = _dot(a_hi, w_hi) + (_dot(a_lo, w_hi) + _dot(a_hi, w_lo)) + b_ref[...]


def _adaln(c, w_ada, b_ada):
    batch = c.shape[0]
    n = w_ada.shape[1]
    tn = 1536
    return pl.pallas_call(
        _adaln_kernel,
        grid=(n // tn,),
        in_specs=[
            pl.BlockSpec((batch, D_MODEL), lambda j: (0, 0)),
            pl.BlockSpec((D_MODEL, tn), lambda j: (0, j)),
            pl.BlockSpec((1, tn), lambda j: (0, j)),
        ],
        out_specs=pl.BlockSpec((batch, tn), lambda j: (0, j)),
        out_shape=jax.ShapeDtypeStruct((batch, n), F32),
        compiler_params=_params(1),
        name="adaln",
    )(c, w_ada, b_ada.reshape(1, n))


def _swap_pairs(v):
    n = v.shape[-1]
    lane = lax.broadcasted_iota(jnp.int32, v.shape, v.ndim - 1)
    from_right = pltpu.roll(v, n - 1, v.ndim - 1)
    from_left = pltpu.roll(v, 1, v.ndim - 1)
    return jnp.where((lane & 1) == 0, from_right, from_left)


def _head_norm_rope(z, gain, ones_blk, cos, sin):
    reps = z.shape[-1] // LANES
    ssq = _dot((z * z).astype(BF16), ones_blk)
    r = lax.rsqrt(ssq * (1.0 / HEAD_DIM) + NORM_EPS)
    zg = z * gain
    cos_t = jnp.concatenate([cos] * reps, axis=-1) if reps > 1 else cos
    sin_t = jnp.concatenate([sin] * reps, axis=-1) if reps > 1 else sin
    return (zg * cos_t + _swap_pairs(zg) * sin_t) * r


def _in_proj_kernel(x_ref, mod_ref, g_ref, w_ref, bg_ref, gq_ref, gk_ref,
                    cos_ref, sin_ref, ones_ref,
                    qt_ref, kp_ref, vt_ref, f_ref, ga_ref, gf_ref):
    shift = mod_ref[0:1, :]
    scale = mod_ref[1:2, :]
    gmod = g_ref[...] * (1.0 + scale)
    for r in range(x_ref.shape[0] // _ROW_CHUNK):
        rows = slice(r * _ROW_CHUNK, (r + 1) * _ROW_CHUNK)
        x = x_ref[rows, :]
        hb = ((x * _rms_scale(x)) * gmod + shift).astype(BF16)
        cos = cos_ref[rows, :]
        sin = sin_ref[rows, :]

        zq = _dot(hb, w_ref[:, 0:_O_K])
        q = _head_norm_rope(zq, gq_ref[...], ones_ref[...], cos, sin)
        qt_ref[:, rows] = q.T.astype(BF16)

        zkv = _dot(hb, w_ref[:, _O_K:_O_F])
        k = _head_norm_rope(zkv[:, 0:KV_WIDTH], gk_ref[...],
                            ones_ref[0:KV_WIDTH, 0:KV_WIDTH], cos, sin)
        k_sw = pltpu.roll(k, HEAD_DIM, 1)
        first = lax.broadcasted_iota(jnp.int32, k.shape, 1) < HEAD_DIM
        kp_ref[rows, :] = jnp.concatenate(
            [jnp.where(first, k, 0.0), jnp.where(first, 0.0, k_sw),
             jnp.where(first, k_sw, 0.0), jnp.where(first, 0.0, k)], axis=-1).astype(BF16)
        zvt = zkv[:, KV_WIDTH:2 * KV_WIDTH].T
        ones_rows = jnp.ones((HEAD_DIM, zvt.shape[1]), F32)
        vt_ref[:, rows] = jnp.concatenate(
            [piece for kv in range(N_KV_HEADS)
             for piece in (zvt[kv * HEAD_DIM:(kv + 1) * HEAD_DIM, :], ones_rows)],
            axis=0).astype(BF16)

        f_ref[rows, :] = _dot(hb, w_ref[:, _O_F:_O_GA]).astype(BF16)

        ta = _dot(hb, w_ref[:, _O_GA:_O_GF]) + bg_ref[:, 0:D_MODEL]
        ga_ref[rows, :] = (1.0 / (1.0 + jnp.exp(-ta))).astype(BF16)
        tf = _dot(hb, w_ref[:, _O_GF:IN_WIDTH]) + bg_ref[:, D_MODEL:2 * D_MODEL]
        gf_ref[rows, :] = (1.0 / (1.0 + jnp.exp(-tf))).astype(BF16)


def _in_proj(x2, mod3, g_pre, w_in, b_gate, gq, gk, cos_t, sin_t, ones_blk, batch, seq, tm):
    t = batch * seq
    per_b = seq // tm
    row = lambda i: (i, 0)
    const = lambda i: (0, 0)
    return pl.pallas_call(
        _in_proj_kernel,
        grid=(t // tm,),
        in_specs=[
            pl.BlockSpec((tm, D_MODEL), row),
            pl.BlockSpec((None, 6, D_MODEL), lambda i: (i // per_b, 0, 0)),
            pl.BlockSpec((1, D_MODEL), const),
            pl.BlockSpec((D_MODEL, IN_WIDTH), const),
            pl.BlockSpec((1, 2 * D_MODEL), const),
            pl.BlockSpec((1, ATTN_WIDTH), const),
            pl.BlockSpec((1, KV_WIDTH), const),
            pl.BlockSpec((tm, LANES), lambda i: (i % per_b, 0)),
            pl.BlockSpec((tm, LANES), lambda i: (i % per_b, 0)),
            pl.BlockSpec((ATTN_WIDTH, ATTN_WIDTH), const),
        ],
        out_specs=[
            pl.BlockSpec((None, ATTN_WIDTH, tm), lambda i: (i // per_b, 0, i % per_b)),
            pl.BlockSpec((tm, 2 * N_KV_HEADS * LANES), row),
            pl.BlockSpec((None, N_KV_HEADS * LANES, tm), lambda i: (i // per_b, 0, i % per_b)),
            pl.BlockSpec((tm, FOURIER_WIDTH), row),
            pl.BlockSpec((tm, D_MODEL), row),
            pl.BlockSpec((tm, D_MODEL), row),
        ],
        out_shape=[
            jax.ShapeDtypeStruct((batch, ATTN_WIDTH, seq), BF16),
            jax.ShapeDtypeStruct((t, 2 * N_KV_HEADS * LANES), BF16),
            jax.ShapeDtypeStruct((batch, N_KV_HEADS * LANES, seq), BF16),
            jax.ShapeDtypeStruct((t, FOURIER_WIDTH), BF16),
            jax.ShapeDtypeStruct((t, D_MODEL), BF16),
            jax.ShapeDtypeStruct((t, D_MODEL), BF16),
        ],
        compiler_params=_params(1),
        name="in_proj",
    )(x2, mod3, g_pre, w_in, b_gate, gq, gk, cos_t, sin_t, ones_blk)


_ATTN_SLOTS = 2


def _attn_step(qt_ref, kp_ref, vt_ref, o_ref, p_ref, s_new_ref, s_old_ref):
    outs = []
    for h in range(N_Q_HEADS):
        kv = h // Q_PER_KV
        slot = h % _ATTN_SLOTS
        pair = h // 2
        variant = 2 * kv + h % 2
        s_new_ref[h] = _dot(kp_ref[:, variant * LANES:(variant + 1) * LANES],
                            qt_ref[pair * LANES:(pair + 1) * LANES, :])
        s = s_old_ref[h]
        m = jnp.max(s, axis=0, keepdims=True)
        p_ref[slot] = jnp.exp2(s - m).astype(BF16)
        ov = _dot(vt_ref[kv * LANES:(kv + 1) * LANES, :], p_ref[slot])
        on = ov[0:HEAD_DIM, :] / ov[HEAD_DIM:2 * HEAD_DIM, :]
        outs.append(on.T)
    o_ref[...] = jnp.concatenate(outs, axis=-1).astype(BF16)


def _attn_kernel(qt_ref, kp_ref, vt_ref, o_ref, p_ref, sa_ref, sb_ref):
    g = pl.program_id(0)

    @pl.when(g == 0)
    def _():
        sb_ref[...] = jnp.zeros(sb_ref.shape, F32)

    @pl.when(g % 2 == 0)
    def _():
        _attn_step(qt_ref, kp_ref, vt_ref, o_ref, p_ref, sa_ref, sb_ref)

    @pl.when(g % 2 == 1)
    def _():
        _attn_step(qt_ref, kp_ref, vt_ref, o_ref, p_ref, sb_ref, sa_ref)


def _attention(qt, kp, vt, batch, seq, tq):
    t = batch * seq
    per_b = seq // tq
    n_blk = t // tq
    cur = lambda g: jnp.minimum(g, n_blk - 1)
    prev = lambda g: jnp.maximum(g - 1, 0)
    return pl.pallas_call(
        _attn_kernel,
        grid=(n_blk + 1,),
        in_specs=[
            pl.BlockSpec((None, ATTN_WIDTH, tq), lambda g: (cur(g) // per_b, 0, cur(g) % per_b)),
            pl.BlockSpec((seq, 2 * N_KV_HEADS * LANES), lambda g: (cur(g) // per_b, 0)),
            pl.BlockSpec((None, N_KV_HEADS * LANES, seq), lambda g: (prev(g) // per_b, 0, 0)),
        ],
        out_specs=pl.BlockSpec((tq, ATTN_WIDTH), lambda g: (prev(g), 0)),
        out_shape=jax.ShapeDtypeStruct((t, ATTN_WIDTH), BF16),
        scratch_shapes=[pltpu.VMEM((_ATTN_SLOTS, seq, tq), BF16),
                        pltpu.VMEM((N_Q_HEADS, seq, tq), F32),
                        pltpu.VMEM((N_Q_HEADS, seq, tq), F32)],
        compiler_params=_params(1),
        name="attn",
    )(qt, kp, vt)


_DFT_SPLIT = 32


def _fourier_kernel(f_ref, cs_ref, ca_ref, sa_ref, cb_ref, sb_ref, o_ref, w_ref, r_ref):
    seq = f_ref.shape[0]
    gd = FOURIER_GROUP_DIM

    @pl.when(pl.program_id(0) == 0)
    def _():
        cb = cb_ref[...]
        sb = sb_ref[...]

        def body(s1, carry):
            ca = ca_ref[pl.ds(s1, 1), :]
            sa = sa_ref[pl.ds(s1, 1), :]
            rows = pl.ds(pl.multiple_of(s1 * _DFT_SPLIT, _DFT_SPLIT), _DFT_SPLIT)
            w_ref[rows, 0:seq] = (ca * cb - sa * sb).astype(BF16)
            w_ref[rows, seq:2 * seq] = (-(sa * cb + ca * sb)).astype(BF16)
            return carry

        lax.fori_loop(0, seq // _DFT_SPLIT, body, 0)

    cs = cs_ref[...].astype(BF16)
    for g in range(N_FOURIER_GROUPS):
        xcs = _dot(f_ref[:, g * gd:(g + 1) * gd], cs)
        r_ref[0:seq, g * gd:(g + 1) * gd] = xcs[:, 0:gd].astype(BF16)
        r_ref[seq:2 * seq, g * gd:(g + 1) * gd] = xcs[:, gd:2 * gd].astype(BF16)
    tm = 512
    for mi in range(seq // tm):
        o_ref[mi * tm:(mi + 1) * tm, :] = _dot(
            w_ref[mi * tm:(mi + 1) * tm, :], r_ref[...]).astype(BF16)


def _fourier(f, cs_c, ca, sa, cb, sb, batch, seq):
    t = batch * seq
    const = lambda b: (0, 0)
    return pl.pallas_call(
        _fourier_kernel,
        grid=(batch,),
        in_specs=[
            pl.BlockSpec((seq, FOURIER_WIDTH), lambda b: (b, 0)),
            pl.BlockSpec(cs_c.shape, const),
            pl.BlockSpec(ca.shape, const),
            pl.BlockSpec(sa.shape, const),
            pl.BlockSpec(cb.shape, const),
            pl.BlockSpec(sb.shape, const),
        ],
        out_specs=pl.BlockSpec((seq, FOURIER_WIDTH), lambda b: (b, 0)),
        out_shape=jax.ShapeDtypeStruct((t, FOURIER_WIDTH), BF16),
        scratch_shapes=[pltpu.VMEM((seq, 2 * seq), BF16),
                        pltpu.VMEM((2 * seq, FOURIER_WIDTH), BF16)],
        compiler_params=_params(1),
        name="fourier",
    )(f, cs_c, ca, sa, cb, sb)


def _merge_kernel(o_ref, yf_ref, ga_ref, gf_ref, x_ref, mod_ref, wa_ref, wf_ref, wo_ref,
                  gpost_ref, gpre2_ref, x1_ref, h2_ref):
    gate1 = mod_ref[2:3, :]
    shift2 = mod_ref[3:4, :]
    scale2 = mod_ref[4:5, :]
    gpost = gate1 * gpost_ref[...]
    gmod2 = gpre2_ref[...] * (1.0 + scale2)
    for r in range(x_ref.shape[0] // _ROW_CHUNK):
        rows = slice(r * _ROW_CHUNK, (r + 1) * _ROW_CHUNK)
        ya = _dot(o_ref[rows, :], wa_ref[...])
        yf = _dot(yf_ref[rows, :], wf_ref[...])
        y = (ga_ref[rows, :].astype(F32) * ya + gf_ref[rows, :].astype(F32) * yf).astype(BF16)
        y2 = _dot(y, wo_ref[...])
        x1 = x_ref[rows, :] + (y2 * _rms_scale(y2)) * gpost
        x1_ref[rows, :] = x1
        h2_ref[rows, :] = ((x1 * _rms_scale(x1)) * gmod2 + shift2).astype(BF16)


def _merge(o, yf, ga, gf, x2, mod3, wa, wf, wo, g_post, g_pre2, seq, tm):
    t = x2.shape[0]
    per_b = seq // tm
    row = lambda i: (i, 0)
    const = lambda i: (0, 0)
    return pl.pallas_call(
        _merge_kernel,
        grid=(t // tm,),
        in_specs=[
            pl.BlockSpec((tm, ATTN_WIDTH), row),
            pl.BlockSpec((tm, FOURIER_WIDTH), row),
            pl.BlockSpec((tm, D_MODEL), row),
            pl.BlockSpec((tm, D_MODEL), row),
            pl.BlockSpec((tm, D_MODEL), row),
            pl.BlockSpec((None, 6, D_MODEL), lambda i: (i // per_b, 0, 0)),
            pl.BlockSpec((ATTN_WIDTH, D_MODEL), const),
            pl.BlockSpec((FOURIER_WIDTH, D_MODEL), const),
            pl.BlockSpec((D_MODEL, D_MODEL), const),
            pl.BlockSpec((1, D_MODEL), const),
            pl.BlockSpec((1, D_MODEL), const),
        ],
        out_specs=[pl.BlockSpec((tm, D_MODEL), row), pl.BlockSpec((tm, D_MODEL), row)],
        out_shape=[jax.ShapeDtypeStruct((t, D_MODEL), F32),
                   jax.ShapeDtypeStruct((t, D_MODEL), BF16)],
        compiler_params=_params(1),
        name="merge",
    )(o, yf, ga, gf, x2, mod3, wa, wf, wo, g_post, g_pre2)


_HALO = BF16_SUBLANES
_MXU_COLS = 256
_GELU_K1 = math.sqrt(2.0 / math.pi)
_GELU_K2 = _GELU_K1 * 0.044715


def _ffn_up_proj(h_ref, hp_ref, hn_ref, w_ref, hx_ref, hperm_ref, tiles_per_seq):
    tm = h_ref.shape[0]
    ext = tm + 2 * _HALO
    pitch = ext // 8
    i = pl.program_id(0)
    keep_prev = jnp.where(i % tiles_per_seq != 0, 1.0, 0.0)
    keep_next = jnp.where(i % tiles_per_seq != tiles_per_seq - 1, 1.0, 0.0)

    for c in range(D_MODEL // LANES):
        lc = slice(c * LANES, (c + 1) * LANES)
        hx_ref[c, 0:_HALO, :] = hp_ref[:, lc].astype(F32) * keep_prev
        hx_ref[c, _HALO:_HALO + tm, :] = h_ref[:, lc].astype(F32)
        hx_ref[c, _HALO + tm:ext, :] = hn_ref[:, lc].astype(F32) * keep_next
    for c in range(D_MODEL // LANES):
        lc = slice(c * LANES, (c + 1) * LANES)
        for b in range(0, pitch, 2):
            pair = jnp.concatenate([hx_ref[c, pl.ds(b, 8, stride=pitch), :],
                                    hx_ref[c, pl.ds(b + 1, 8, stride=pitch), :]], axis=0)
            hperm_ref[8 * b:8 * b + 16, lc] = pair.astype(BF16)
    hperm = hperm_ref[...]

    def pair_thunk(jj):
        def run(u_ref):
            for half in range(2):
                cols = slice(half * D_FF + jj * _MXU_COLS, half * D_FF + (jj + 1) * _MXU_COLS)
                u_ref[:, cols] = _dot(hperm, w_ref[:, cols])
        return run

    return [pair_thunk(jj) for jj in range(D_FF // _MXU_COLS)]


def _ffn_up_conv(u_ref, cw_ref, cb_ref, o_ref, st_ref, jj):
    tm = o_ref.shape[0]
    ext = tm + 2 * _HALO
    pitch = ext // 8
    for k in range(_MXU_COLS // LANES):
        j = 2 * jj + k
        la = slice(j * LANES, (j + 1) * LANES)
        lb = slice(D_FF + j * LANES, D_FF + (j + 1) * LANES)
        bc = lambda ref, r, l: jnp.broadcast_to(ref[r:r + 1, l], (8, LANES))
        wa = [bc(cw_ref, r, la) for r in range(CONV_WIDTH)]
        wb = [bc(cw_ref, r, lb) for r in range(CONV_WIDTH)]
        ba = bc(cb_ref, 0, la)
        bb = bc(cb_ref, 0, lb)

        def vreg(lanes, b):
            if b < 0:
                return pltpu.roll(u_ref[8 * (pitch - 1):8 * pitch, lanes], 1, 0)
            if b >= pitch:
                return pltpu.roll(u_ref[0:8, lanes], 7, 0)
            return u_ref[8 * b:8 * b + 8, lanes]

        for b in range(pitch):
            ca = vreg(la, b - 1) * wa[0] + vreg(la, b) * wa[1] + vreg(la, b + 1) * wa[2] + ba
            cv = vreg(lb, b - 1) * wb[0] + vreg(lb, b) * wb[1] + vreg(lb, b + 1) * wb[2] + bb
            t = jnp.tanh(ca * (_GELU_K1 + _GELU_K2 * (ca * ca)))
            st_ref[j % 2, pl.ds(b, 8, stride=pitch), :] = (ca * cv) * (1.0 + t)
        o_ref[:, la] = st_ref[j % 2, _HALO:_HALO + tm, :].astype(BF16)


def _ffn_up_kernel(h_ref, hp_ref, hn_ref, w_ref, cw_ref, cb_ref, o_ref,
                   hx_ref, hperm_ref, st_ref, u0_ref, u1_ref, *, tiles_per_seq, n_tiles):
    i = pl.program_id(0)
    u_refs = (u0_ref, u1_ref)
    n_pairs = D_FF // _MXU_COLS

    @pl.when(i == 0)
    def _():
        for thunk in _ffn_up_proj(h_ref, hp_ref, hn_ref, w_ref, hx_ref, hperm_ref, tiles_per_seq):
            thunk(u0_ref)

    for par in range(2):
        @pl.when((i > 0) & (i < n_tiles) & (i % 2 == par))
        def _(par=par):
            thunks = _ffn_up_proj(h_ref, hp_ref, hn_ref, w_ref, hx_ref, hperm_ref, tiles_per_seq)
            for jj in range(n_pairs):
                thunks[jj](u_refs[par])
                _ffn_up_conv(u_refs[1 - par], cw_ref, cb_ref, o_ref, st_ref, jj)

    @pl.when(i == n_tiles)
    def _():
        for jj in range(n_pairs):
            _ffn_up_conv(u_refs[(n_tiles - 1) % 2], cw_ref, cb_ref, o_ref, st_ref, jj)


def _ffn_up(h2, w_up, conv_w, conv_b, seq, tm):
    t = h2.shape[0]
    n_tiles = t // tm
    tiles_per_seq = seq // tm
    halo_per_tile = tm // _HALO
    n_halo = t // _HALO
    ext = tm + 2 * _HALO
    const = lambda i: (0, 0)
    cur = lambda i: jnp.minimum(i, n_tiles - 1)
    return pl.pallas_call(
        functools.partial(_ffn_up_kernel, tiles_per_seq=tiles_per_seq, n_tiles=n_tiles),
        grid=(n_tiles + 1,),
        in_specs=[
            pl.BlockSpec((tm, D_MODEL), lambda i: (cur(i), 0)),
            pl.BlockSpec((_HALO, D_MODEL),
                         lambda i: (jnp.maximum(cur(i) * halo_per_tile - 1, 0), 0)),
            pl.BlockSpec((_HALO, D_MODEL),
                         lambda i: (jnp.minimum((cur(i) + 1) * halo_per_tile, n_halo - 1), 0)),
            pl.BlockSpec((D_MODEL, 2 * D_FF), const),
            pl.BlockSpec((CONV_WIDTH, 2 * D_FF), const),
            pl.BlockSpec((1, 2 * D_FF), const),
        ],
        out_specs=pl.BlockSpec((tm, D_FF), lambda i: (jnp.maximum(i - 1, 0), 0)),
        out_shape=jax.ShapeDtypeStruct((t, D_FF), BF16),
        scratch_shapes=[pltpu.VMEM((D_MODEL // LANES, ext, LANES), F32),
                        pltpu.VMEM((ext, D_MODEL), BF16),
                        pltpu.VMEM((2, ext, LANES), F32),
                        pltpu.VMEM((ext, 2 * D_FF), F32),
                        pltpu.VMEM((ext, 2 * D_FF), F32)],
        compiler_params=_params(1),
        name="ffn_up",
    )(h2, h2, h2, w_up, conv_w, conv_b)


def _ffn_down_kernel(a_ref, w_ref, x1_ref, mod_ref, g_ref, o_ref):
    gpost = mod_ref[5:6, :] * g_ref[...]
    for r in range(a_ref.shape[0] // _ROW_CHUNK):
        rows = slice(r * _ROW_CHUNK, (r + 1) * _ROW_CHUNK)
        y = _dot(a_ref[rows, :], w_ref[...])
        o_ref[rows, :] = x1_ref[rows, :] + (y * _rms_scale(y)) * gpost


def _ffn_down(act, w_down, x1, mod3, g_post, seq, tm):
    t = x1.shape[0]
    per_b = seq // tm
    row = lambda i: (i, 0)
    const = lambda i: (0, 0)
    return pl.pallas_call(
        _ffn_down_kernel,
        grid=(t // tm,),
        in_specs=[
            pl.BlockSpec((tm, D_FF), row),
            pl.BlockSpec((D_FF, D_MODEL), const),
            pl.BlockSpec((tm, D_MODEL), row),
            pl.BlockSpec((None, 6, D_MODEL), lambda i: (i // per_b, 0, 0)),
            pl.BlockSpec((1, D_MODEL), const),
        ],
        out_specs=pl.BlockSpec((tm, D_MODEL), row),
        out_shape=jax.ShapeDtypeStruct((t, D_MODEL), F32),
        compiler_params=_params(1),
        name="ffn_down",
    )(act, w_down, x1, mod3, g_post)


@functools.lru_cache(maxsize=None)
def _rope_tables(seq):
    pos = np.arange(seq)
    row = (pos // GRID_W).astype(np.float64)
    col = (pos % GRID_W).astype(np.float64)
    inv = ROPE_THETA ** (-np.arange(ROPE_PAIRS_PER_AXIS, dtype=np.float64) / ROPE_PAIRS_PER_AXIS)
    ang = np.concatenate([row[:, None] * inv, col[:, None] * inv], axis=-1)
    cos = np.repeat(np.cos(ang), 2, axis=-1)
    sin = np.repeat(np.sin(ang), 2, axis=-1)
    sign = np.tile(np.array([-1.0, 1.0]), HEAD_DIM // 2)
    reps = LANES // HEAD_DIM
    return (np.tile(cos, (1, reps)).astype(np.float32),
            np.tile(sin * sign, (1, reps)).astype(np.float32))


@functools.lru_cache(maxsize=None)
def _dft_tables(seq):
    gd = FOURIER_GROUP_DIM
    kc = (np.outer(np.arange(gd), np.arange(gd)) % gd).astype(np.float64) * (2.0 * np.pi / gd)
    norm = 1.0 / math.sqrt(seq * gd)
    cs_c = np.concatenate([np.cos(kc), np.sin(kc)], axis=1) * norm
    n1 = seq // _DFT_SPLIT
    t_idx = np.arange(seq)
    ang_a = (np.outer(np.arange(n1), t_idx) % n1).astype(np.float64) * (2.0 * np.pi / n1)
    ang_b = (np.outer(np.arange(_DFT_SPLIT), t_idx) % seq).astype(np.float64) * (2.0 * np.pi / seq)
    f32 = lambda a: a.astype(np.float32)
    return f32(cs_c), f32(np.cos(ang_a)), f32(np.sin(ang_a)), f32(np.cos(ang_b)), f32(np.sin(ang_b))


def kernel(x, c, w_ada, b_ada, mix_pre_g, w_in, q_norm_g, k_norm_g, b_gate, w_attn_branch,
           w_fourier_branch, w_out, mix_post_g, ffn_pre_g, w_up, conv_w, conv_b, w_down,
           ffn_post_g):
    batch, seq, _ = x.shape
    depth = w_ada.shape[0]
    t = batch * seq
    tiles = _TILES

    cos_np, sin_np = _rope_tables(seq)
    cos_t = jnp.asarray(cos_np)
    sin_t = jnp.asarray(sin_np)
    dft_tabs = [jnp.asarray(a) for a in _dft_tables(seq)]
    seg = np.arange(ATTN_WIDTH) // HEAD_DIM
    ones_blk = jnp.asarray((seg[:, None] == seg[None, :]).astype(np.float32), dtype=BF16)

    x2 = x.reshape(t, D_MODEL)
    for l in range(depth):
        mod3 = _adaln(c, w_ada[l], b_ada[l]).reshape(batch, 6, D_MODEL)
        gq = (jnp.tile(q_norm_g[l], N_Q_HEADS) * (HEAD_DIM ** -0.5 * _LOG2_E)).reshape(1, ATTN_WIDTH)
        gk = jnp.tile(k_norm_g[l], N_KV_HEADS).reshape(1, KV_WIDTH)
        qt, kp, vt, f, ga, gf = _in_proj(
            x2, mod3, mix_pre_g[l].reshape(1, D_MODEL), w_in[l].astype(BF16),
            b_gate[l].reshape(1, 2 * D_MODEL), gq, gk, cos_t, sin_t, ones_blk, batch, seq,
            tiles["in_proj"])
        o = _attention(qt, kp, vt, batch, seq, tiles["attn"])
        yf = _fourier(f, *dft_tabs, batch, seq)
        x1, h2 = _merge(
            o, yf, ga, gf, x2, mod3, w_attn_branch[l].astype(BF16),
            w_fourier_branch[l].astype(BF16), w_out[l].astype(BF16),
            mix_post_g[l].reshape(1, D_MODEL), ffn_pre_g[l].reshape(1, D_MODEL), seq,
            tiles["merge"])
        glu_half = jnp.concatenate([jnp.ones((D_FF,), F32), jnp.full((D_FF,), 0.5, F32)])
        act = _ffn_up(h2, w_up[l].astype(BF16), conv_w[l] * glu_half,
                      (conv_b[l] * glu_half).reshape(1, 2 * D_FF), seq, tiles["ffn_up"])
        x2 = _ffn_down(act, w_down[l].astype(BF16), x1, mod3,
                       ffn_post_g[l].reshape(1, D_MODEL), seq, tiles["ffn_down"])
    return x2.reshape(batch, seq, D_MODEL)
```

```python
import functools
import math

import numpy as np
import jax
import jax.numpy as jnp
from jax import lax
from jax.experimental import pallas as pl
from jax.experimental.pallas import tpu as pltpu

D_MODEL = 1024
GRID_W = 64
HEAD_DIM = 64
N_Q_HEADS = 8
N_KV_HEADS = 2
Q_PER_KV = N_Q_HEADS // N_KV_HEADS
ATTN_WIDTH = N_Q_HEADS * HEAD_DIM
KV_WIDTH = N_KV_HEADS * HEAD_DIM
N_FOURIER_GROUPS = 4
FOURIER_GROUP_DIM = 128
FOURIER_WIDTH = N_FOURIER_GROUPS * FOURIER_GROUP_DIM
IN_WIDTH = ATTN_WIDTH + 2 * KV_WIDTH + FOURIER_WIDTH + 2 * D_MODEL
ROPE_THETA = 10000.0
ROPE_PAIRS_PER_AXIS = HEAD_DIM // 4
D_FF = 2816
CONV_WIDTH = 3
NORM_EPS = 1e-6

LANES = 128
BF16_SUBLANES = 16
VMEM_LIMIT_BYTES = 56 * 1024 * 1024
_ROW_CHUNK = 256
_TILES = {"in_proj": 512, "attn": 256, "merge": 1024, "ffn_up": 512, "ffn_down": 1024}

_LOG2_E = math.log2(math.e)

F32 = jnp.float32
BF16 = jnp.bfloat16

_O_K = ATTN_WIDTH
_O_V = _O_K + KV_WIDTH
_O_F = _O_V + KV_WIDTH
_O_GA = _O_F + FOURIER_WIDTH
_O_GF = _O_GA + D_MODEL


def _params(n_grid_dims, flags=None):
    return pltpu.CompilerParams(
        dimension_semantics=("arbitrary",) * n_grid_dims,
        vmem_limit_bytes=VMEM_LIMIT_BYTES,
        flags=flags,
    )


def _dot(a, b):
    return jnp.dot(a, b, preferred_element_type=F32)


def _rms_scale(v):
    return lax.rsqrt(jnp.mean(v * v, axis=-1, keepdims=True) + NORM_EPS)


def _adaln_kernel(c_ref, w_ref, b_ref, o_ref):
    c = c_ref[...]
    a = c / (1.0 + jnp.exp(-c))
    a_hi = a.astype(BF16)
    a_lo = (a - a_hi.astype(F32)).astype(BF16)
    w = w_ref[...]
    w_hi = w.astype(BF16)
    w_lo = (w - w_hi.astype(F32)).astype(BF16)
    o_ref[...] = _dot(a_hi, w_hi) + (_dot(a_lo, w_hi) + _dot(a_hi, w_lo)) + b_ref[...]


def _adaln(c, w_ada, b_ada):
    batch = c.shape[0]
    n = w_ada.shape[1]
    tn = 1536
    return pl.pallas_call(
        _adaln_kernel,
        grid=(n // tn,),
        in_specs=[
            pl.BlockSpec((batch, D_MODEL), lambda j: (0, 0)),
            pl.BlockSpec((D_MODEL, tn), lambda j: (0, j)),
            pl.BlockSpec((1, tn), lambda j: (0, j)),
        ],
        out_specs=pl.BlockSpec((batch, tn), lambda j: (0, j)),
        out_shape=jax.ShapeDtypeStruct((batch, n), F32),
        compiler_params=_params(1),
        name="adaln",
    )(c, w_ada, b_ada.reshape(1, n))


def _swap_pairs(v):
    n = v.shape[-1]
    lane = lax.broadcasted_iota(jnp.int32, v.shape, v.ndim - 1)
    from_right = pltpu.roll(v, n - 1, v.ndim - 1)
    from_left = pltpu.roll(v, 1, v.ndim - 1)
    return jnp.where((lane & 1) == 0, from_right, from_left)


def _head_norm_rope(z, gain, ones_blk, cos, sin):
    reps = z.shape[-1] // LANES
    ssq = _dot((z * z).astype(BF16), ones_blk)
    r = lax.rsqrt(ssq * (1.0 / HEAD_DIM) + NORM_EPS)
    zg = z * gain
    cos_t = jnp.concatenate([cos] * reps, axis=-1) if reps > 1 else cos
    sin_t = jnp.concatenate([sin] * reps, axis=-1) if reps > 1 else sin
    return (zg * cos_t + _swap_pairs(zg) * sin_t) * r


def _cast_once(src_ref, dst_ref):
    @pl.when(pl.program_id(0) == 0)
    def _():
        dst_ref[...] = src_ref[...].astype(BF16)


def _in_proj_kernel(x_ref, mod_ref, g_ref, w32_ref, bg_ref, gq_ref, gk_ref,
                    cos_ref, sin_ref, ones_ref,
                    qt_ref, kp_ref, vt_ref, f_ref, ga_ref, gf_ref, w_ref):
    _cast_once(w32_ref, w_ref)
    shift = mod_ref[0:1, :]
    scale = mod_ref[1:2, :]
    gmod = g_ref[...] * (1.0 + scale)
    for r in range(x_ref.shape[0] // _ROW_CHUNK):
        rows = slice(r * _ROW_CHUNK, (r + 1) * _ROW_CHUNK)
        x = x_ref[rows, :]
        hb = ((x * _rms_scale(x)) * gmod + shift).astype(BF16)
        cos = cos_ref[rows, :]
        sin = sin_ref[rows, :]

        zq = _dot(hb, w_ref[:, 0:_O_K])
        q = _head_norm_rope(zq, gq_ref[...], ones_ref[...], cos, sin)
        qt_ref[:, rows] = q.T.astype(BF16)

        zkv = _dot(hb, w_ref[:, _O_K:_O_F])
        k = _head_norm_rope(zkv[:, 0:KV_WIDTH], gk_ref[...],
                            ones_ref[0:KV_WIDTH, 0:KV_WIDTH], cos, sin)
        k_sw = pltpu.roll(k, HEAD_DIM, 1)
        first = lax.broadcasted_iota(jnp.int32, k.shape, 1) < HEAD_DIM
        kp_ref[rows, :] = jnp.concatenate(
            [jnp.where(first, k, 0.0), jnp.where(first, 0.0, k_sw),
             jnp.where(first, k_sw, 0.0), jnp.where(first, 0.0, k)], axis=-1).astype(BF16)
        zvt = zkv[:, KV_WIDTH:2 * KV_WIDTH].T
        ones_rows = jnp.ones((HEAD_DIM, zvt.shape[1]), F32)
        vt_ref[:, rows] = jnp.concatenate(
            [piece for kv in range(N_KV_HEADS)
             for piece in (zvt[kv * HEAD_DIM:(kv + 1) * HEAD_DIM, :], ones_rows)],
            axis=0).astype(BF16)

        f_ref[rows, :] = _dot(hb, w_ref[:, _O_F:_O_GA]).astype(BF16)

        ta = _dot(hb, w_ref[:, _O_GA:_O_GF]) + bg_ref[:, 0:D_MODEL]
        ga_ref[rows, :] = (1.0 / (1.0 + jnp.exp(-ta))).astype(BF16)
        tf = _dot(hb, w_ref[:, _O_GF:IN_WIDTH]) + bg_ref[:, D_MODEL:2 * D_MODEL]
        gf_ref[rows, :] = (1.0 / (1.0 + jnp.exp(-tf))).astype(BF16)


def _in_proj(x2, mod3, g_pre, w_in, b_gate, gq, gk, cos_t, sin_t, ones_blk, batch, seq, tm):
    t = batch * seq
    per_b = seq // tm
    row = lambda i: (i, 0)
    const = lambda i: (0, 0)
    return pl.pallas_call(
        _in_proj_kernel,
        grid=(t // tm,),
        in_specs=[
            pl.BlockSpec((tm, D_MODEL), row),
            pl.BlockSpec((None, 6, D_MODEL), lambda i: (i // per_b, 0, 0)),
            pl.BlockSpec((1, D_MODEL), const),
            pl.BlockSpec((D_MODEL, IN_WIDTH), const),
            pl.BlockSpec((1, 2 * D_MODEL), const),
            pl.BlockSpec((1, ATTN_WIDTH), const),
            pl.BlockSpec((1, KV_WIDTH), const),
            pl.BlockSpec((tm, LANES), lambda i: (i % per_b, 0)),
            pl.BlockSpec((tm, LANES), lambda i: (i % per_b, 0)),
            pl.BlockSpec((ATTN_WIDTH, ATTN_WIDTH), const),
        ],
        out_specs=[
            pl.BlockSpec((None, ATTN_WIDTH, tm), lambda i: (i // per_b, 0, i % per_b)),
            pl.BlockSpec((tm, 2 * N_KV_HEADS * LANES), row),
            pl.BlockSpec((None, N_KV_HEADS * LANES, tm), lambda i: (i // per_b, 0, i % per_b)),
            pl.BlockSpec((tm, FOURIER_WIDTH), row),
            pl.BlockSpec((tm, D_MODEL), row),
            pl.BlockSpec((tm, D_MODEL), row),
        ],
        out_shape=[
            jax.ShapeDtypeStruct((batch, ATTN_WIDTH, seq), BF16),
            jax.ShapeDtypeStruct((t, 2 * N_KV_HEADS * LANES), BF16),
            jax.ShapeDtypeStruct((batch, N_KV_HEADS * LANES, seq), BF16),
            jax.ShapeDtypeStruct((t, FOURIER_WIDTH), BF16),
            jax.ShapeDtypeStruct((t, D_MODEL), BF16),
            jax.ShapeDtypeStruct((t, D_MODEL), BF16),
        ],
        scratch_shapes=[pltpu.VMEM((D_MODEL, IN_WIDTH), BF16)],
        compiler_params=_params(1),
        name="in_proj",
    )(x2, mod3, g_pre, w_in, b_gate, gq, gk, cos_t, sin_t, ones_blk)


_ATTN_SLOTS = 2


def _attn_step(qt_ref, kp_ref, vt_ref, o_ref, p_ref, s_new_ref, s_old_ref):
    outs = []
    for h in range(N_Q_HEADS):
        kv = h // Q_PER_KV
        slot = h % _ATTN_SLOTS
        pair = h // 2
        variant = 2 * kv + h % 2
        s_new_ref[h] = _dot(kp_ref[:, variant * LANES:(variant + 1) * LANES],
                            qt_ref[pair * LANES:(pair + 1) * LANES, :])
        s = s_old_ref[h]
        m = jnp.max(s, axis=0, keepdims=True)
        p_ref[slot] = jnp.exp2(s - m).astype(BF16)
        ov = _dot(vt_ref[kv * LANES:(kv + 1) * LANES, :], p_ref[slot])
        on = ov[0:HEAD_DIM, :] / ov[HEAD_DIM:2 * HEAD_DIM, :]
        outs.append(on.T)
    o_ref[...] = jnp.concatenate(outs, axis=-1).astype(BF16)


def _attn_kernel(qt_ref, kp_ref, vt_ref, o_ref, p_ref, sa_ref, sb_ref):
    g = pl.program_id(0)

    @pl.when(g == 0)
    def _():
        sb_ref[...] = jnp.zeros(sb_ref.shape, F32)

    @pl.when(g % 2 == 0)
    def _():
        _attn_step(qt_ref, kp_ref, vt_ref, o_ref, p_ref, sa_ref, sb_ref)

    @pl.when(g % 2 == 1)
    def _():
        _attn_step(qt_ref, kp_ref, vt_ref, o_ref, p_ref, sb_ref, sa_ref)


def _attention(qt, kp, vt, batch, seq, tq):
    t = batch * seq
    per_b = seq // tq
    n_blk = t // tq
    cur = lambda g: jnp.minimum(g, n_blk - 1)
    prev = lambda g: jnp.maximum(g - 1, 0)
    return pl.pallas_call(
        _attn_kernel,
        grid=(n_blk + 1,),
        in_specs=[
            pl.BlockSpec((None, ATTN_WIDTH, tq), lambda g: (cur(g) // per_b, 0, cur(g) % per_b)),
            pl.BlockSpec((seq, 2 * N_KV_HEADS * LANES), lambda g: (cur(g) // per_b, 0)),
            pl.BlockSpec((None, N_KV_HEADS * LANES, seq), lambda g: (prev(g) // per_b, 0, 0)),
        ],
        out_specs=pl.BlockSpec((tq, ATTN_WIDTH), lambda g: (prev(g), 0)),
        out_shape=jax.ShapeDtypeStruct((t, ATTN_WIDTH), BF16),
        scratch_shapes=[pltpu.VMEM((_ATTN_SLOTS, seq, tq), BF16),
                        pltpu.VMEM((N_Q_HEADS, seq, tq), F32),
                        pltpu.VMEM((N_Q_HEADS, seq, tq), F32)],
        compiler_params=_params(1),
        name="attn",
    )(qt, kp, vt)


_DFT_SPLIT = 32
_REV_BLOCK = 256


def _fourier_kernel(f_ref, cs_ref, jw_ref, ca_ref, sa_ref, cb_ref, sb_ref, o_ref,
                    w_ref, x_ref, r_ref):
    seq = f_ref.shape[0]
    half = seq // 2
    gd = FOURIER_GROUP_DIM
    fw = FOURIER_WIDTH

    @pl.when(pl.program_id(0) == 0)
    def _():
        cb = cb_ref[...]
        sb = sb_ref[...]

        def body(s1, carry):
            ca = ca_ref[pl.ds(s1, 1), :]
            sa = sa_ref[pl.ds(s1, 1), :]
            rows = pl.ds(pl.multiple_of(s1 * _DFT_SPLIT, _DFT_SPLIT), _DFT_SPLIT)
            w_ref[rows, 0:half] = (ca * cb - sa * sb).astype(BF16)
            w_ref[rows, half:seq] = (-(sa * cb + ca * sb)).astype(BF16)
            return carry

        lax.fori_loop(0, seq // _DFT_SPLIT, body, 0)

    cs = cs_ref[...].astype(BF16)
    for g in range(N_FOURIER_GROUPS):
        xcs = _dot(f_ref[:, g * gd:(g + 1) * gd], cs)
        x_ref[:, g * gd:(g + 1) * gd] = xcs[:, 0:gd].astype(BF16)
        x_ref[:, fw + g * gd:fw + (g + 1) * gd] = xcs[:, gd:2 * gd].astype(BF16)

    n_blk = seq // _REV_BLOCK
    jw = jw_ref[...].astype(BF16)
    for a in range(half // _REV_BLOCK):
        beta = n_blk - 1 - a
        nxt = (beta + 1) % n_blk
        win = jnp.concatenate([x_ref[beta * _REV_BLOCK:(beta + 1) * _REV_BLOCK, :],
                               x_ref[nxt * _REV_BLOCK:(nxt + 1) * _REV_BLOCK, :]], axis=0)
        rev = _dot(jw, win)
        top = x_ref[a * _REV_BLOCK:(a + 1) * _REV_BLOCK, :].astype(F32)
        rows = slice(a * _REV_BLOCK, (a + 1) * _REV_BLOCK)
        r_ref[rows, :] = (top[:, 0:fw] + rev[:, 0:fw]).astype(BF16)
        r_ref[half + a * _REV_BLOCK:half + (a + 1) * _REV_BLOCK, :] = (
            top[:, fw:2 * fw] - rev[:, fw:2 * fw]).astype(BF16)

    tm = 512
    x_mid = x_ref[half:half + 1, 0:fw].astype(F32)
    odd = (lax.broadcasted_iota(jnp.int32, (tm, fw), 0) & 1) == 1
    parity_term = jnp.where(odd, -x_mid, x_mid)
    for mi in range(seq // tm):
        acc = _dot(w_ref[mi * tm:(mi + 1) * tm, :], r_ref[...])
        o_ref[mi * tm:(mi + 1) * tm, :] = (acc + parity_term).astype(BF16)


def _fourier(f, cs_c, jw, ca, sa, cb, sb, batch, seq):
    t = batch * seq
    const = lambda b: (0, 0)
    return pl.pallas_call(
        _fourier_kernel,
        grid=(batch,),
        in_specs=[
            pl.BlockSpec((seq, FOURIER_WIDTH), lambda b: (b, 0)),
            pl.BlockSpec(cs_c.shape, const),
            pl.BlockSpec(jw.shape, const),
            pl.BlockSpec(ca.shape, const),
            pl.BlockSpec(sa.shape, const),
            pl.BlockSpec(cb.shape, const),
            pl.BlockSpec(sb.shape, const),
        ],
        out_specs=pl.BlockSpec((seq, FOURIER_WIDTH), lambda b: (b, 0)),
        out_shape=jax.ShapeDtypeStruct((t, FOURIER_WIDTH), BF16),
        scratch_shapes=[pltpu.VMEM((seq, seq), BF16),
                        pltpu.VMEM((seq, 2 * FOURIER_WIDTH), BF16),
                        pltpu.VMEM((seq, FOURIER_WIDTH), BF16)],
        compiler_params=_params(1),
        name="fourier",
    )(f, cs_c, jw, ca, sa, cb, sb)


def _merge_kernel(o_ref, yf_ref, ga_ref, gf_ref, x_ref, mod_ref, wa32_ref, wf32_ref, wo32_ref,
                  gpost_ref, gpre2_ref, x1_ref, h2_ref, wa_ref, wf_ref, wo_ref):
    _cast_once(wa32_ref, wa_ref)
    _cast_once(wf32_ref, wf_ref)
    _cast_once(wo32_ref, wo_ref)
    gate1 = mod_ref[2:3, :]
    shift2 = mod_ref[3:4, :]
    scale2 = mod_ref[4:5, :]
    gpost = gate1 * gpost_ref[...]
    gmod2 = gpre2_ref[...] * (1.0 + scale2)
    for r in range(x_ref.shape[0] // _ROW_CHUNK):
        rows = slice(r * _ROW_CHUNK, (r + 1) * _ROW_CHUNK)
        ya = _dot(o_ref[rows, :], wa_ref[...])
        yf = _dot(yf_ref[rows, :], wf_ref[...])
        y = (ga_ref[rows, :].astype(F32) * ya + gf_ref[rows, :].astype(F32) * yf).astype(BF16)
        y2 = _dot(y, wo_ref[...])
        x1 = x_ref[rows, :] + (y2 * _rms_scale(y2)) * gpost
        x1_ref[rows, :] = x1
        h2_ref[rows, :] = ((x1 * _rms_scale(x1)) * gmod2 + shift2).astype(BF16)


def _merge(o, yf, ga, gf, x2, mod3, wa, wf, wo, g_post, g_pre2, seq, tm):
    t = x2.shape[0]
    per_b = seq // tm
    row = lambda i: (i, 0)
    const = lambda i: (0, 0)
    return pl.pallas_call(
        _merge_kernel,
        grid=(t // tm,),
        in_specs=[
            pl.BlockSpec((tm, ATTN_WIDTH), row),
            pl.BlockSpec((tm, FOURIER_WIDTH), row),
            pl.BlockSpec((tm, D_MODEL), row),
            pl.BlockSpec((tm, D_MODEL), row),
            pl.BlockSpec((tm, D_MODEL), row),
            pl.BlockSpec((None, 6, D_MODEL), lambda i: (i // per_b, 0, 0)),
            pl.BlockSpec((ATTN_WIDTH, D_MODEL), const),
            pl.BlockSpec((FOURIER_WIDTH, D_MODEL), const),
            pl.BlockSpec((D_MODEL, D_MODEL), const),
            pl.BlockSpec((1, D_MODEL), const),
            pl.BlockSpec((1, D_MODEL), const),
        ],
        out_specs=[pl.BlockSpec((tm, D_MODEL), row), pl.BlockSpec((tm, D_MODEL), row)],
        out_shape=[jax.ShapeDtypeStruct((t, D_MODEL), F32),
                   jax.ShapeDtypeStruct((t, D_MODEL), BF16)],
        scratch_shapes=[pltpu.VMEM((ATTN_WIDTH, D_MODEL), BF16),
                        pltpu.VMEM((FOURIER_WIDTH, D_MODEL), BF16),
                        pltpu.VMEM((D_MODEL, D_MODEL), BF16)],
        compiler_params=_params(1),
        name="merge",
    )(o, yf, ga, gf, x2, mod3, wa, wf, wo, g_post, g_pre2)


_HALO = BF16_SUBLANES
_MXU_COLS = 256
_GELU_K1 = math.sqrt(2.0 / math.pi)
_GELU_K2 = _GELU_K1 * 0.044715


def _ffn_up_proj(h_ref, hp_ref, hn_ref, w_ref, hx_ref, hperm_ref, tiles_per_seq):
    tm = h_ref.shape[0]
    ext = tm + 2 * _HALO
    pitch = ext // 8
    i = pl.program_id(0)
    keep_prev = jnp.where(i % tiles_per_seq != 0, 1.0, 0.0)
    keep_next = jnp.where(i % tiles_per_seq != tiles_per_seq - 1, 1.0, 0.0)

    for c in range(D_MODEL // LANES):
        lc = slice(c * LANES, (c + 1) * LANES)
        hx_ref[c, 0:_HALO, :] = hp_ref[:, lc].astype(F32) * keep_prev
        hx_ref[c, _HALO:_HALO + tm, :] = h_ref[:, lc].astype(F32)
        hx_ref[c, _HALO + tm:ext, :] = hn_ref[:, lc].astype(F32) * keep_next
    for c in range(D_MODEL // LANES):
        lc = slice(c * LANES, (c + 1) * LANES)
        for b in range(0, pitch, 2):
            pair = jnp.concatenate([hx_ref[c, pl.ds(b, 8, stride=pitch), :],
                                    hx_ref[c, pl.ds(b + 1, 8, stride=pitch), :]], axis=0)
            hperm_ref[8 * b:8 * b + 16, lc] = pair.astype(BF16)
    hperm = hperm_ref[...]

    def pair_thunk(jj):
        def run(u_ref):
            for half in range(2):
                cols = slice(half * D_FF + jj * _MXU_COLS, half * D_FF + (jj + 1) * _MXU_COLS)
                u_ref[:, cols] = _dot(hperm, w_ref[:, cols])
        return run

    return [pair_thunk(jj) for jj in range(D_FF // _MXU_COLS)]


def _ffn_up_conv(u_ref, cw_ref, cb_ref, o_ref, st_ref, jj):
    tm = o_ref.shape[0]
    ext = tm + 2 * _HALO
    pitch = ext // 8
    for k in range(_MXU_COLS // LANES):
        j = 2 * jj + k
        la = slice(j * LANES, (j + 1) * LANES)
        lb = slice(D_FF + j * LANES, D_FF + (j + 1) * LANES)
        bc = lambda ref, r, l: jnp.broadcast_to(ref[r:r + 1, l], (8, LANES))
        wa = [bc(cw_ref, r, la) for r in range(CONV_WIDTH)]
        wb = [bc(cw_ref, r, lb) for r in range(CONV_WIDTH)]
        ba = bc(cb_ref, 0, la)
        bb = bc(cb_ref, 0, lb)

        def vreg(lanes, b):
            if b < 0:
                return pltpu.roll(u_ref[8 * (pitch - 1):8 * pitch, lanes], 1, 0)
            if b >= pitch:
                return pltpu.roll(u_ref[0:8, lanes], 7, 0)
            return u_ref[8 * b:8 * b + 8, lanes]

        for b in range(pitch):
            ca = vreg(la, b - 1) * wa[0] + vreg(la, b) * wa[1] + vreg(la, b + 1) * wa[2] + ba
            cv = vreg(lb, b - 1) * wb[0] + vreg(lb, b) * wb[1] + vreg(lb, b + 1) * wb[2] + bb
            t = jnp.tanh(ca * (_GELU_K1 + _GELU_K2 * (ca * ca)))
            st_ref[j % 2, pl.ds(b, 8, stride=pitch), :] = (ca * cv) * (1.0 + t)
        o_ref[:, la] = st_ref[j % 2, _HALO:_HALO + tm, :].astype(BF16)


def _ffn_up_kernel(h_ref, hp_ref, hn_ref, w_ref, cw_ref, cb_ref, o_ref,
                   hx_ref, hperm_ref, st_ref, u0_ref, u1_ref, *, tiles_per_seq, n_tiles):
    i = pl.program_id(0)
    u_refs = (u0_ref, u1_ref)
    n_pairs = D_FF // _MXU_COLS

    @pl.when(i == 0)
    def _():
        for thunk in _ffn_up_proj(h_ref, hp_ref, hn_ref, w_ref, hx_ref, hperm_ref, tiles_per_seq):
            thunk(u0_ref)

    for par in range(2):
        @pl.when((i > 0) & (i < n_tiles) & (i % 2 == par))
        def _(par=par):
            thunks = _ffn_up_proj(h_ref, hp_ref, hn_ref, w_ref, hx_ref, hperm_ref, tiles_per_seq)
            for jj in range(n_pairs):
                thunks[jj](u_refs[par])
                _ffn_up_conv(u_refs[1 - par], cw_ref, cb_ref, o_ref, st_ref, jj)

    @pl.when(i == n_tiles)
    def _():
        for jj in range(n_pairs):
            _ffn_up_conv(u_refs[(n_tiles - 1) % 2], cw_ref, cb_ref, o_ref, st_ref, jj)


def _ffn_up(h2, w_up, conv_w, conv_b, seq, tm):
    t = h2.shape[0]
    n_tiles = t // tm
    tiles_per_seq = seq // tm
    halo_per_tile = tm // _HALO
    n_halo = t // _HALO
    ext = tm + 2 * _HALO
    const = lambda i: (0, 0)
    cur = lambda i: jnp.minimum(i, n_tiles - 1)
    return pl.pallas_call(
        functools.partial(_ffn_up_kernel, tiles_per_seq=tiles_per_seq, n_tiles=n_tiles),
        grid=(n_tiles + 1,),
        in_specs=[
            pl.BlockSpec((tm, D_MODEL), lambda i: (cur(i), 0)),
            pl.BlockSpec((_HALO, D_MODEL),
                         lambda i: (jnp.maximum(cur(i) * halo_per_tile - 1, 0), 0)),
            pl.BlockSpec((_HALO, D_MODEL),
                         lambda i: (jnp.minimum((cur(i) + 1) * halo_per_tile, n_halo - 1), 0)),
            pl.BlockSpec((D_MODEL, 2 * D_FF), const),
            pl.BlockSpec((CONV_WIDTH, 2 * D_FF), const),
            pl.BlockSpec((1, 2 * D_FF), const),
        ],
        out_specs=pl.BlockSpec((tm, D_FF), lambda i: (jnp.maximum(i - 1, 0), 0)),
        out_shape=jax.ShapeDtypeStruct((t, D_FF), BF16),
        scratch_shapes=[pltpu.VMEM((D_MODEL // LANES, ext, LANES), F32),
                        pltpu.VMEM((ext, D_MODEL), BF16),
                        pltpu.VMEM((2, ext, LANES), F32),
                        pltpu.VMEM((ext, 2 * D_FF), F32),
                        pltpu.VMEM((ext, 2 * D_FF), F32)],
        compiler_params=_params(1),
        name="ffn_up",
    )(h2, h2, h2, w_up, conv_w, conv_b)


def _ffn_down_kernel(a_ref, w32_ref, x1_ref, mod_ref, g_ref, o_ref, w_ref):
    _cast_once(w32_ref, w_ref)
    gpost = mod_ref[5:6, :] * g_ref[...]
    for r in range(a_ref.shape[0] // _ROW_CHUNK):
        rows = slice(r * _ROW_CHUNK, (r + 1) * _ROW_CHUNK)
        y = _dot(a_ref[rows, :], w_ref[...])
        o_ref[rows, :] = x1_ref[rows, :] + (y * _rms_scale(y)) * gpost


def _ffn_down(act, w_down, x1, mod3, g_post, seq, tm):
    t = x1.shape[0]
    per_b = seq // tm
    row = lambda i: (i, 0)
    const = lambda i: (0, 0)
    return pl.pallas_call(
        _ffn_down_kernel,
        grid=(t // tm,),
        in_specs=[
            pl.BlockSpec((tm, D_FF), row),
            pl.BlockSpec((D_FF, D_MODEL), const),
            pl.BlockSpec((tm, D_MODEL), row),
            pl.BlockSpec((None, 6, D_MODEL), lambda i: (i // per_b, 0, 0)),
            pl.BlockSpec((1, D_MODEL), const),
        ],
        out_specs=pl.BlockSpec((tm, D_MODEL), row),
        out_shape=jax.ShapeDtypeStruct((t, D_MODEL), F32),
        scratch_shapes=[pltpu.VMEM((D_FF, D_MODEL), BF16)],
        compiler_params=_params(1),
        name="ffn_down",
    )(act, w_down, x1, mod3, g_post)


@functools.lru_cache(maxsize=None)
def _rope_tables(seq):
    pos = np.arange(seq)
    row = (pos // GRID_W).astype(np.float64)
    col = (pos % GRID_W).astype(np.float64)
    inv = ROPE_THETA ** (-np.arange(ROPE_PAIRS_PER_AXIS, dtype=np.float64) / ROPE_PAIRS_PER_AXIS)
    ang = np.concatenate([row[:, None] * inv, col[:, None] * inv], axis=-1)
    cos = np.repeat(np.cos(ang), 2, axis=-1)
    sin = np.repeat(np.sin(ang), 2, axis=-1)
    sign = np.tile(np.array([-1.0, 1.0]), HEAD_DIM // 2)
    reps = LANES // HEAD_DIM
    return (np.tile(cos, (1, reps)).astype(np.float32),
            np.tile(sin * sign, (1, reps)).astype(np.float32))


@functools.lru_cache(maxsize=None)
def _dft_tables(seq):
    gd = FOURIER_GROUP_DIM
    kc = (np.outer(np.arange(gd), np.arange(gd)) % gd).astype(np.float64) * (2.0 * np.pi / gd)
    norm = 1.0 / math.sqrt(seq * gd)
    cs_c = np.concatenate([np.cos(kc), np.sin(kc)], axis=1) * norm
    n1 = seq // _DFT_SPLIT
    t_idx = np.arange(seq // 2)
    ang_a = (np.outer(np.arange(n1), t_idx) % n1).astype(np.float64) * (2.0 * np.pi / n1)
    ang_b = (np.outer(np.arange(_DFT_SPLIT), t_idx) % seq).astype(np.float64) * (2.0 * np.pi / seq)
    cos_b = np.cos(ang_b)
    cos_b[:, 0] *= 0.5
    jw = np.zeros((_REV_BLOCK, 2 * _REV_BLOCK))
    jw[np.arange(_REV_BLOCK), _REV_BLOCK - np.arange(_REV_BLOCK)] = 1.0
    f32 = lambda a: a.astype(np.float32)
    return (f32(cs_c), f32(jw), f32(np.cos(ang_a)), f32(np.sin(ang_a)), f32(cos_b),
            f32(np.sin(ang_b)))


def kernel(x, c, w_ada, b_ada, mix_pre_g, w_in, q_norm_g, k_norm_g, b_gate, w_attn_branch,
           w_fourier_branch, w_out, mix_post_g, ffn_pre_g, w_up, conv_w, conv_b, w_down,
           ffn_post_g):
    batch, seq, _ = x.shape
    depth = w_ada.shape[0]
    t = batch * seq
    tiles = _TILES

    cos_np, sin_np = _rope_tables(seq)
    cos_t = jnp.asarray(cos_np)
    sin_t = jnp.asarray(sin_np)
    dft_tabs = [jnp.asarray(a) for a in _dft_tables(seq)]
    seg = np.arange(ATTN_WIDTH) // HEAD_DIM
    ones_blk = jnp.asarray((seg[:, None] == seg[None, :]).astype(np.float32), dtype=BF16)

    x2 = x.reshape(t, D_MODEL)
    for l in range(depth):
        mod3 = _adaln(c, w_ada[l], b_ada[l]).reshape(batch, 6, D_MODEL)
        gq = (jnp.tile(q_norm_g[l], N_Q_HEADS) * (HEAD_DIM ** -0.5 * _LOG2_E)).reshape(1, ATTN_WIDTH)
        gk = jnp.tile(k_norm_g[l], N_KV_HEADS).reshape(1, KV_WIDTH)
        qt, kp, vt, f, ga, gf = _in_proj(
            x2, mod3, mix_pre_g[l].reshape(1, D_MODEL), w_in[l],
            b_gate[l].reshape(1, 2 * D_MODEL), gq, gk, cos_t, sin_t, ones_blk, batch, seq,
            tiles["in_proj"])
        o = _attention(qt, kp, vt, batch, seq, tiles["attn"])
        yf = _fourier(f, *dft_tabs, batch, seq)
        x1, h2 = _merge(
            o, yf, ga, gf, x2, mod3, w_attn_branch[l], w_fourier_branch[l], w_out[l],
            mix_post_g[l].reshape(1, D_MODEL), ffn_pre_g[l].reshape(1, D_MODEL), seq,
            tiles["merge"])
        glu_half = jnp.concatenate([jnp.ones((D_FF,), F32), jnp.full((D_FF,), 0.5, F32)])
        act = _ffn_up(h2, w_up[l].astype(BF16), conv_w[l] * glu_half,
                      (conv_b[l] * glu_half).reshape(1, 2 * D_FF), seq, tiles["ffn_up"])
        x2 = _ffn_down(act, w_down[l], x1, mod3,
                       ffn_post_g[l].reshape(1, D_MODEL), seq, tiles["ffn_down"])
    return x2.reshape(batch, seq, D_MODEL)
```

```python
import functools
import math

import numpy as np
import jax
import jax.numpy as jnp
from jax import lax
from jax.experimental import pallas as pl
from jax.experimental.pallas import tpu as pltpu

D_MODEL = 1024
GRID_W = 64
HEAD_DIM = 64
N_Q_HEADS = 8
N_KV_HEADS = 2
Q_PER_KV = N_Q_HEADS // N_KV_HEADS
ATTN_WIDTH = N_Q_HEADS * HEAD_DIM
KV_WIDTH = N_KV_HEADS * HEAD_DIM
N_FOURIER_GROUPS = 4
FOURIER_GROUP_DIM = 128
FOURIER_WIDTH = N_FOURIER_GROUPS * FOURIER_GROUP_DIM
IN_WIDTH = ATTN_WIDTH + 2 * KV_WIDTH + FOURIER_WIDTH + 2 * D_MODEL
ROPE_THETA = 10000.0
ROPE_PAIRS_PER_AXIS = HEAD_DIM // 4
D_FF = 2816
CONV_WIDTH = 3
NORM_EPS = 1e-6

LANES = 128
BF16_SUBLANES = 16
VMEM_LIMIT_BYTES = 56 * 1024 * 1024
_ROW_CHUNK = 256
_TILES = {"in_proj": 512, "attn": 256, "merge": 512, "ffn_up": 512, "ffn_down": 1024}

_LOG2_E = math.log2(math.e)

F32 = jnp.float32
BF16 = jnp.bfloat16

_O_K = ATTN_WIDTH
_O_V = _O_K + KV_WIDTH
_O_F = _O_V + KV_WIDTH
_O_GA = _O_F + FOURIER_WIDTH
_O_GF = _O_GA + D_MODEL


def _params(n_grid_dims, flags=None):
    return pltpu.CompilerParams(
        dimension_semantics=("arbitrary",) * n_grid_dims,
        vmem_limit_bytes=VMEM_LIMIT_BYTES,
        flags=flags,
    )


def _dot(a, b):
    return jnp.dot(a, b, preferred_element_type=F32)


def _rms_scale(v):
    return lax.rsqrt(jnp.mean(v * v, axis=-1, keepdims=True) + NORM_EPS)


def _adaln_kernel(c_ref, w_ref, b_ref, o_ref):
    c = c_ref[...]
    a = c / (1.0 + jnp.exp(-c))
    a_hi = a.astype(BF16)
    a_lo = (a - a_hi.astype(F32)).astype(BF16)
    w = w_ref[...]
    w_hi = w.astype(BF16)
    w_lo = (w - w_hi.astype(F32)).astype(BF16)
    o_ref[...] = _dot(a_hi, w_hi) + (_dot(a_lo, w_hi) + _dot(a_hi, w_lo)) + b_ref[...]


def _adaln(c, w_ada, b_ada):
    batch = c.shape[0]
    n = w_ada.shape[1]
    tn = 1536
    return pl.pallas_call(
        _adaln_kernel,
        grid=(n // tn,),
        in_specs=[
            pl.BlockSpec((batch, D_MODEL), lambda j: (0, 0)),
            pl.BlockSpec((D_MODEL, tn), lambda j: (0, j)),
            pl.BlockSpec((1, tn), lambda j: (0, j)),
        ],
        out_specs=pl.BlockSpec((batch, tn), lambda j: (0, j)),
        out_shape=jax.ShapeDtypeStruct((batch, n), F32),
        compiler_params=_params(1),
        name="adaln",
    )(c, w_ada, b_ada.reshape(1, n))


def _swap_pairs(v):
    n = v.shape[-1]
    lane = lax.broadcasted_iota(jnp.int32, v.shape, v.ndim - 1)
    from_right = pltpu.roll(v, n - 1, v.ndim - 1)
    from_left = pltpu.roll(v, 1, v.ndim - 1)
    return jnp.where((lane & 1) == 0, from_right, from_left)


def _head_norm_rope(z, gain, ones_blk, cos, sin):
    reps = z.shape[-1] // LANES
    ssq = _dot((z * z).astype(BF16), ones_blk)
    r = lax.rsqrt(ssq * (1.0 / HEAD_DIM) + NORM_EPS)
    zg = z * gain
    cos_t = jnp.concatenate([cos] * reps, axis=-1) if reps > 1 else cos
    sin_t = jnp.concatenate([sin] * reps, axis=-1) if reps > 1 else sin
    return (zg * cos_t + _swap_pairs(zg) * sin_t) * r


def _cast_once(src_ref, dst_ref):
    @pl.when(pl.program_id(0) == 0)
    def _():
        dst_ref[...] = src_ref[...].astype(BF16)


def _in_proj_kernel(x_ref, mod_ref, g_ref, w32_ref, gq_ref, gk_ref,
                    cos_ref, sin_ref, ones_ref,
                    qt_ref, kp_ref, vt_ref, f_ref, w_ref):
    _cast_once(w32_ref, w_ref)
    shift = mod_ref[0:1, :]
    scale = mod_ref[1:2, :]
    gmod = g_ref[...] * (1.0 + scale)
    for r in range(x_ref.shape[0] // _ROW_CHUNK):
        rows = slice(r * _ROW_CHUNK, (r + 1) * _ROW_CHUNK)
        x = x_ref[rows, :]
        hb = ((x * _rms_scale(x)) * gmod + shift).astype(BF16)
        cos = cos_ref[rows, :]
        sin = sin_ref[rows, :]

        zq = _dot(hb, w_ref[:, 0:_O_K])
        q = _head_norm_rope(zq, gq_ref[...], ones_ref[...], cos, sin)
        qt_ref[:, rows] = q.T.astype(BF16)

        zkv = _dot(hb, w_ref[:, _O_K:_O_F])
        k = _head_norm_rope(zkv[:, 0:KV_WIDTH], gk_ref[...],
                            ones_ref[0:KV_WIDTH, 0:KV_WIDTH], cos, sin)
        k_sw = pltpu.roll(k, HEAD_DIM, 1)
        first = lax.broadcasted_iota(jnp.int32, k.shape, 1) < HEAD_DIM
        kp_ref[rows, :] = jnp.concatenate(
            [jnp.where(first, k, 0.0), jnp.where(first, 0.0, k_sw),
             jnp.where(first, k_sw, 0.0), jnp.where(first, 0.0, k)], axis=-1).astype(BF16)
        zvt = zkv[:, KV_WIDTH:2 * KV_WIDTH].T
        ones_rows = jnp.ones((HEAD_DIM, zvt.shape[1]), F32)
        vt_ref[:, rows] = jnp.concatenate(
            [piece for kv in range(N_KV_HEADS)
             for piece in (zvt[kv * HEAD_DIM:(kv + 1) * HEAD_DIM, :], ones_rows)],
            axis=0).astype(BF16)

        f_ref[rows, :] = _dot(hb, w_ref[:, _O_F:_O_GA]).astype(BF16)


def _in_proj(x2, mod3, g_pre, w_in, gq, gk, cos_t, sin_t, ones_blk, batch, seq, tm):
    t = batch * seq
    per_b = seq // tm
    row = lambda i: (i, 0)
    const = lambda i: (0, 0)
    return pl.pallas_call(
        _in_proj_kernel,
        grid=(t // tm,),
        in_specs=[
            pl.BlockSpec((tm, D_MODEL), row),
            pl.BlockSpec((None, 6, D_MODEL), lambda i: (i // per_b, 0, 0)),
            pl.BlockSpec((1, D_MODEL), const),
            pl.BlockSpec((D_MODEL, _O_GA), const),
            pl.BlockSpec((1, ATTN_WIDTH), const),
            pl.BlockSpec((1, KV_WIDTH), const),
            pl.BlockSpec((tm, LANES), lambda i: (i % per_b, 0)),
            pl.BlockSpec((tm, LANES), lambda i: (i % per_b, 0)),
            pl.BlockSpec((ATTN_WIDTH, ATTN_WIDTH), const),
        ],
        out_specs=[
            pl.BlockSpec((None, ATTN_WIDTH, tm), lambda i: (i // per_b, 0, i % per_b)),
            pl.BlockSpec((tm, 2 * N_KV_HEADS * LANES), row),
            pl.BlockSpec((None, N_KV_HEADS * LANES, tm), lambda i: (i // per_b, 0, i % per_b)),
            pl.BlockSpec((tm, FOURIER_WIDTH), row),
        ],
        out_shape=[
            jax.ShapeDtypeStruct((batch, ATTN_WIDTH, seq), BF16),
            jax.ShapeDtypeStruct((t, 2 * N_KV_HEADS * LANES), BF16),
            jax.ShapeDtypeStruct((batch, N_KV_HEADS * LANES, seq), BF16),
            jax.ShapeDtypeStruct((t, FOURIER_WIDTH), BF16),
        ],
        scratch_shapes=[pltpu.VMEM((D_MODEL, _O_GA), BF16)],
        compiler_params=_params(1),
        name="in_proj",
    )(x2, mod3, g_pre, w_in, gq, gk, cos_t, sin_t, ones_blk)


_ATTN_SLOTS = 2


def _attn_step(qt_ref, kp_ref, vt_ref, o_ref, p_ref, s_new_ref, s_old_ref):
    outs = []
    for h in range(N_Q_HEADS):
        kv = h // Q_PER_KV
        slot = h % _ATTN_SLOTS
        pair = h // 2
        variant = 2 * kv + h % 2
        s_new_ref[h] = _dot(kp_ref[:, variant * LANES:(variant + 1) * LANES],
                            qt_ref[pair * LANES:(pair + 1) * LANES, :])
        s = s_old_ref[h]
        m = jnp.max(s, axis=0, keepdims=True)
        p_ref[slot] = jnp.exp2(s - m).astype(BF16)
        ov = _dot(vt_ref[kv * LANES:(kv + 1) * LANES, :], p_ref[slot])
        on = ov[0:HEAD_DIM, :] / ov[HEAD_DIM:2 * HEAD_DIM, :]
        outs.append(on.T)
    o_ref[...] = jnp.concatenate(outs, axis=-1).astype(BF16)


def _attn_kernel(qt_ref, kp_ref, vt_ref, o_ref, p_ref, sa_ref, sb_ref):
    g = pl.program_id(0)

    @pl.when(g == 0)
    def _():
        sb_ref[...] = jnp.zeros(sb_ref.shape, F32)

    @pl.when(g % 2 == 0)
    def _():
        _attn_step(qt_ref, kp_ref, vt_ref, o_ref, p_ref, sa_ref, sb_ref)

    @pl.when(g % 2 == 1)
    def _():
        _attn_step(qt_ref, kp_ref, vt_ref, o_ref, p_ref, sb_ref, sa_ref)


def _attention(qt, kp, vt, batch, seq, tq):
    t = batch * seq
    per_b = seq // tq
    n_blk = t // tq
    cur = lambda g: jnp.minimum(g, n_blk - 1)
    prev = lambda g: jnp.maximum(g - 1, 0)
    return pl.pallas_call(
        _attn_kernel,
        grid=(n_blk + 1,),
        in_specs=[
            pl.BlockSpec((None, ATTN_WIDTH, tq), lambda g: (cur(g) // per_b, 0, cur(g) % per_b)),
            pl.BlockSpec((seq, 2 * N_KV_HEADS * LANES), lambda g: (cur(g) // per_b, 0)),
            pl.BlockSpec((None, N_KV_HEADS * LANES, seq), lambda g: (prev(g) // per_b, 0, 0)),
        ],
        out_specs=pl.BlockSpec((tq, ATTN_WIDTH), lambda g: (prev(g), 0)),
        out_shape=jax.ShapeDtypeStruct((t, ATTN_WIDTH), BF16),
        scratch_shapes=[pltpu.VMEM((_ATTN_SLOTS, seq, tq), BF16),
                        pltpu.VMEM((N_Q_HEADS, seq, tq), F32),
                        pltpu.VMEM((N_Q_HEADS, seq, tq), F32)],
        compiler_params=_params(1),
        name="attn",
    )(qt, kp, vt)


_DFT_SPLIT = 32
_REV_BLOCK = 256


def _fourier_kernel(f_ref, cs_ref, jw_ref, ca_ref, sa_ref, cb_ref, sb_ref, o_ref,
                    w_ref, x_ref, r_ref):
    seq = f_ref.shape[0]
    half = seq // 2
    gd = FOURIER_GROUP_DIM
    fw = FOURIER_WIDTH

    @pl.when(pl.program_id(0) == 0)
    def _():
        cb = cb_ref[...]
        sb = sb_ref[...]

        def body(s1, carry):
            ca = ca_ref[pl.ds(s1, 1), :]
            sa = sa_ref[pl.ds(s1, 1), :]
            rows = pl.ds(pl.multiple_of(s1 * _DFT_SPLIT, _DFT_SPLIT), _DFT_SPLIT)
            w_ref[rows, 0:half] = (ca * cb - sa * sb).astype(BF16)
            w_ref[rows, half:seq] = (-(sa * cb + ca * sb)).astype(BF16)
            return carry

        lax.fori_loop(0, seq // _DFT_SPLIT, body, 0)

    cs = cs_ref[...].astype(BF16)
    for g in range(N_FOURIER_GROUPS):
        xcs = _dot(f_ref[:, g * gd:(g + 1) * gd], cs)
        x_ref[:, g * gd:(g + 1) * gd] = xcs[:, 0:gd].astype(BF16)
        x_ref[:, fw + g * gd:fw + (g + 1) * gd] = xcs[:, gd:2 * gd].astype(BF16)

    n_blk = seq // _REV_BLOCK
    jw = jw_ref[...].astype(BF16)
    for a in range(half // _REV_BLOCK):
        beta = n_blk - 1 - a
        nxt = (beta + 1) % n_blk
        win = jnp.concatenate([x_ref[beta * _REV_BLOCK:(beta + 1) * _REV_BLOCK, :],
                               x_ref[nxt * _REV_BLOCK:(nxt + 1) * _REV_BLOCK, :]], axis=0)
        rev = _dot(jw, win)
        top = x_ref[a * _REV_BLOCK:(a + 1) * _REV_BLOCK, :].astype(F32)
        rows = slice(a * _REV_BLOCK, (a + 1) * _REV_BLOCK)
        r_ref[rows, :] = (top[:, 0:fw] + rev[:, 0:fw]).astype(BF16)
        r_ref[half + a * _REV_BLOCK:half + (a + 1) * _REV_BLOCK, :] = (
            top[:, fw:2 * fw] - rev[:, fw:2 * fw]).astype(BF16)

    tm = 512
    x_mid = x_ref[half:half + 1, 0:fw].astype(F32)
    odd = (lax.broadcasted_iota(jnp.int32, (tm, fw), 0) & 1) == 1
    parity_term = jnp.where(odd, -x_mid, x_mid)
    for mi in range(seq // tm):
        acc = _dot(w_ref[mi * tm:(mi + 1) * tm, :], r_ref[...])
        o_ref[mi * tm:(mi + 1) * tm, :] = (acc + parity_term).astype(BF16)


def _fourier(f, cs_c, jw, ca, sa, cb, sb, batch, seq):
    t = batch * seq
    const = lambda b: (0, 0)
    return pl.pallas_call(
        _fourier_kernel,
        grid=(batch,),
        in_specs=[
            pl.BlockSpec((seq, FOURIER_WIDTH), lambda b: (b, 0)),
            pl.BlockSpec(cs_c.shape, const),
            pl.BlockSpec(jw.shape, const),
            pl.BlockSpec(ca.shape, const),
            pl.BlockSpec(sa.shape, const),
            pl.BlockSpec(cb.shape, const),
            pl.BlockSpec(sb.shape, const),
        ],
        out_specs=pl.BlockSpec((seq, FOURIER_WIDTH), lambda b: (b, 0)),
        out_shape=jax.ShapeDtypeStruct((t, FOURIER_WIDTH), BF16),
        scratch_shapes=[pltpu.VMEM((seq, seq), BF16),
                        pltpu.VMEM((seq, 2 * FOURIER_WIDTH), BF16),
                        pltpu.VMEM((seq, FOURIER_WIDTH), BF16)],
        compiler_params=_params(1),
        name="fourier",
    )(f, cs_c, jw, ca, sa, cb, sb)


def _merge_kernel(o_ref, yf_ref, x_ref, mod_ref, gpre1_ref, win32_ref, bg_ref,
                  wa32_ref, wf32_ref, wo32_ref, gpost_ref, gpre2_ref,
                  x1_ref, h2_ref, wg_ref, wa_ref, wf_ref, wo_ref):
    @pl.when(pl.program_id(0) == 0)
    def _():
        wg_ref[...] = win32_ref[:, _O_GA:IN_WIDTH].astype(BF16)

    _cast_once(wa32_ref, wa_ref)
    _cast_once(wf32_ref, wf_ref)
    _cast_once(wo32_ref, wo_ref)
    shift1 = mod_ref[0:1, :]
    scale1 = mod_ref[1:2, :]
    gate1 = mod_ref[2:3, :]
    shift2 = mod_ref[3:4, :]
    scale2 = mod_ref[4:5, :]
    gmod1 = gpre1_ref[...] * (1.0 + scale1)
    gpost = gate1 * gpost_ref[...]
    gmod2 = gpre2_ref[...] * (1.0 + scale2)
    for r in range(x_ref.shape[0] // _ROW_CHUNK):
        rows = slice(r * _ROW_CHUNK, (r + 1) * _ROW_CHUNK)
        x = x_ref[rows, :]
        hb = ((x * _rms_scale(x)) * gmod1 + shift1).astype(BF16)
        ta = _dot(hb, wg_ref[:, 0:D_MODEL]) + bg_ref[:, 0:D_MODEL]
        ga = 1.0 / (1.0 + jnp.exp(-ta))
        tf = _dot(hb, wg_ref[:, D_MODEL:2 * D_MODEL]) + bg_ref[:, D_MODEL:2 * D_MODEL]
        gf = 1.0 / (1.0 + jnp.exp(-tf))
        ya = _dot(o_ref[rows, :], wa_ref[...])
        yf = _dot(yf_ref[rows, :], wf_ref[...])
        y = (ga * ya + gf * yf).astype(BF16)
        y2 = _dot(y, wo_ref[...])
        x1 = x + (y2 * _rms_scale(y2)) * gpost
        x1_ref[rows, :] = x1
        h2_ref[rows, :] = ((x1 * _rms_scale(x1)) * gmod2 + shift2).astype(BF16)


def _merge(o, yf, x2, mod3, g_pre1, w_in, b_gate, wa, wf, wo, g_post, g_pre2, seq, tm):
    t = x2.shape[0]
    per_b = seq // tm
    row = lambda i: (i, 0)
    const = lambda i: (0, 0)
    return pl.pallas_call(
        _merge_kernel,
        grid=(t // tm,),
        in_specs=[
            pl.BlockSpec((tm, ATTN_WIDTH), row),
            pl.BlockSpec((tm, FOURIER_WIDTH), row),
            pl.BlockSpec((tm, D_MODEL), row),
            pl.BlockSpec((None, 6, D_MODEL), lambda i: (i // per_b, 0, 0)),
            pl.BlockSpec((1, D_MODEL), const),
            pl.BlockSpec((D_MODEL, IN_WIDTH), const),
            pl.BlockSpec((1, 2 * D_MODEL), const),
            pl.BlockSpec((ATTN_WIDTH, D_MODEL), const),
            pl.BlockSpec((FOURIER_WIDTH, D_MODEL), const),
            pl.BlockSpec((D_MODEL, D_MODEL), const),
            pl.BlockSpec((1, D_MODEL), const),
            pl.BlockSpec((1, D_MODEL), const),
        ],
        out_specs=[pl.BlockSpec((tm, D_MODEL), row), pl.BlockSpec((tm, D_MODEL), row)],
        out_shape=[jax.ShapeDtypeStruct((t, D_MODEL), F32),
                   jax.ShapeDtypeStruct((t, D_MODEL), BF16)],
        scratch_shapes=[pltpu.VMEM((D_MODEL, 2 * D_MODEL), BF16),
                        pltpu.VMEM((ATTN_WIDTH, D_MODEL), BF16),
                        pltpu.VMEM((FOURIER_WIDTH, D_MODEL), BF16),
                        pltpu.VMEM((D_MODEL, D_MODEL), BF16)],
        compiler_params=_params(1),
        name="merge",
    )(o, yf, x2, mod3, g_pre1, w_in, b_gate, wa, wf, wo, g_post, g_pre2)


_HALO = BF16_SUBLANES
_MXU_COLS = 256
_GELU_K1 = math.sqrt(2.0 / math.pi)
_GELU_K2 = _GELU_K1 * 0.044715


def _ffn_up_proj(h_ref, hp_ref, hn_ref, w_ref, hx_ref, hperm_ref, tiles_per_seq):
    tm = h_ref.shape[0]
    ext = tm + 2 * _HALO
    pitch = ext // 8
    i = pl.program_id(0)
    keep_prev = jnp.where(i % tiles_per_seq != 0, 1.0, 0.0)
    keep_next = jnp.where(i % tiles_per_seq != tiles_per_seq - 1, 1.0, 0.0)

    for c in range(D_MODEL // LANES):
        lc = slice(c * LANES, (c + 1) * LANES)
        hx_ref[c, 0:_HALO, :] = hp_ref[:, lc].astype(F32) * keep_prev
        hx_ref[c, _HALO:_HALO + tm, :] = h_ref[:, lc].astype(F32)
        hx_ref[c, _HALO + tm:ext, :] = hn_ref[:, lc].astype(F32) * keep_next
    for c in range(D_MODEL // LANES):
        lc = slice(c * LANES, (c + 1) * LANES)
        for b in range(0, pitch, 2):
            pair = jnp.concatenate([hx_ref[c, pl.ds(b, 8, stride=pitch), :],
                                    hx_ref[c, pl.ds(b + 1, 8, stride=pitch), :]], axis=0)
            hperm_ref[8 * b:8 * b + 16, lc] = pair.astype(BF16)
    hperm = hperm_ref[...]

    def pair_thunk(jj):
        def run(u_ref):
            for half in range(2):
                cols = slice(half * D_FF + jj * _MXU_COLS, half * D_FF + (jj + 1) * _MXU_COLS)
                u_ref[:, cols] = _dot(hperm, w_ref[:, cols])
        return run

    return [pair_thunk(jj) for jj in range(D_FF // _MXU_COLS)]


def _ffn_up_conv(u_ref, cw_ref, cb_ref, o_ref, st_ref, jj):
    tm = o_ref.shape[0]
    ext = tm + 2 * _HALO
    pitch = ext // 8
    for k in range(_MXU_COLS // LANES):
        j = 2 * jj + k
        la = slice(j * LANES, (j + 1) * LANES)
        lb = slice(D_FF + j * LANES, D_FF + (j + 1) * LANES)
        bc = lambda ref, r, l: jnp.broadcast_to(ref[r:r + 1, l], (8, LANES))
        wa = [bc(cw_ref, r, la) for r in range(CONV_WIDTH)]
        wb = [bc(cw_ref, r, lb) for r in range(CONV_WIDTH)]
        ba = bc(cb_ref, 0, la)
        bb = bc(cb_ref, 0, lb)

        def vreg(lanes, b):
            if b < 0:
                return pltpu.roll(u_ref[8 * (pitch - 1):8 * pitch, lanes], 1, 0)
            if b >= pitch:
                return pltpu.roll(u_ref[0:8, lanes], 7, 0)
            return u_ref[8 * b:8 * b + 8, lanes]

        for b in range(pitch):
            ca = vreg(la, b - 1) * wa[0] + vreg(la, b) * wa[1] + vreg(la, b + 1) * wa[2] + ba
            cv = vreg(lb, b - 1) * wb[0] + vreg(lb, b) * wb[1] + vreg(lb, b + 1) * wb[2] + bb
            t = jnp.tanh(ca * (_GELU_K1 + _GELU_K2 * (ca * ca)))
            st_ref[j % 2, pl.ds(b, 8, stride=pitch), :] = (ca * cv) * (1.0 + t)
        o_ref[:, la] = st_ref[j % 2, _HALO:_HALO + tm, :].astype(BF16)


def _ffn_up_kernel(h_ref, hp_ref, hn_ref, w_ref, cw_ref, cb_ref, o_ref,
                   hx_ref, hperm_ref, st_ref, u0_ref, u1_ref, *, tiles_per_seq, n_tiles):
    i = pl.program_id(0)
    u_refs = (u0_ref, u1_ref)
    n_pairs = D_FF // _MXU_COLS

    @pl.when(i == 0)
    def _():
        for thunk in _ffn_up_proj(h_ref, hp_ref, hn_ref, w_ref, hx_ref, hperm_ref, tiles_per_seq):
            thunk(u0_ref)

    for par in range(2):
        @pl.when((i > 0) & (i < n_tiles) & (i % 2 == par))
        def _(par=par):
            thunks = _ffn_up_proj(h_ref, hp_ref, hn_ref, w_ref, hx_ref, hperm_ref, tiles_per_seq)
            for jj in range(n_pairs):
                thunks[jj](u_refs[par])
                _ffn_up_conv(u_refs[1 - par], cw_ref, cb_ref, o_ref, st_ref, jj)

    @pl.when(i == n_tiles)
    def _():
        for jj in range(n_pairs):
            _ffn_up_conv(u_refs[(n_tiles - 1) % 2], cw_ref, cb_ref, o_ref, st_ref, jj)


def _ffn_up(h2, w_up, conv_w, conv_b, seq, tm):
    t = h2.shape[0]
    n_tiles = t // tm
    tiles_per_seq = seq // tm
    halo_per_tile = tm // _HALO
    n_halo = t // _HALO
    ext = tm + 2 * _HALO
    const = lambda i: (0, 0)
    cur = lambda i: jnp.minimum(i, n_tiles - 1)
    return pl.pallas_call(
        functools.partial(_ffn_up_kernel, tiles_per_seq=tiles_per_seq, n_tiles=n_tiles),
        grid=(n_tiles + 1,),
        in_specs=[
            pl.BlockSpec((tm, D_MODEL), lambda i: (cur(i), 0)),
            pl.BlockSpec((_HALO, D_MODEL),
                         lambda i: (jnp.maximum(cur(i) * halo_per_tile - 1, 0), 0)),
            pl.BlockSpec((_HALO, D_MODEL),
                         lambda i: (jnp.minimum((cur(i) + 1) * halo_per_tile, n_halo - 1), 0)),
            pl.BlockSpec((D_MODEL, 2 * D_FF), const),
            pl.BlockSpec((CONV_WIDTH, 2 * D_FF), const),
            pl.BlockSpec((1, 2 * D_FF), const),
        ],
        out_specs=pl.BlockSpec((tm, D_FF), lambda i: (jnp.maximum(i - 1, 0), 0)),
        out_shape=jax.ShapeDtypeStruct((t, D_FF), BF16),
        scratch_shapes=[pltpu.VMEM((D_MODEL // LANES, ext, LANES), F32),
                        pltpu.VMEM((ext, D_MODEL), BF16),
                        pltpu.VMEM((2, ext, LANES), F32),
                        pltpu.VMEM((ext, 2 * D_FF), F32),
                        pltpu.VMEM((ext, 2 * D_FF), F32)],
        compiler_params=_params(1),
        name="ffn_up",
    )(h2, h2, h2, w_up, conv_w, conv_b)


def _ffn_down_kernel(a_ref, w32_ref, x1_ref, mod_ref, g_ref, o_ref, w_ref):
    _cast_once(w32_ref, w_ref)
    gpost = mod_ref[5:6, :] * g_ref[...]
    for r in range(a_ref.shape[0] // _ROW_CHUNK):
        rows = slice(r * _ROW_CHUNK, (r + 1) * _ROW_CHUNK)
        y = _dot(a_ref[rows, :], w_ref[...])
        o_ref[rows, :] = x1_ref[rows, :] + (y * _rms_scale(y)) * gpost


def _ffn_down(act, w_down, x1, mod3, g_post, seq, tm):
    t = x1.shape[0]
    per_b = seq // tm
    row = lambda i: (i, 0)
    const = lambda i: (0, 0)
    return pl.pallas_call(
        _ffn_down_kernel,
        grid=(t // tm,),
        in_specs=[
            pl.BlockSpec((tm, D_FF), row),
            pl.BlockSpec((D_FF, D_MODEL), const),
            pl.BlockSpec((tm, D_MODEL), row),
            pl.BlockSpec((None, 6, D_MODEL), lambda i: (i // per_b, 0, 0)),
            pl.BlockSpec((1, D_MODEL), const),
        ],
        out_specs=pl.BlockSpec((tm, D_MODEL), row),
        out_shape=jax.ShapeDtypeStruct((t, D_MODEL), F32),
        scratch_shapes=[pltpu.VMEM((D_FF, D_MODEL), BF16)],
        compiler_params=_params(1),
        name="ffn_down",
    )(act, w_down, x1, mod3, g_post)


@functools.lru_cache(maxsize=None)
def _rope_tables(seq):
    pos = np.arange(seq)
    row = (pos // GRID_W).astype(np.float64)
    col = (pos % GRID_W).astype(np.float64)
    inv = ROPE_THETA ** (-np.arange(ROPE_PAIRS_PER_AXIS, dtype=np.float64) / ROPE_PAIRS_PER_AXIS)
    ang = np.concatenate([row[:, None] * inv, col[:, None] * inv], axis=-1)
    cos = np.repeat(np.cos(ang), 2, axis=-1)
    sin = np.repeat(np.sin(ang), 2, axis=-1)
    sign = np.tile(np.array([-1.0, 1.0]), HEAD_DIM // 2)
    reps = LANES // HEAD_DIM
    return (np.tile(cos, (1, reps)).astype(np.float32),
            np.tile(sin * sign, (1, reps)).astype(np.float32))


@functools.lru_cache(maxsize=None)
def _dft_tables(seq):
    gd = FOURIER_GROUP_DIM
    kc = (np.outer(np.arange(gd), np.arange(gd)) % gd).astype(np.float64) * (2.0 * np.pi / gd)
    norm = 1.0 / math.sqrt(seq * gd)
    cs_c = np.concatenate([np.cos(kc), np.sin(kc)], axis=1) * norm
    n1 = seq // _DFT_SPLIT
    t_idx = np.arange(seq // 2)
    ang_a = (np.outer(np.arange(n1), t_idx) % n1).astype(np.float64) * (2.0 * np.pi / n1)
    ang_b = (np.outer(np.arange(_DFT_SPLIT), t_idx) % seq).astype(np.float64) * (2.0 * np.pi / seq)
    cos_b = np.cos(ang_b)
    cos_b[:, 0] *= 0.5
    jw = np.zeros((_REV_BLOCK, 2 * _REV_BLOCK))
    jw[np.arange(_REV_BLOCK), _REV_BLOCK - np.arange(_REV_BLOCK)] = 1.0
    f32 = lambda a: a.astype(np.float32)
    return (f32(cs_c), f32(jw), f32(np.cos(ang_a)), f32(np.sin(ang_a)), f32(cos_b),
            f32(np.sin(ang_b)))


def kernel(x, c, w_ada, b_ada, mix_pre_g, w_in, q_norm_g, k_norm_g, b_gate, w_attn_branch,
           w_fourier_branch, w_out, mix_post_g, ffn_pre_g, w_up, conv_w, conv_b, w_down,
           ffn_post_g):
    batch, seq, _ = x.shape
    depth = w_ada.shape[0]
    t = batch * seq
    tiles = _TILES

    cos_np, sin_np = _rope_tables(seq)
    cos_t = jnp.asarray(cos_np)
    sin_t = jnp.asarray(sin_np)
    dft_tabs = [jnp.asarray(a) for a in _dft_tables(seq)]
    seg = np.arange(ATTN_WIDTH) // HEAD_DIM
    ones_blk = jnp.asarray((seg[:, None] == seg[None, :]).astype(np.float32), dtype=BF16)

    x2 = x.reshape(t, D_MODEL)
    for l in range(depth):
        mod3 = _adaln(c, w_ada[l], b_ada[l]).reshape(batch, 6, D_MODEL)
        gq = (jnp.tile(q_norm_g[l], N_Q_HEADS) * (HEAD_DIM ** -0.5 * _LOG2_E)).reshape(1, ATTN_WIDTH)
        gk = jnp.tile(k_norm_g[l], N_KV_HEADS).reshape(1, KV_WIDTH)
        g_pre1 = mix_pre_g[l].reshape(1, D_MODEL)
        qt, kp, vt, f = _in_proj(
            x2, mod3, g_pre1, w_in[l], gq, gk, cos_t, sin_t, ones_blk, batch, seq,
            tiles["in_proj"])
        o = _attention(qt, kp, vt, batch, seq, tiles["attn"])
        yf = _fourier(f, *dft_tabs, batch, seq)
        x1, h2 = _merge(
            o, yf, x2, mod3, g_pre1, w_in[l], b_gate[l].reshape(1, 2 * D_MODEL),
            w_attn_branch[l], w_fourier_branch[l], w_out[l],
            mix_post_g[l].reshape(1, D_MODEL), ffn_pre_g[l].reshape(1, D_MODEL), seq,
            tiles["merge"])
        glu_half = jnp.concatenate([jnp.ones((D_FF,), F32), jnp.full((D_FF,), 0.5, F32)])
        act = _ffn_up(h2, w_up[l].astype(BF16), conv_w[l] * glu_half,
                      (conv_b[l] * glu_half).reshape(1, 2 * D_FF), seq, tiles["ffn_up"])
        x2 = _ffn_down(act, w_down[l], x1, mod3,
                       ffn_post_g[l].reshape(1, D_MODEL), seq, tiles["ffn_down"])
    return x2.reshape(batch, seq, D_MODEL)
```

```python
import functools
import math

import numpy as np
import jax
import jax.numpy as jnp
from jax import lax
from jax.experimental import pallas as pl
from jax.experimental.pallas import tpu as pltpu

D_MODEL = 1024
GRID_W = 64
HEAD_DIM = 64
N_Q_HEADS = 8
N_KV_HEADS = 2
Q_PER_KV = N_Q_HEADS // N_KV_HEADS
ATTN_WIDTH = N_Q_HEADS * HEAD_DIM
KV_WIDTH = N_KV_HEADS * HEAD_DIM
N_FOURIER_GROUPS = 4
FOURIER_GROUP_DIM = 128
FOURIER_WIDTH = N_FOURIER_GROUPS * FOURIER_GROUP_DIM
IN_WIDTH = ATTN_WIDTH + 2 * KV_WIDTH + FOURIER_WIDTH + 2 * D_MODEL
ROPE_THETA = 10000.0
ROPE_PAIRS_PER_AXIS = HEAD_DIM // 4
D_FF = 2816
CONV_WIDTH = 3
NORM_EPS = 1e-6

LANES = 128
BF16_SUBLANES = 16
VMEM_LIMIT_BYTES = 56 * 1024 * 1024
_ROW_CHUNK = 256
_TILES = {"in_proj": 1024, "attn": 256, "merge": 1024, "ffn_up": 512, "ffn_down": 1024}

_LOG2_E = math.log2(math.e)

F32 = jnp.float32
BF16 = jnp.bfloat16

_O_K = ATTN_WIDTH
_O_V = _O_K + KV_WIDTH
_O_F = _O_V + KV_WIDTH
_O_GA = _O_F + FOURIER_WIDTH
_O_GF = _O_GA + D_MODEL


def _params(n_grid_dims, flags=None):
    return pltpu.CompilerParams(
        dimension_semantics=("arbitrary",) * n_grid_dims,
        vmem_limit_bytes=VMEM_LIMIT_BYTES,
        flags=flags,
    )


def _dot(a, b):
    return jnp.dot(a, b, preferred_element_type=F32)


def _rms_scale(v):
    return lax.rsqrt(jnp.mean(v * v, axis=-1, keepdims=True) + NORM_EPS)


def _adaln_kernel(c_ref, w_ref, b_ref, o_ref):
    c = c_ref[...]
    a = c / (1.0 + jnp.exp(-c))
    a_hi = a.astype(BF16)
    a_lo = (a - a_hi.astype(F32)).astype(BF16)
    n = a.shape[0]
    res = _dot(jnp.concatenate([a_hi, a_lo], axis=0), w_ref[...].astype(BF16))
    o_ref[...] = res[0:n, :] + res[n:2 * n, :] + b_ref[...]


def _adaln(c, w_ada, b_ada):
    batch = c.shape[0]
    n = w_ada.shape[1]
    tn = 1536
    return pl.pallas_call(
        _adaln_kernel,
        grid=(n // tn,),
        in_specs=[
            pl.BlockSpec((batch, D_MODEL), lambda j: (0, 0)),
            pl.BlockSpec((D_MODEL, tn), lambda j: (0, j)),
            pl.BlockSpec((1, tn), lambda j: (0, j)),
        ],
        out_specs=pl.BlockSpec((batch, tn), lambda j: (0, j)),
        out_shape=jax.ShapeDtypeStruct((batch, n), F32),
        compiler_params=_params(1),
        name="adaln",
    )(c, w_ada, b_ada.reshape(1, n))


def _swap_pairs(v):
    n = v.shape[-1]
    lane = lax.broadcasted_iota(jnp.int32, v.shape, v.ndim - 1)
    from_right = pltpu.roll(v, n - 1, v.ndim - 1)
    from_left = pltpu.roll(v, 1, v.ndim - 1)
    return jnp.where((lane & 1) == 0, from_right, from_left)


def _head_norm_rope(z, gain, ones_blk, cos, sin):
    reps = z.shape[-1] // LANES
    ssq = _dot((z * z).astype(BF16), ones_blk)
    r = lax.rsqrt(ssq * (1.0 / HEAD_DIM) + NORM_EPS)
    zg = z * gain
    cos_t = jnp.concatenate([cos] * reps, axis=-1) if reps > 1 else cos
    sin_t = jnp.concatenate([sin] * reps, axis=-1) if reps > 1 else sin
    return (zg * cos_t + _swap_pairs(zg) * sin_t) * r


def _cast_once(src_ref, dst_ref):
    @pl.when(pl.program_id(0) == 0)
    def _():
        dst_ref[...] = src_ref[...].astype(BF16)


def _in_proj_kernel(x_ref, mod_ref, g_ref, w32_ref, gq_ref, gk_ref,
                    cos_ref, sin_ref, ones_ref,
                    qt_ref, kp_ref, vt_ref, f_ref, w_ref):
    _cast_once(w32_ref, w_ref)
    shift = mod_ref[0:1, :]
    scale = mod_ref[1:2, :]
    gmod = g_ref[...] * (1.0 + scale)
    for r in range(x_ref.shape[0] // _ROW_CHUNK):
        rows = slice(r * _ROW_CHUNK, (r + 1) * _ROW_CHUNK)
        x = x_ref[rows, :]
        hb = ((x * _rms_scale(x)) * gmod + shift).astype(BF16)
        cos = cos_ref[rows, :]
        sin = sin_ref[rows, :]

        zq = _dot(hb, w_ref[:, 0:_O_K])
        q = _head_norm_rope(zq, gq_ref[...], ones_ref[...], cos, sin)
        qt_ref[:, rows] = q.T.astype(BF16)

        zkv = _dot(hb, w_ref[:, _O_K:_O_F])
        k = _head_norm_rope(zkv[:, 0:KV_WIDTH], gk_ref[...],
                            ones_ref[0:KV_WIDTH, 0:KV_WIDTH], cos, sin)
        k_sw = pltpu.roll(k, HEAD_DIM, 1)
        first = lax.broadcasted_iota(jnp.int32, k.shape, 1) < HEAD_DIM
        kp_ref[rows, :] = jnp.concatenate(
            [jnp.where(first, k, 0.0), jnp.where(first, 0.0, k_sw),
             jnp.where(first, k_sw, 0.0), jnp.where(first, 0.0, k)], axis=-1).astype(BF16)
        zvt = zkv[:, KV_WIDTH:2 * KV_WIDTH].T
        ones_rows = jnp.ones((HEAD_DIM, zvt.shape[1]), F32)
        vt_ref[:, rows] = jnp.concatenate(
            [piece for kv in range(N_KV_HEADS)
             for piece in (zvt[kv * HEAD_DIM:(kv + 1) * HEAD_DIM, :], ones_rows)],
            axis=0).astype(BF16)

        f_ref[rows, :] = _dot(hb, w_ref[:, _O_F:_O_GA]).astype(BF16)


def _in_proj(x2, mod3, g_pre, w_in, gq, gk, cos_t, sin_t, ones_blk, batch, seq, tm):
    t = batch * seq
    per_b = seq // tm
    row = lambda i: (i, 0)
    const = lambda i: (0, 0)
    return pl.pallas_call(
        _in_proj_kernel,
        grid=(t // tm,),
        in_specs=[
            pl.BlockSpec((tm, D_MODEL), row),
            pl.BlockSpec((None, 6, D_MODEL), lambda i: (i // per_b, 0, 0)),
            pl.BlockSpec((1, D_MODEL), const),
            pl.BlockSpec((D_MODEL, _O_GA), const),
            pl.BlockSpec((1, ATTN_WIDTH), const),
            pl.BlockSpec((1, KV_WIDTH), const),
            pl.BlockSpec((tm, LANES), lambda i: (i % per_b, 0)),
            pl.BlockSpec((tm, LANES), lambda i: (i % per_b, 0)),
            pl.BlockSpec((ATTN_WIDTH, ATTN_WIDTH), const),
        ],
        out_specs=[
            pl.BlockSpec((None, ATTN_WIDTH, tm), lambda i: (i // per_b, 0, i % per_b)),
            pl.BlockSpec((tm, 2 * N_KV_HEADS * LANES), row),
            pl.BlockSpec((None, N_KV_HEADS * LANES, tm), lambda i: (i // per_b, 0, i % per_b)),
            pl.BlockSpec((tm, FOURIER_WIDTH), row),
        ],
        out_shape=[
            jax.ShapeDtypeStruct((batch, ATTN_WIDTH, seq), BF16),
            jax.ShapeDtypeStruct((t, 2 * N_KV_HEADS * LANES), BF16),
            jax.ShapeDtypeStruct((batch, N_KV_HEADS * LANES, seq), BF16),
            jax.ShapeDtypeStruct((t, FOURIER_WIDTH), BF16),
        ],
        scratch_shapes=[pltpu.VMEM((D_MODEL, _O_GA), BF16)],
        compiler_params=_params(1),
        name="in_proj",
    )(x2, mod3, g_pre, w_in, gq, gk, cos_t, sin_t, ones_blk)


_ATTN_SLOTS = 2


def _attn_step(qt_ref, kp_ref, vt_ref, o_ref, p_ref, s_new_ref, s_old_ref):
    outs = []
    for h in range(N_Q_HEADS):
        kv = h // Q_PER_KV
        slot = h % _ATTN_SLOTS
        pair = h // 2
        variant = 2 * kv + h % 2
        s_new_ref[h] = _dot(kp_ref[:, variant * LANES:(variant + 1) * LANES],
                            qt_ref[pair * LANES:(pair + 1) * LANES, :])
        s = s_old_ref[h]
        m = jnp.max(s, axis=0, keepdims=True)
        p_ref[slot] = jnp.exp2(s - m).astype(BF16)
        ov = _dot(vt_ref[kv * LANES:(kv + 1) * LANES, :], p_ref[slot])
        on = ov[0:HEAD_DIM, :] / ov[HEAD_DIM:2 * HEAD_DIM, :]
        outs.append(on.T)
    o_ref[...] = jnp.concatenate(outs, axis=-1).astype(BF16)


def _attn_kernel(qt_ref, kp_ref, vt_ref, o_ref, p_ref, sa_ref, sb_ref):
    g = pl.program_id(0)

    @pl.when(g == 0)
    def _():
        sb_ref[...] = jnp.zeros(sb_ref.shape, F32)

    @pl.when(g % 2 == 0)
    def _():
        _attn_step(qt_ref, kp_ref, vt_ref, o_ref, p_ref, sa_ref, sb_ref)

    @pl.when(g % 2 == 1)
    def _():
        _attn_step(qt_ref, kp_ref, vt_ref, o_ref, p_ref, sb_ref, sa_ref)


def _attention(qt, kp, vt, batch, seq, tq):
    t = batch * seq
    per_b = seq // tq
    n_blk = t // tq
    cur = lambda g: jnp.minimum(g, n_blk - 1)
    prev = lambda g: jnp.maximum(g - 1, 0)
    return pl.pallas_call(
        _attn_kernel,
        grid=(n_blk + 1,),
        in_specs=[
            pl.BlockSpec((None, ATTN_WIDTH, tq), lambda g: (cur(g) // per_b, 0, cur(g) % per_b)),
            pl.BlockSpec((seq, 2 * N_KV_HEADS * LANES), lambda g: (cur(g) // per_b, 0)),
            pl.BlockSpec((None, N_KV_HEADS * LANES, seq), lambda g: (prev(g) // per_b, 0, 0)),
        ],
        out_specs=pl.BlockSpec((tq, ATTN_WIDTH), lambda g: (prev(g), 0)),
        out_shape=jax.ShapeDtypeStruct((t, ATTN_WIDTH), BF16),
        scratch_shapes=[pltpu.VMEM((_ATTN_SLOTS, seq, tq), BF16),
                        pltpu.VMEM((N_Q_HEADS, seq, tq), F32),
                        pltpu.VMEM((N_Q_HEADS, seq, tq), F32)],
        compiler_params=_params(1),
        name="attn",
    )(qt, kp, vt)


_DFT_SPLIT = 32
_REV_BLOCK = 256


def _fourier_kernel(f_ref, cs_ref, jw_ref, ca_ref, sa_ref, cb_ref, sb_ref, o_ref,
                    w_ref, x_ref, r_ref):
    seq = f_ref.shape[0]
    half = seq // 2
    gd = FOURIER_GROUP_DIM
    fw = FOURIER_WIDTH

    @pl.when(pl.program_id(0) == 0)
    def _():
        cb = cb_ref[...]
        sb = sb_ref[...]

        def body(s1, carry):
            ca = ca_ref[pl.ds(s1, 1), :]
            sa = sa_ref[pl.ds(s1, 1), :]
            rows = pl.ds(pl.multiple_of(s1 * _DFT_SPLIT, _DFT_SPLIT), _DFT_SPLIT)
            w_ref[rows, 0:half] = (ca * cb - sa * sb).astype(BF16)
            w_ref[rows, half:seq] = (-(sa * cb + ca * sb)).astype(BF16)
            return carry

        lax.fori_loop(0, seq // _DFT_SPLIT, body, 0)

    cs = cs_ref[...].astype(BF16)
    for g in range(N_FOURIER_GROUPS):
        xcs = _dot(f_ref[:, g * gd:(g + 1) * gd], cs)
        x_ref[:, g * gd:(g + 1) * gd] = xcs[:, 0:gd].astype(BF16)
        x_ref[:, fw + g * gd:fw + (g + 1) * gd] = xcs[:, gd:2 * gd].astype(BF16)

    n_blk = seq // _REV_BLOCK
    jw = jw_ref[...].astype(BF16)
    for a in range(half // _REV_BLOCK):
        beta = n_blk - 1 - a
        nxt = (beta + 1) % n_blk
        win = jnp.concatenate([x_ref[beta * _REV_BLOCK:(beta + 1) * _REV_BLOCK, :],
                               x_ref[nxt * _REV_BLOCK:(nxt + 1) * _REV_BLOCK, :]], axis=0)
        rev = _dot(jw, win)
        top = x_ref[a * _REV_BLOCK:(a + 1) * _REV_BLOCK, :].astype(F32)
        rows = slice(a * _REV_BLOCK, (a + 1) * _REV_BLOCK)
        r_ref[rows, :] = (top[:, 0:fw] + rev[:, 0:fw]).astype(BF16)
        r_ref[half + a * _REV_BLOCK:half + (a + 1) * _REV_BLOCK, :] = (
            top[:, fw:2 * fw] - rev[:, fw:2 * fw]).astype(BF16)

    tm = 512
    x_mid = x_ref[half:half + 1, 0:fw].astype(F32)
    odd = (lax.broadcasted_iota(jnp.int32, (tm, fw), 0) & 1) == 1
    parity_term = jnp.where(odd, -x_mid, x_mid)
    for mi in range(seq // tm):
        acc = _dot(w_ref[mi * tm:(mi + 1) * tm, :], r_ref[...])
        o_ref[mi * tm:(mi + 1) * tm, :] = (acc + parity_term).astype(BF16)


def _fourier(f, cs_c, jw, ca, sa, cb, sb, batch, seq):
    t = batch * seq
    const = lambda b: (0, 0)
    return pl.pallas_call(
        _fourier_kernel,
        grid=(batch,),
        in_specs=[
            pl.BlockSpec((seq, FOURIER_WIDTH), lambda b: (b, 0)),
            pl.BlockSpec(cs_c.shape, const),
            pl.BlockSpec(jw.shape, const),
            pl.BlockSpec(ca.shape, const),
            pl.BlockSpec(sa.shape, const),
            pl.BlockSpec(cb.shape, const),
            pl.BlockSpec(sb.shape, const),
        ],
        out_specs=pl.BlockSpec((seq, FOURIER_WIDTH), lambda b: (b, 0)),
        out_shape=jax.ShapeDtypeStruct((t, FOURIER_WIDTH), BF16),
        scratch_shapes=[pltpu.VMEM((seq, seq), BF16),
                        pltpu.VMEM((seq, 2 * FOURIER_WIDTH), BF16),
                        pltpu.VMEM((seq, FOURIER_WIDTH), BF16)],
        compiler_params=_params(1),
        name="fourier",
    )(f, cs_c, jw, ca, sa, cb, sb)


def _merge_kernel(o_ref, yf_ref, x_ref, mod_ref, gpre1_ref, wg_ref, bg_ref,
                  wa32_ref, wf32_ref, wo32_ref, gpost_ref, gpre2_ref,
                  x1_ref, h2_ref, wa_ref, wf_ref, wo_ref):
    _cast_once(wa32_ref, wa_ref)
    _cast_once(wf32_ref, wf_ref)
    _cast_once(wo32_ref, wo_ref)
    shift1 = mod_ref[0:1, :]
    scale1 = mod_ref[1:2, :]
    gate1 = mod_ref[2:3, :]
    shift2 = mod_ref[3:4, :]
    scale2 = mod_ref[4:5, :]
    gmod1 = gpre1_ref[...] * (1.0 + scale1)
    gpost = gate1 * gpost_ref[...]
    gmod2 = gpre2_ref[...] * (1.0 + scale2)
    for r in range(x_ref.shape[0] // _ROW_CHUNK):
        rows = slice(r * _ROW_CHUNK, (r + 1) * _ROW_CHUNK)
        x = x_ref[rows, :]
        hb = ((x * _rms_scale(x)) * gmod1 + shift1).astype(BF16)
        ta = _dot(hb, wg_ref[:, 0:D_MODEL]) + bg_ref[:, 0:D_MODEL]
        ga = 1.0 / (1.0 + jnp.exp(-ta))
        tf = _dot(hb, wg_ref[:, D_MODEL:2 * D_MODEL]) + bg_ref[:, D_MODEL:2 * D_MODEL]
        gf = 1.0 / (1.0 + jnp.exp(-tf))
        ya = _dot(o_ref[rows, :], wa_ref[...])
        yf = _dot(yf_ref[rows, :], wf_ref[...])
        y = (ga * ya + gf * yf).astype(BF16)
        y2 = _dot(y, wo_ref[...])
        x1 = x + (y2 * _rms_scale(y2)) * gpost
        x1_ref[rows, :] = x1
        h2_ref[rows, :] = ((x1 * _rms_scale(x1)) * gmod2 + shift2).astype(BF16)


def _merge(o, yf, x2, mod3, g_pre1, w_gate, b_gate, wa, wf, wo, g_post, g_pre2, seq, tm):
    t = x2.shape[0]
    per_b = seq // tm
    row = lambda i: (i, 0)
    const = lambda i: (0, 0)
    return pl.pallas_call(
        _merge_kernel,
        grid=(t // tm,),
        in_specs=[
            pl.BlockSpec((tm, ATTN_WIDTH), row),
            pl.BlockSpec((tm, FOURIER_WIDTH), row),
            pl.BlockSpec((tm, D_MODEL), row),
            pl.BlockSpec((None, 6, D_MODEL), lambda i: (i // per_b, 0, 0)),
            pl.BlockSpec((1, D_MODEL), const),
            pl.BlockSpec((D_MODEL, 2 * D_MODEL), const),
            pl.BlockSpec((1, 2 * D_MODEL), const),
            pl.BlockSpec((ATTN_WIDTH, D_MODEL), const),
            pl.BlockSpec((FOURIER_WIDTH, D_MODEL), const),
            pl.BlockSpec((D_MODEL, D_MODEL), const),
            pl.BlockSpec((1, D_MODEL), const),
            pl.BlockSpec((1, D_MODEL), const),
        ],
        out_specs=[pl.BlockSpec((tm, D_MODEL), row), pl.BlockSpec((tm, D_MODEL), row)],
        out_shape=[jax.ShapeDtypeStruct((t, D_MODEL), F32),
                   jax.ShapeDtypeStruct((t, D_MODEL), BF16)],
        scratch_shapes=[pltpu.VMEM((ATTN_WIDTH, D_MODEL), BF16),
                        pltpu.VMEM((FOURIER_WIDTH, D_MODEL), BF16),
                        pltpu.VMEM((D_MODEL, D_MODEL), BF16)],
        compiler_params=_params(1),
        name="merge",
    )(o, yf, x2, mod3, g_pre1, w_gate, b_gate, wa, wf, wo, g_post, g_pre2)


_HALO = BF16_SUBLANES
_MXU_COLS = 256
_GELU_K1 = math.sqrt(2.0 / math.pi)
_GELU_K2 = _GELU_K1 * 0.044715


def _ffn_up_proj(h_ref, hp_ref, hn_ref, w_ref, hx_ref, hperm_ref, tiles_per_seq):
    tm = h_ref.shape[0]
    ext = tm + 2 * _HALO
    pitch = ext // 8
    i = pl.program_id(0)
    keep_prev = jnp.where(i % tiles_per_seq != 0, 1.0, 0.0)
    keep_next = jnp.where(i % tiles_per_seq != tiles_per_seq - 1, 1.0, 0.0)

    for c in range(D_MODEL // LANES):
        lc = slice(c * LANES, (c + 1) * LANES)
        hx_ref[c, 0:_HALO, :] = hp_ref[:, lc].astype(F32) * keep_prev
        hx_ref[c, _HALO:_HALO + tm, :] = h_ref[:, lc].astype(F32)
        hx_ref[c, _HALO + tm:ext, :] = hn_ref[:, lc].astype(F32) * keep_next
    for c in range(D_MODEL // LANES):
        lc = slice(c * LANES, (c + 1) * LANES)
        for b in range(0, pitch, 2):
            pair = jnp.concatenate([hx_ref[c, pl.ds(b, 8, stride=pitch), :],
                                    hx_ref[c, pl.ds(b + 1, 8, stride=pitch), :]], axis=0)
            hperm_ref[8 * b:8 * b + 16, lc] = pair.astype(BF16)
    hperm = hperm_ref[...]

    def pair_thunk(jj):
        def run(u_ref):
            for half in range(2):
                cols = slice(half * D_FF + jj * _MXU_COLS, half * D_FF + (jj + 1) * _MXU_COLS)
                u_ref[:, cols] = _dot(hperm, w_ref[:, cols])
        return run

    return [pair_thunk(jj) for jj in range(D_FF // _MXU_COLS)]


def _ffn_up_conv(u_ref, cw_ref, cb_ref, o_ref, st_ref, jj):
    tm = o_ref.shape[0]
    ext = tm + 2 * _HALO
    pitch = ext // 8
    for k in range(_MXU_COLS // LANES):
        j = 2 * jj + k
        la = slice(j * LANES, (j + 1) * LANES)
        lb = slice(D_FF + j * LANES, D_FF + (j + 1) * LANES)
        bc = lambda ref, r, l: jnp.broadcast_to(ref[r:r + 1, l], (8, LANES))
        wa = [bc(cw_ref, r, la) for r in range(CONV_WIDTH)]
        wb = [bc(cw_ref, r, lb) for r in range(CONV_WIDTH)]
        ba = bc(cb_ref, 0, la)
        bb = bc(cb_ref, 0, lb)

        def vreg(lanes, b):
            if b < 0:
                return pltpu.roll(u_ref[8 * (pitch - 1):8 * pitch, lanes], 1, 0)
            if b >= pitch:
                return pltpu.roll(u_ref[0:8, lanes], 7, 0)
            return u_ref[8 * b:8 * b + 8, lanes]

        for b in range(pitch):
            ca = vreg(la, b - 1) * wa[0] + vreg(la, b) * wa[1] + vreg(la, b + 1) * wa[2] + ba
            cv = vreg(lb, b - 1) * wb[0] + vreg(lb, b) * wb[1] + vreg(lb, b + 1) * wb[2] + bb
            t = jnp.tanh(ca * (_GELU_K1 + _GELU_K2 * (ca * ca)))
            st_ref[j % 2, pl.ds(b, 8, stride=pitch), :] = (ca * cv) * (1.0 + t)
        o_ref[:, la] = st_ref[j % 2, _HALO:_HALO + tm, :].astype(BF16)


def _ffn_up_kernel(h_ref, hp_ref, hn_ref, w_ref, cw_ref, cb_ref, o_ref,
                   hx_ref, hperm_ref, st_ref, u0_ref, u1_ref, *, tiles_per_seq, n_tiles):
    i = pl.program_id(0)
    u_refs = (u0_ref, u1_ref)
    n_pairs = D_FF // _MXU_COLS

    @pl.when(i == 0)
    def _():
        for thunk in _ffn_up_proj(h_ref, hp_ref, hn_ref, w_ref, hx_ref, hperm_ref, tiles_per_seq):
            thunk(u0_ref)

    for par in range(2):
        @pl.when((i > 0) & (i < n_tiles) & (i % 2 == par))
        def _(par=par):
            thunks = _ffn_up_proj(h_ref, hp_ref, hn_ref, w_ref, hx_ref, hperm_ref, tiles_per_seq)
            for jj in range(n_pairs):
                thunks[jj](u_refs[par])
                _ffn_up_conv(u_refs[1 - par], cw_ref, cb_ref, o_ref, st_ref, jj)

    @pl.when(i == n_tiles)
    def _():
        for jj in range(n_pairs):
            _ffn_up_conv(u_refs[(n_tiles - 1) % 2], cw_ref, cb_ref, o_ref, st_ref, jj)


def _ffn_up(h2, w_up, conv_w, conv_b, seq, tm):
    t = h2.shape[0]
    n_tiles = t // tm
    tiles_per_seq = seq // tm
    halo_per_tile = tm // _HALO
    n_halo = t // _HALO
    ext = tm + 2 * _HALO
    const = lambda i: (0, 0)
    cur = lambda i: jnp.minimum(i, n_tiles - 1)
    return pl.pallas_call(
        functools.partial(_ffn_up_kernel, tiles_per_seq=tiles_per_seq, n_tiles=n_tiles),
        grid=(n_tiles + 1,),
        in_specs=[
            pl.BlockSpec((tm, D_MODEL), lambda i: (cur(i), 0)),
            pl.BlockSpec((_HALO, D_MODEL),
                         lambda i: (jnp.maximum(cur(i) * halo_per_tile - 1, 0), 0)),
            pl.BlockSpec((_HALO, D_MODEL),
                         lambda i: (jnp.minimum((cur(i) + 1) * halo_per_tile, n_halo - 1), 0)),
            pl.BlockSpec((D_MODEL, 2 * D_FF), const),
            pl.BlockSpec((CONV_WIDTH, 2 * D_FF), const),
            pl.BlockSpec((1, 2 * D_FF), const),
        ],
        out_specs=pl.BlockSpec((tm, D_FF), lambda i: (jnp.maximum(i - 1, 0), 0)),
        out_shape=jax.ShapeDtypeStruct((t, D_FF), BF16),
        scratch_shapes=[pltpu.VMEM((D_MODEL // LANES, ext, LANES), F32),
                        pltpu.VMEM((ext, D_MODEL), BF16),
                        pltpu.VMEM((2, ext, LANES), F32),
                        pltpu.VMEM((ext, 2 * D_FF), F32),
                        pltpu.VMEM((ext, 2 * D_FF), F32)],
        compiler_params=_params(1),
        name="ffn_up",
    )(h2, h2, h2, w_up, conv_w, conv_b)


def _ffn_down_kernel(a_ref, w32_ref, x1_ref, mod_ref, g_ref, o_ref, w_ref):
    _cast_once(w32_ref, w_ref)
    gpost = mod_ref[5:6, :] * g_ref[...]
    for r in range(a_ref.shape[0] // _ROW_CHUNK):
        rows = slice(r * _ROW_CHUNK, (r + 1) * _ROW_CHUNK)
        y = _dot(a_ref[rows, :], w_ref[...])
        o_ref[rows, :] = x1_ref[rows, :] + (y * _rms_scale(y)) * gpost


def _ffn_down(act, w_down, x1, mod3, g_post, seq, tm):
    t = x1.shape[0]
    per_b = seq // tm
    row = lambda i: (i, 0)
    const = lambda i: (0, 0)
    return pl.pallas_call(
        _ffn_down_kernel,
        grid=(t // tm,),
        in_specs=[
            pl.BlockSpec((tm, D_FF), row),
            pl.BlockSpec((D_FF, D_MODEL), const),
            pl.BlockSpec((tm, D_MODEL), row),
            pl.BlockSpec((None, 6, D_MODEL), lambda i: (i // per_b, 0, 0)),
            pl.BlockSpec((1, D_MODEL), const),
        ],
        out_specs=pl.BlockSpec((tm, D_MODEL), row),
        out_shape=jax.ShapeDtypeStruct((t, D_MODEL), F32),
        scratch_shapes=[pltpu.VMEM((D_FF, D_MODEL), BF16)],
        compiler_params=_params(1),
        name="ffn_down",
    )(act, w_down, x1, mod3, g_post)


@functools.lru_cache(maxsize=None)
def _rope_tables(seq):
    pos = np.arange(seq)
    row = (pos // GRID_W).astype(np.float64)
    col = (pos % GRID_W).astype(np.float64)
    inv = ROPE_THETA ** (-np.arange(ROPE_PAIRS_PER_AXIS, dtype=np.float64) / ROPE_PAIRS_PER_AXIS)
    ang = np.concatenate([row[:, None] * inv, col[:, None] * inv], axis=-1)
    cos = np.repeat(np.cos(ang), 2, axis=-1)
    sin = np.repeat(np.sin(ang), 2, axis=-1)
    sign = np.tile(np.array([-1.0, 1.0]), HEAD_DIM // 2)
    reps = LANES // HEAD_DIM
    return (np.tile(cos, (1, reps)).astype(np.float32),
            np.tile(sin * sign, (1, reps)).astype(np.float32))


@functools.lru_cache(maxsize=None)
def _dft_tables(seq):
    gd = FOURIER_GROUP_DIM
    kc = (np.outer(np.arange(gd), np.arange(gd)) % gd).astype(np.float64) * (2.0 * np.pi / gd)
    norm = 1.0 / math.sqrt(seq * gd)
    cs_c = np.concatenate([np.cos(kc), np.sin(kc)], axis=1) * norm
    n1 = seq // _DFT_SPLIT
    t_idx = np.arange(seq // 2)
    ang_a = (np.outer(np.arange(n1), t_idx) % n1).astype(np.float64) * (2.0 * np.pi / n1)
    ang_b = (np.outer(np.arange(_DFT_SPLIT), t_idx) % seq).astype(np.float64) * (2.0 * np.pi / seq)
    cos_b = np.cos(ang_b)
    cos_b[:, 0] *= 0.5
    jw = np.zeros((_REV_BLOCK, 2 * _REV_BLOCK))
    jw[np.arange(_REV_BLOCK), _REV_BLOCK - np.arange(_REV_BLOCK)] = 1.0
    f32 = lambda a: a.astype(np.float32)
    return (f32(cs_c), f32(jw), f32(np.cos(ang_a)), f32(np.sin(ang_a)), f32(cos_b),
            f32(np.sin(ang_b)))


def kernel(x, c, w_ada, b_ada, mix_pre_g, w_in, q_norm_g, k_norm_g, b_gate, w_attn_branch,
           w_fourier_branch, w_out, mix_post_g, ffn_pre_g, w_up, conv_w, conv_b, w_down,
           ffn_post_g):
    batch, seq, _ = x.shape
    depth = w_ada.shape[0]
    t = batch * seq
    tiles = _TILES

    cos_np, sin_np = _rope_tables(seq)
    cos_t = jnp.asarray(cos_np)
    sin_t = jnp.asarray(sin_np)
    dft_tabs = [jnp.asarray(a) for a in _dft_tables(seq)]
    seg = np.arange(ATTN_WIDTH) // HEAD_DIM
    ones_blk = jnp.asarray((seg[:, None] == seg[None, :]).astype(np.float32), dtype=BF16)

    x2 = x.reshape(t, D_MODEL)
    for l in range(depth):
        mod3 = _adaln(c, w_ada[l], b_ada[l]).reshape(batch, 6, D_MODEL)
        gq = (jnp.tile(q_norm_g[l], N_Q_HEADS) * (HEAD_DIM ** -0.5 * _LOG2_E)).reshape(1, ATTN_WIDTH)
        gk = jnp.tile(k_norm_g[l], N_KV_HEADS).reshape(1, KV_WIDTH)
        g_pre1 = mix_pre_g[l].reshape(1, D_MODEL)
        qt, kp, vt, f = _in_proj(
            x2, mod3, g_pre1, w_in[l], gq, gk, cos_t, sin_t, ones_blk, batch, seq,
            tiles["in_proj"])
        o = _attention(qt, kp, vt, batch, seq, tiles["attn"])
        yf = _fourier(f, *dft_tabs, batch, seq)
        x1, h2 = _merge(
            o, yf, x2, mod3, g_pre1, w_in[l, :, _O_GA:].astype(BF16),
            b_gate[l].reshape(1, 2 * D_MODEL),
            w_attn_branch[l], w_fourier_branch[l], w_out[l],
            mix_post_g[l].reshape(1, D_MODEL), ffn_pre_g[l].reshape(1, D_MODEL), seq,
            tiles["merge"])
        glu_half = jnp.concatenate([jnp.ones((D_FF,), F32), jnp.full((D_FF,), 0.5, F32)])
        act = _ffn_up(h2, w_up[l].astype(BF16), conv_w[l] * glu_half,
                      (conv_b[l] * glu_half).reshape(1, 2 * D_FF), seq, tiles["ffn_up"])
        x2 = _ffn_down(act, w_down[l], x1, mod3,
                       ffn_post_g[l].reshape(1, D_MODEL), seq, tiles["ffn_down"])
    return x2.reshape(batch, seq, D_MODEL)
```

```python
import functools
import math

import numpy as np
import jax
import jax.numpy as jnp
from jax import lax
from jax.experimental import pallas as pl
from jax.experimental.pallas import tpu as pltpu

D_MODEL = 1024
GRID_W = 64
HEAD_DIM = 64
N_Q_HEADS = 8
N_KV_HEADS = 2
Q_PER_KV = N_Q_HEADS // N_KV_HEADS
ATTN_WIDTH = N_Q_HEADS * HEAD_DIM
KV_WIDTH = N_KV_HEADS * HEAD_DIM
N_FOURIER_GROUPS = 4
FOURIER_GROUP_DIM = 128
FOURIER_WIDTH = N_FOURIER_GROUPS * FOURIER_GROUP_DIM
IN_WIDTH = ATTN_WIDTH + 2 * KV_WIDTH + FOURIER_WIDTH + 2 * D_MODEL
ROPE_THETA = 10000.0
ROPE_PAIRS_PER_AXIS = HEAD_DIM // 4
D_FF = 2816
CONV_WIDTH = 3
NORM_EPS = 1e-6

LANES = 128
BF16_SUBLANES = 16
VMEM_LIMIT_BYTES = 56 * 1024 * 1024
_ROW_CHUNK = 256
_TILES = {"in_proj": 1024, "attn": 256, "merge": 1024, "ffn_up": 512, "ffn_down": 1024}

_LOG2_E = math.log2(math.e)

F32 = jnp.float32
BF16 = jnp.bfloat16

_O_K = ATTN_WIDTH
_O_V = _O_K + KV_WIDTH
_O_F = _O_V + KV_WIDTH
_O_GA = _O_F + FOURIER_WIDTH
_O_GF = _O_GA + D_MODEL


def _params(n_grid_dims, flags=None):
    return pltpu.CompilerParams(
        dimension_semantics=("arbitrary",) * n_grid_dims,
        vmem_limit_bytes=VMEM_LIMIT_BYTES,
        flags=flags,
    )


def _dot(a, b):
    return jnp.dot(a, b, preferred_element_type=F32)


def _rms_scale(v):
    return lax.rsqrt(jnp.mean(v * v, axis=-1, keepdims=True) + NORM_EPS)


def _adaln_kernel(c_ref, w_ref, b_ref, o_ref):
    c = c_ref[...]
    a = c / (1.0 + jnp.exp(-c))
    a_hi = a.astype(BF16)
    a_lo = (a - a_hi.astype(F32)).astype(BF16)
    n = a.shape[0]
    res = _dot(jnp.concatenate([a_hi, a_lo], axis=0), w_ref[...].astype(BF16))
    o_ref[...] = res[0:n, :] + res[n:2 * n, :] + b_ref[...]


def _adaln(c, w_ada, b_ada):
    batch = c.shape[0]
    n = w_ada.shape[1]
    tn = 1536
    return pl.pallas_call(
        _adaln_kernel,
        grid=(n // tn,),
        in_specs=[
            pl.BlockSpec((batch, D_MODEL), lambda j: (0, 0)),
            pl.BlockSpec((D_MODEL, tn), lambda j: (0, j)),
            pl.BlockSpec((1, tn), lambda j: (0, j)),
        ],
        out_specs=pl.BlockSpec((batch, tn), lambda j: (0, j)),
        out_shape=jax.ShapeDtypeStruct((batch, n), F32),
        compiler_params=_params(1),
        name="adaln",
    )(c, w_ada, b_ada.reshape(1, n))


def _swap_pairs(v):
    n = v.shape[-1]
    lane = lax.broadcasted_iota(jnp.int32, v.shape, v.ndim - 1)
    from_right = pltpu.roll(v, n - 1, v.ndim - 1)
    from_left = pltpu.roll(v, 1, v.ndim - 1)
    return jnp.where((lane & 1) == 0, from_right, from_left)


def _head_norm_rope(z, gain, ones_blk, cos, sin):
    reps = z.shape[-1] // LANES
    ssq = _dot((z * z).astype(BF16), ones_blk)
    r = lax.rsqrt(ssq * (1.0 / HEAD_DIM) + NORM_EPS)
    zg = z * gain
    cos_t = jnp.concatenate([cos] * reps, axis=-1) if reps > 1 else cos
    sin_t = jnp.concatenate([sin] * reps, axis=-1) if reps > 1 else sin
    return (zg * cos_t + _swap_pairs(zg) * sin_t) * r


def _cast_once(src_ref, dst_ref):
    @pl.when(pl.program_id(0) == 0)
    def _():
        dst_ref[...] = src_ref[...].astype(BF16)


def _in_proj_kernel(x_ref, mod_ref, g_ref, w32_ref, gq_ref, gk_ref,
                    cos_ref, sin_ref, ones_ref,
                    qt_ref, kp_ref, vt_ref, f_ref, w_ref):
    _cast_once(w32_ref, w_ref)
    shift = mod_ref[0:1, :]
    scale = mod_ref[1:2, :]
    gmod = g_ref[...] * (1.0 + scale)
    for r in range(x_ref.shape[0] // _ROW_CHUNK):
        rows = slice(r * _ROW_CHUNK, (r + 1) * _ROW_CHUNK)
        x = x_ref[rows, :]
        hb = ((x * _rms_scale(x)) * gmod + shift).astype(BF16)
        cos = cos_ref[rows, :]
        sin = sin_ref[rows, :]

        zq = _dot(hb, w_ref[:, 0:_O_K])
        q = _head_norm_rope(zq, gq_ref[...], ones_ref[...], cos, sin)
        qt_ref[:, rows] = q.T.astype(BF16)

        zkv = _dot(hb, w_ref[:, _O_K:_O_F])
        k = _head_norm_rope(zkv[:, 0:KV_WIDTH], gk_ref[...],
                            ones_ref[0:KV_WIDTH, 0:KV_WIDTH], cos, sin)
        k_sw = pltpu.roll(k, HEAD_DIM, 1)
        first = lax.broadcasted_iota(jnp.int32, k.shape, 1) < HEAD_DIM
        kp_ref[rows, :] = jnp.concatenate(
            [jnp.where(first, k, 0.0), jnp.where(first, 0.0, k_sw),
             jnp.where(first, k_sw, 0.0), jnp.where(first, 0.0, k)], axis=-1).astype(BF16)
        zvt = zkv[:, KV_WIDTH:2 * KV_WIDTH].T
        ones_rows = jnp.ones((HEAD_DIM, zvt.shape[1]), F32)
        vt_ref[:, rows] = jnp.concatenate(
            [piece for kv in range(N_KV_HEADS)
             for piece in (zvt[kv * HEAD_DIM:(kv + 1) * HEAD_DIM, :], ones_rows)],
            axis=0).astype(BF16)

        f_ref[rows, :] = _dot(hb, w_ref[:, _O_F:_O_GA]).astype(BF16)


def _in_proj(x2, mod3, g_pre, w_in, gq, gk, cos_t, sin_t, ones_blk, batch, seq, tm):
    t = batch * seq
    per_b = seq // tm
    row = lambda i: (i, 0)
    const = lambda i: (0, 0)
    return pl.pallas_call(
        _in_proj_kernel,
        grid=(t // tm,),
        in_specs=[
            pl.BlockSpec((tm, D_MODEL), row),
            pl.BlockSpec((None, 6, D_MODEL), lambda i: (i // per_b, 0, 0)),
            pl.BlockSpec((1, D_MODEL), const),
            pl.BlockSpec((D_MODEL, _O_GA), const),
            pl.BlockSpec((1, ATTN_WIDTH), const),
            pl.BlockSpec((1, KV_WIDTH), const),
            pl.BlockSpec((tm, LANES), lambda i: (i % per_b, 0)),
            pl.BlockSpec((tm, LANES), lambda i: (i % per_b, 0)),
            pl.BlockSpec((ATTN_WIDTH, ATTN_WIDTH), const),
        ],
        out_specs=[
            pl.BlockSpec((None, ATTN_WIDTH, tm), lambda i: (i // per_b, 0, i % per_b)),
            pl.BlockSpec((tm, 2 * N_KV_HEADS * LANES), row),
            pl.BlockSpec((None, N_KV_HEADS * LANES, tm), lambda i: (i // per_b, 0, i % per_b)),
            pl.BlockSpec((tm, FOURIER_WIDTH), row),
        ],
        out_shape=[
            jax.ShapeDtypeStruct((batch, ATTN_WIDTH, seq), BF16),
            jax.ShapeDtypeStruct((t, 2 * N_KV_HEADS * LANES), BF16),
            jax.ShapeDtypeStruct((batch, N_KV_HEADS * LANES, seq), BF16),
            jax.ShapeDtypeStruct((t, FOURIER_WIDTH), BF16),
        ],
        scratch_shapes=[pltpu.VMEM((D_MODEL, _O_GA), BF16)],
        compiler_params=_params(1),
        name="in_proj",
    )(x2, mod3, g_pre, w_in, gq, gk, cos_t, sin_t, ones_blk)


_ATTN_SLOTS = 2


def _attn_step(qt_ref, kp_ref, vt_ref, o_ref, p_ref, s_new_ref, s_old_ref):
    outs = []
    for h in range(N_Q_HEADS):
        kv = h // Q_PER_KV
        slot = h % _ATTN_SLOTS
        pair = h // 2
        variant = 2 * kv + h % 2
        s_new_ref[h] = _dot(kp_ref[:, variant * LANES:(variant + 1) * LANES],
                            qt_ref[pair * LANES:(pair + 1) * LANES, :])
        s = s_old_ref[h]
        m = jnp.max(s, axis=0, keepdims=True)
        p_ref[slot] = jnp.exp2(s - m).astype(BF16)
        ov = _dot(vt_ref[kv * LANES:(kv + 1) * LANES, :], p_ref[slot])
        on = ov[0:HEAD_DIM, :] / ov[HEAD_DIM:2 * HEAD_DIM, :]
        outs.append(on.T)
    o_ref[...] = jnp.concatenate(outs, axis=-1).astype(BF16)


def _attn_kernel(qt_ref, kp_ref, vt_ref, o_ref, p_ref, sa_ref, sb_ref):
    g = pl.program_id(0)

    @pl.when(g == 0)
    def _():
        sb_ref[...] = jnp.zeros(sb_ref.shape, F32)

    @pl.when(g % 2 == 0)
    def _():
        _attn_step(qt_ref, kp_ref, vt_ref, o_ref, p_ref, sa_ref, sb_ref)

    @pl.when(g % 2 == 1)
    def _():
        _attn_step(qt_ref, kp_ref, vt_ref, o_ref, p_ref, sb_ref, sa_ref)


def _attention(qt, kp, vt, batch, seq, tq):
    t = batch * seq
    per_b = seq // tq
    n_blk = t // tq
    cur = lambda g: jnp.minimum(g, n_blk - 1)
    prev = lambda g: jnp.maximum(g - 1, 0)
    return pl.pallas_call(
        _attn_kernel,
        grid=(n_blk + 1,),
        in_specs=[
            pl.BlockSpec((None, ATTN_WIDTH, tq), lambda g: (cur(g) // per_b, 0, cur(g) % per_b)),
            pl.BlockSpec((seq, 2 * N_KV_HEADS * LANES), lambda g: (cur(g) // per_b, 0)),
            pl.BlockSpec((None, N_KV_HEADS * LANES, seq), lambda g: (prev(g) // per_b, 0, 0)),
        ],
        out_specs=pl.BlockSpec((tq, ATTN_WIDTH), lambda g: (prev(g), 0)),
        out_shape=jax.ShapeDtypeStruct((t, ATTN_WIDTH), BF16),
        scratch_shapes=[pltpu.VMEM((_ATTN_SLOTS, seq, tq), BF16),
                        pltpu.VMEM((N_Q_HEADS, seq, tq), F32),
                        pltpu.VMEM((N_Q_HEADS, seq, tq), F32)],
        compiler_params=_params(1),
        name="attn",
    )(qt, kp, vt)


_DFT_SPLIT = 32
_REV_BLOCK = 256


def _fourier_kernel(f_ref, cs_ref, jw_ref, ca_ref, sa_ref, cb_ref, sb_ref, o_ref,
                    w_ref, x_ref, r_ref, d_ref):
    seq = f_ref.shape[0]
    half = seq // 2
    gd = FOURIER_GROUP_DIM
    fw = FOURIER_WIDTH

    @pl.when(pl.program_id(0) == 0)
    def _():
        cb = cb_ref[...]
        sb = sb_ref[...]

        def body(s1, carry):
            ca = ca_ref[pl.ds(s1, 1), :]
            sa = sa_ref[pl.ds(s1, 1), :]
            rows = pl.ds(pl.multiple_of(s1 * _DFT_SPLIT, _DFT_SPLIT), _DFT_SPLIT)
            w_ref[rows, 0:half] = (ca * cb - sa * sb).astype(BF16)
            w_ref[rows, half:seq] = (-(sa * cb + ca * sb)).astype(BF16)
            return carry

        lax.fori_loop(0, w_ref.shape[0] // _DFT_SPLIT, body, 0)

    cs = cs_ref[...].astype(BF16)
    for g in range(N_FOURIER_GROUPS):
        xcs = _dot(f_ref[:, g * gd:(g + 1) * gd], cs)
        x_ref[:, g * gd:(g + 1) * gd] = xcs[:, 0:gd].astype(BF16)
        x_ref[:, fw + g * gd:fw + (g + 1) * gd] = xcs[:, gd:2 * gd].astype(BF16)

    n_blk = seq // _REV_BLOCK
    jw = jw_ref[...].astype(BF16)
    for a in range(half // _REV_BLOCK):
        beta = n_blk - 1 - a
        nxt = (beta + 1) % n_blk
        win = jnp.concatenate([x_ref[beta * _REV_BLOCK:(beta + 1) * _REV_BLOCK, :],
                               x_ref[nxt * _REV_BLOCK:(nxt + 1) * _REV_BLOCK, :]], axis=0)
        rev = _dot(jw, win)
        top = x_ref[a * _REV_BLOCK:(a + 1) * _REV_BLOCK, :].astype(F32)
        rows = slice(a * _REV_BLOCK, (a + 1) * _REV_BLOCK)
        r_ref[rows, :] = (top[:, 0:fw] + rev[:, 0:fw]).astype(BF16)
        r_ref[half + a * _REV_BLOCK:half + (a + 1) * _REV_BLOCK, :] = (
            top[:, fw:2 * fw] - rev[:, fw:2 * fw]).astype(BF16)

    m_rows = d_ref.shape[0] - _REV_BLOCK + BF16_SUBLANES
    x_mid = x_ref[half:half + 1, 0:fw].astype(F32)
    d_ref[m_rows:d_ref.shape[0], :] = jnp.zeros((d_ref.shape[0] - m_rows, fw), BF16)
    split = half // 2
    for lo_r, hi_r in ((0, split), (split, m_rows)):
        n = hi_r - lo_r
        odd = (lax.broadcasted_iota(jnp.int32, (n, fw), 0) & 1) == 1
        a_part = _dot(w_ref[lo_r:hi_r, 0:half], r_ref[0:half, :]) + jnp.where(odd, -x_mid, x_mid)
        b_part = _dot(w_ref[lo_r:hi_r, half:seq], r_ref[half:seq, :])
        keep = min(hi_r, half) - lo_r
        o_ref[lo_r:lo_r + keep, :] = (a_part + b_part)[0:keep, :].astype(BF16)
        d_ref[lo_r:hi_r, :] = (a_part - b_part).astype(BF16)
    for a in range(half // _REV_BLOCK):
        beta = half // _REV_BLOCK - 1 - a
        win = d_ref[beta * _REV_BLOCK:(beta + 2) * _REV_BLOCK, :]
        o_ref[half + a * _REV_BLOCK:half + (a + 1) * _REV_BLOCK, :] = _dot(jw, win).astype(BF16)


def _fourier(f, cs_c, jw, ca, sa, cb, sb, batch, seq):
    t = batch * seq
    const = lambda b: (0, 0)
    return pl.pallas_call(
        _fourier_kernel,
        grid=(batch,),
        in_specs=[
            pl.BlockSpec((seq, FOURIER_WIDTH), lambda b: (b, 0)),
            pl.BlockSpec(cs_c.shape, const),
            pl.BlockSpec(jw.shape, const),
            pl.BlockSpec(ca.shape, const),
            pl.BlockSpec(sa.shape, const),
            pl.BlockSpec(cb.shape, const),
            pl.BlockSpec(sb.shape, const),
        ],
        out_specs=pl.BlockSpec((seq, FOURIER_WIDTH), lambda b: (b, 0)),
        out_shape=jax.ShapeDtypeStruct((t, FOURIER_WIDTH), BF16),
        scratch_shapes=[
            pltpu.VMEM(((seq // 2 // _DFT_SPLIT + 1) * _DFT_SPLIT, seq), BF16),
            pltpu.VMEM((seq, 2 * FOURIER_WIDTH), BF16),
            pltpu.VMEM((seq, FOURIER_WIDTH), BF16),
            pltpu.VMEM((seq // 2 + _REV_BLOCK, FOURIER_WIDTH), BF16)],
        compiler_params=_params(1),
        name="fourier",
    )(f, cs_c, jw, ca, sa, cb, sb)


def _merge_kernel(o_ref, yf_ref, x_ref, mod_ref, gpre1_ref, wg_ref, bg_ref,
                  wa32_ref, wf32_ref, wo32_ref, gpost_ref, gpre2_ref,
                  x1_ref, h2_ref, wa_ref, wf_ref, wo_ref):
    _cast_once(wa32_ref, wa_ref)
    _cast_once(wf32_ref, wf_ref)
    _cast_once(wo32_ref, wo_ref)
    shift1 = mod_ref[0:1, :]
    scale1 = mod_ref[1:2, :]
    gate1 = mod_ref[2:3, :]
    shift2 = mod_ref[3:4, :]
    scale2 = mod_ref[4:5, :]
    gmod1 = gpre1_ref[...] * (1.0 + scale1)
    gpost = gate1 * gpost_ref[...]
    gmod2 = gpre2_ref[...] * (1.0 + scale2)
    for r in range(x_ref.shape[0] // _ROW_CHUNK):
        rows = slice(r * _ROW_CHUNK, (r + 1) * _ROW_CHUNK)
        x = x_ref[rows, :]
        hb = ((x * _rms_scale(x)) * gmod1 + shift1).astype(BF16)
        ta = _dot(hb, wg_ref[:, 0:D_MODEL]) + bg_ref[:, 0:D_MODEL]
        ga = 1.0 / (1.0 + jnp.exp(-ta))
        tf = _dot(hb, wg_ref[:, D_MODEL:2 * D_MODEL]) + bg_ref[:, D_MODEL:2 * D_MODEL]
        gf = 1.0 / (1.0 + jnp.exp(-tf))
        ya = _dot(o_ref[rows, :], wa_ref[...])
        yf = _dot(yf_ref[rows, :], wf_ref[...])
        y = (ga * ya + gf * yf).astype(BF16)
        y2 = _dot(y, wo_ref[...])
        x1 = x + (y2 * _rms_scale(y2)) * gpost
        x1_ref[rows, :] = x1
        h2_ref[rows, :] = ((x1 * _rms_scale(x1)) * gmod2 + shift2).astype(BF16)


def _merge(o, yf, x2, mod3, g_pre1, w_gate, b_gate, wa, wf, wo, g_post, g_pre2, seq, tm):
    t = x2.shape[0]
    per_b = seq // tm
    row = lambda i: (i, 0)
    const = lambda i: (0, 0)
    return pl.pallas_call(
        _merge_kernel,
        grid=(t // tm,),
        in_specs=[
            pl.BlockSpec((tm, ATTN_WIDTH), row),
            pl.BlockSpec((tm, FOURIER_WIDTH), row),
            pl.BlockSpec((tm, D_MODEL), row),
            pl.BlockSpec((None, 6, D_MODEL), lambda i: (i // per_b, 0, 0)),
            pl.BlockSpec((1, D_MODEL), const),
            pl.BlockSpec((D_MODEL, 2 * D_MODEL), const),
            pl.BlockSpec((1, 2 * D_MODEL), const),
            pl.BlockSpec((ATTN_WIDTH, D_MODEL), const),
            pl.BlockSpec((FOURIER_WIDTH, D_MODEL), const),
            pl.BlockSpec((D_MODEL, D_MODEL), const),
            pl.BlockSpec((1, D_MODEL), const),
            pl.BlockSpec((1, D_MODEL), const),
        ],
        out_specs=[pl.BlockSpec((tm, D_MODEL), row), pl.BlockSpec((tm, D_MODEL), row)],
        out_shape=[jax.ShapeDtypeStruct((t, D_MODEL), F32),
                   jax.ShapeDtypeStruct((t, D_MODEL), BF16)],
        scratch_shapes=[pltpu.VMEM((ATTN_WIDTH, D_MODEL), BF16),
                        pltpu.VMEM((FOURIER_WIDTH, D_MODEL), BF16),
                        pltpu.VMEM((D_MODEL, D_MODEL), BF16)],
        compiler_params=_params(1),
        name="merge",
    )(o, yf, x2, mod3, g_pre1, w_gate, b_gate, wa, wf, wo, g_post, g_pre2)


_HALO = BF16_SUBLANES
_MXU_COLS = 256
_GELU_K1 = math.sqrt(2.0 / math.pi)
_GELU_K2 = _GELU_K1 * 0.044715


def _ffn_up_proj(h_ref, hp_ref, hn_ref, w_ref, hx_ref, hperm_ref, tiles_per_seq):
    tm = h_ref.shape[0]
    ext = tm + 2 * _HALO
    pitch = ext // 8
    i = pl.program_id(0)
    keep_prev = jnp.where(i % tiles_per_seq != 0, 1.0, 0.0)
    keep_next = jnp.where(i % tiles_per_seq != tiles_per_seq - 1, 1.0, 0.0)

    for c in range(D_MODEL // LANES):
        lc = slice(c * LANES, (c + 1) * LANES)
        hx_ref[c, 0:_HALO, :] = hp_ref[:, lc].astype(F32) * keep_prev
        hx_ref[c, _HALO:_HALO + tm, :] = h_ref[:, lc].astype(F32)
        hx_ref[c, _HALO + tm:ext, :] = hn_ref[:, lc].astype(F32) * keep_next
    for c in range(D_MODEL // LANES):
        lc = slice(c * LANES, (c + 1) * LANES)
        for b in range(0, pitch, 2):
            pair = jnp.concatenate([hx_ref[c, pl.ds(b, 8, stride=pitch), :],
                                    hx_ref[c, pl.ds(b + 1, 8, stride=pitch), :]], axis=0)
            hperm_ref[8 * b:8 * b + 16, lc] = pair.astype(BF16)
    hperm = hperm_ref[...]

    def pair_thunk(jj):
        def run(u_ref):
            for half in range(2):
                cols = slice(half * D_FF + jj * _MXU_COLS, half * D_FF + (jj + 1) * _MXU_COLS)
                u_ref[:, cols] = _dot(hperm, w_ref[:, cols])
        return run

    return [pair_thunk(jj) for jj in range(D_FF // _MXU_COLS)]


def _ffn_up_conv(u_ref, cw_ref, cb_ref, o_ref, st_ref, jj):
    tm = o_ref.shape[0]
    ext = tm + 2 * _HALO
    pitch = ext // 8
    for k in range(_MXU_COLS // LANES):
        j = 2 * jj + k
        la = slice(j * LANES, (j + 1) * LANES)
        lb = slice(D_FF + j * LANES, D_FF + (j + 1) * LANES)
        bc = lambda ref, r, l: jnp.broadcast_to(ref[r:r + 1, l], (8, LANES))
        wa = [bc(cw_ref, r, la) for r in range(CONV_WIDTH)]
        wb = [bc(cw_ref, r, lb) for r in range(CONV_WIDTH)]
        ba = bc(cb_ref, 0, la)
        bb = bc(cb_ref, 0, lb)

        def vreg(lanes, b):
            if b < 0:
                return pltpu.roll(u_ref[8 * (pitch - 1):8 * pitch, lanes], 1, 0)
            if b >= pitch:
                return pltpu.roll(u_ref[0:8, lanes], 7, 0)
            return u_ref[8 * b:8 * b + 8, lanes]

        for b in range(pitch):
            ca = vreg(la, b - 1) * wa[0] + vreg(la, b) * wa[1] + vreg(la, b + 1) * wa[2] + ba
            cv = vreg(lb, b - 1) * wb[0] + vreg(lb, b) * wb[1] + vreg(lb, b + 1) * wb[2] + bb
            t = jnp.tanh(ca * (_GELU_K1 + _GELU_K2 * (ca * ca)))
            st_ref[j % 2, pl.ds(b, 8, stride=pitch), :] = (ca * cv) * (1.0 + t)
        o_ref[:, la] = st_ref[j % 2, _HALO:_HALO + tm, :].astype(BF16)


def _ffn_up_kernel(h_ref, hp_ref, hn_ref, w_ref, cw_ref, cb_ref, o_ref,
                   hx_ref, hperm_ref, st_ref, u0_ref, u1_ref, *, tiles_per_seq, n_tiles):
    i = pl.program_id(0)
    u_refs = (u0_ref, u1_ref)
    n_pairs = D_FF // _MXU_COLS

    @pl.when(i == 0)
    def _():
        for thunk in _ffn_up_proj(h_ref, hp_ref, hn_ref, w_ref, hx_ref, hperm_ref, tiles_per_seq):
            thunk(u0_ref)

    for par in range(2):
        @pl.when((i > 0) & (i < n_tiles) & (i % 2 == par))
        def _(par=par):
            thunks = _ffn_up_proj(h_ref, hp_ref, hn_ref, w_ref, hx_ref, hperm_ref, tiles_per_seq)
            for jj in range(n_pairs):
                thunks[jj](u_refs[par])
                _ffn_up_conv(u_refs[1 - par], cw_ref, cb_ref, o_ref, st_ref, jj)

    @pl.when(i == n_tiles)
    def _():
        for jj in range(n_pairs):
            _ffn_up_conv(u_refs[(n_tiles - 1) % 2], cw_ref, cb_ref, o_ref, st_ref, jj)


def _ffn_up(h2, w_up, conv_w, conv_b, seq, tm):
    t = h2.shape[0]
    n_tiles = t // tm
    tiles_per_seq = seq // tm
    halo_per_tile = tm // _HALO
    n_halo = t // _HALO
    ext = tm + 2 * _HALO
    const = lambda i: (0, 0)
    cur = lambda i: jnp.minimum(i, n_tiles - 1)
    return pl.pallas_call(
        functools.partial(_ffn_up_kernel, tiles_per_seq=tiles_per_seq, n_tiles=n_tiles),
        grid=(n_tiles + 1,),
        in_specs=[
            pl.BlockSpec((tm, D_MODEL), lambda i: (cur(i), 0)),
            pl.BlockSpec((_HALO, D_MODEL),
                         lambda i: (jnp.maximum(cur(i) * halo_per_tile - 1, 0), 0)),
            pl.BlockSpec((_HALO, D_MODEL),
                         lambda i: (jnp.minimum((cur(i) + 1) * halo_per_tile, n_halo - 1), 0)),
            pl.BlockSpec((D_MODEL, 2 * D_FF), const),
            pl.BlockSpec((CONV_WIDTH, 2 * D_FF), const),
            pl.BlockSpec((1, 2 * D_FF), const),
        ],
        out_specs=pl.BlockSpec((tm, D_FF), lambda i: (jnp.maximum(i - 1, 0), 0)),
        out_shape=jax.ShapeDtypeStruct((t, D_FF), BF16),
        scratch_shapes=[pltpu.VMEM((D_MODEL // LANES, ext, LANES), F32),
                        pltpu.VMEM((ext, D_MODEL), BF16),
                        pltpu.VMEM((2, ext, LANES), F32),
                        pltpu.VMEM((ext, 2 * D_FF), F32),
                        pltpu.VMEM((ext, 2 * D_FF), F32)],
        compiler_params=_params(1),
        name="ffn_up",
    )(h2, h2, h2, w_up, conv_w, conv_b)


def _ffn_down_kernel(a_ref, w32_ref, x1_ref, mod_ref, g_ref, o_ref, w_ref):
    _cast_once(w32_ref, w_ref)
    gpost = mod_ref[5:6, :] * g_ref[...]
    for r in range(a_ref.shape[0] // _ROW_CHUNK):
        rows = slice(r * _ROW_CHUNK, (r + 1) * _ROW_CHUNK)
        y = _dot(a_ref[rows, :], w_ref[...])
        o_ref[rows, :] = x1_ref[rows, :] + (y * _rms_scale(y)) * gpost


def _ffn_down(act, w_down, x1, mod3, g_post, seq, tm):
    t = x1.shape[0]
    per_b = seq // tm
    row = lambda i: (i, 0)
    const = lambda i: (0, 0)
    return pl.pallas_call(
        _ffn_down_kernel,
        grid=(t // tm,),
        in_specs=[
            pl.BlockSpec((tm, D_FF), row),
            pl.BlockSpec((D_FF, D_MODEL), const),
            pl.BlockSpec((tm, D_MODEL), row),
            pl.BlockSpec((None, 6, D_MODEL), lambda i: (i // per_b, 0, 0)),
            pl.BlockSpec((1, D_MODEL), const),
        ],
        out_specs=pl.BlockSpec((tm, D_MODEL), row),
        out_shape=jax.ShapeDtypeStruct((t, D_MODEL), F32),
        scratch_shapes=[pltpu.VMEM((D_FF, D_MODEL), BF16)],
        compiler_params=_params(1),
        name="ffn_down",
    )(act, w_down, x1, mod3, g_post)


@functools.lru_cache(maxsize=None)
def _rope_tables(seq):
    pos = np.arange(seq)
    row = (pos // GRID_W).astype(np.float64)
    col = (pos % GRID_W).astype(np.float64)
    inv = ROPE_THETA ** (-np.arange(ROPE_PAIRS_PER_AXIS, dtype=np.float64) / ROPE_PAIRS_PER_AXIS)
    ang = np.concatenate([row[:, None] * inv, col[:, None] * inv], axis=-1)
    cos = np.repeat(np.cos(ang), 2, axis=-1)
    sin = np.repeat(np.sin(ang), 2, axis=-1)
    sign = np.tile(np.array([-1.0, 1.0]), HEAD_DIM // 2)
    reps = LANES // HEAD_DIM
    return (np.tile(cos, (1, reps)).astype(np.float32),
            np.tile(sin * sign, (1, reps)).astype(np.float32))


@functools.lru_cache(maxsize=None)
def _dft_tables(seq):
    gd = FOURIER_GROUP_DIM
    kc = (np.outer(np.arange(gd), np.arange(gd)) % gd).astype(np.float64) * (2.0 * np.pi / gd)
    norm = 1.0 / math.sqrt(seq * gd)
    cs_c = np.concatenate([np.cos(kc), np.sin(kc)], axis=1) * norm
    n1 = seq // _DFT_SPLIT
    t_idx = np.arange(seq // 2)
    ang_a = (np.outer(np.arange(n1), t_idx) % n1).astype(np.float64) * (2.0 * np.pi / n1)
    ang_b = (np.outer(np.arange(_DFT_SPLIT), t_idx) % seq).astype(np.float64) * (2.0 * np.pi / seq)
    cos_b = np.cos(ang_b)
    cos_b[:, 0] *= 0.5
    jw = np.zeros((_REV_BLOCK, 2 * _REV_BLOCK))
    jw[np.arange(_REV_BLOCK), _REV_BLOCK - np.arange(_REV_BLOCK)] = 1.0
    f32 = lambda a: a.astype(np.float32)
    return (f32(cs_c), f32(jw), f32(np.cos(ang_a)), f32(np.sin(ang_a)), f32(cos_b),
            f32(np.sin(ang_b)))


def kernel(x, c, w_ada, b_ada, mix_pre_g, w_in, q_norm_g, k_norm_g, b_gate, w_attn_branch,
           w_fourier_branch, w_out, mix_post_g, ffn_pre_g, w_up, conv_w, conv_b, w_down,
           ffn_post_g):
    batch, seq, _ = x.shape
    depth = w_ada.shape[0]
    t = batch * seq
    tiles = _TILES

    cos_np, sin_np = _rope_tables(seq)
    cos_t = jnp.asarray(cos_np)
    sin_t = jnp.asarray(sin_np)
    dft_tabs = [jnp.asarray(a) for a in _dft_tables(seq)]
    seg = np.arange(ATTN_WIDTH) // HEAD_DIM
    ones_blk = jnp.asarray((seg[:, None] == seg[None, :]).astype(np.float32), dtype=BF16)

    x2 = x.reshape(t, D_MODEL)
    for l in range(depth):
        mod3 = _adaln(c, w_ada[l], b_ada[l]).reshape(batch, 6, D_MODEL)
        gq = (jnp.tile(q_norm_g[l], N_Q_HEADS) * (HEAD_DIM ** -0.5 * _LOG2_E)).reshape(1, ATTN_WIDTH)
        gk = jnp.tile(k_norm_g[l], N_KV_HEADS).reshape(1, KV_WIDTH)
        g_pre1 = mix_pre_g[l].reshape(1, D_MODEL)
        qt, kp, vt, f = _in_proj(
            x2, mod3, g_pre1, w_in[l], gq, gk, cos_t, sin_t, ones_blk, batch, seq,
            tiles["in_proj"])
        o = _attention(qt, kp, vt, batch, seq, tiles["attn"])
        yf = _fourier(f, *dft_tabs, batch, seq)
        x1, h2 = _merge(
            o, yf, x2, mod3, g_pre1, w_in[l, :, _O_GA:].astype(BF16),
            b_gate[l].reshape(1, 2 * D_MODEL),
            w_attn_branch[l], w_fourier_branch[l], w_out[l],
            mix_post_g[l].reshape(1, D_MODEL), ffn_pre_g[l].reshape(1, D_MODEL), seq,
            tiles["merge"])
        glu_half = jnp.concatenate([jnp.ones((D_FF,), F32), jnp.full((D_FF,), 0.5, F32)])
        act = _ffn_up(h2, w_up[l].astype(BF16), conv_w[l] * glu_half,
                      (conv_b[l] * glu_half).reshape(1, 2 * D_FF), seq, tiles["ffn_up"])
        x2 = _ffn_down(act, w_down[l], x1, mod3,
                       ffn_post_g[l].reshape(1, D_MODEL), seq, tiles["ffn_down"])
    return x2.reshape(batch, seq, D_MODEL)
```

```python
import functools
import math

import numpy as np
import jax
import jax.numpy as jnp
from jax import lax
from jax.experimental import pallas as pl
from jax.experimental.pallas import tpu as pltpu

D_MODEL = 1024
GRID_W = 64
HEAD_DIM = 64
N_Q_HEADS = 8
N_KV_HEADS = 2
Q_PER_KV = N_Q_HEADS // N_KV_HEADS
ATTN_WIDTH = N_Q_HEADS * HEAD_DIM
KV_WIDTH = N_KV_HEADS * HEAD_DIM
N_FOURIER_GROUPS = 4
FOURIER_GROUP_DIM = 128
FOURIER_WIDTH = N_FOURIER_GROUPS * FOURIER_GROUP_DIM
IN_WIDTH = ATTN_WIDTH + 2 * KV_WIDTH + FOURIER_WIDTH + 2 * D_MODEL
ROPE_THETA = 10000.0
ROPE_PAIRS_PER_AXIS = HEAD_DIM // 4
D_FF = 2816
CONV_WIDTH = 3
NORM_EPS = 1e-6

LANES = 128
BF16_SUBLANES = 16
VMEM_LIMIT_BYTES = 56 * 1024 * 1024
_ROW_CHUNK = 256
_TILES = {"in_proj": 1024, "attn": 256, "merge": 1024, "ffn_up": 512, "ffn_down": 1024}

_LOG2_E = math.log2(math.e)

F32 = jnp.float32
BF16 = jnp.bfloat16

_O_K = ATTN_WIDTH
_O_V = _O_K + KV_WIDTH
_O_F = _O_V + KV_WIDTH
_O_GA = _O_F + FOURIER_WIDTH
_O_GF = _O_GA + D_MODEL


def _params(n_grid_dims, flags=None):
    return pltpu.CompilerParams(
        dimension_semantics=("arbitrary",) * n_grid_dims,
        vmem_limit_bytes=VMEM_LIMIT_BYTES,
        flags=flags,
    )


def _dot(a, b):
    return jnp.dot(a, b, preferred_element_type=F32)


def _rms_scale(v):
    return lax.rsqrt(jnp.mean(v * v, axis=-1, keepdims=True) + NORM_EPS)


def _adaln_kernel(c_ref, w_ref, b_ref, o_ref):
    c = c_ref[...]
    a = c / (1.0 + jnp.exp(-c))
    a_hi = a.astype(BF16)
    a_lo = (a - a_hi.astype(F32)).astype(BF16)
    n = a.shape[0]
    res = _dot(jnp.concatenate([a_hi, a_lo], axis=0), w_ref[...].astype(BF16))
    o_ref[...] = res[0:n, :] + res[n:2 * n, :] + b_ref[...]


def _adaln(c, w_ada, b_ada):
    batch = c.shape[0]
    n = w_ada.shape[1]
    tn = 1536
    return pl.pallas_call(
        _adaln_kernel,
        grid=(n // tn,),
        in_specs=[
            pl.BlockSpec((batch, D_MODEL), lambda j: (0, 0)),
            pl.BlockSpec((D_MODEL, tn), lambda j: (0, j)),
            pl.BlockSpec((1, tn), lambda j: (0, j)),
        ],
        out_specs=pl.BlockSpec((batch, tn), lambda j: (0, j)),
        out_shape=jax.ShapeDtypeStruct((batch, n), F32),
        compiler_params=_params(1),
        name="adaln",
    )(c, w_ada, b_ada.reshape(1, n))


def _swap_pairs(v):
    n = v.shape[-1]
    lane = lax.broadcasted_iota(jnp.int32, v.shape, v.ndim - 1)
    from_right = pltpu.roll(v, n - 1, v.ndim - 1)
    from_left = pltpu.roll(v, 1, v.ndim - 1)
    return jnp.where((lane & 1) == 0, from_right, from_left)


def _head_norm_rope(z, gain, ones_blk, cos, sin):
    reps = z.shape[-1] // LANES
    ssq = _dot((z * z).astype(BF16), ones_blk)
    r = lax.rsqrt(ssq * (1.0 / HEAD_DIM) + NORM_EPS)
    zg = z * gain
    cos_t = jnp.concatenate([cos] * reps, axis=-1) if reps > 1 else cos
    sin_t = jnp.concatenate([sin] * reps, axis=-1) if reps > 1 else sin
    return (zg * cos_t + _swap_pairs(zg) * sin_t) * r


def _cast_once(src_ref, dst_ref):
    @pl.when(pl.program_id(0) == 0)
    def _():
        dst_ref[...] = src_ref[...].astype(BF16)


def _in_proj_kernel(x_ref, mod_ref, g_ref, w32_ref, gq_ref, gk_ref,
                    cos_ref, sin_ref, ones_ref,
                    qt_ref, kp_ref, vt_ref, f_ref, w_ref):
    _cast_once(w32_ref, w_ref)
    shift = mod_ref[0:1, :]
    scale = mod_ref[1:2, :]
    gmod = g_ref[...] * (1.0 + scale)
    for r in range(x_ref.shape[0] // _ROW_CHUNK):
        rows = slice(r * _ROW_CHUNK, (r + 1) * _ROW_CHUNK)
        x = x_ref[rows, :]
        hb = ((x * _rms_scale(x)) * gmod + shift).astype(BF16)
        cos = cos_ref[rows, :]
        sin = sin_ref[rows, :]

        zq = _dot(hb, w_ref[:, 0:_O_K])
        q = _head_norm_rope(zq, gq_ref[...], ones_ref[...], cos, sin)
        qt_ref[:, rows] = q.T.astype(BF16)

        zkv = _dot(hb, w_ref[:, _O_K:_O_F])
        k = _head_norm_rope(zkv[:, 0:KV_WIDTH], gk_ref[...],
                            ones_ref[0:KV_WIDTH, 0:KV_WIDTH], cos, sin)
        k_sw = pltpu.roll(k, HEAD_DIM, 1)
        first = lax.broadcasted_iota(jnp.int32, k.shape, 1) < HEAD_DIM
        kp_ref[rows, :] = jnp.concatenate(
            [jnp.where(first, k, 0.0), jnp.where(first, 0.0, k_sw),
             jnp.where(first, k_sw, 0.0), jnp.where(first, 0.0, k)], axis=-1).astype(BF16)
        zvt = zkv[:, KV_WIDTH:2 * KV_WIDTH].T
        ones_rows = jnp.ones((HEAD_DIM, zvt.shape[1]), F32)
        vt_ref[:, rows] = jnp.concatenate(
            [piece for kv in range(N_KV_HEADS)
             for piece in (zvt[kv * HEAD_DIM:(kv + 1) * HEAD_DIM, :], ones_rows)],
            axis=0).astype(BF16)

        f_ref[rows, :] = _dot(hb, w_ref[:, _O_F:_O_GA]).astype(BF16)


def _in_proj(x2, mod3, g_pre, w_in, gq, gk, cos_t, sin_t, ones_blk, batch, seq, tm):
    t = batch * seq
    per_b = seq // tm
    row = lambda i: (i, 0)
    const = lambda i: (0, 0)
    return pl.pallas_call(
        _in_proj_kernel,
        grid=(t // tm,),
        in_specs=[
            pl.BlockSpec((tm, D_MODEL), row),
            pl.BlockSpec((None, 6, D_MODEL), lambda i: (i // per_b, 0, 0)),
            pl.BlockSpec((1, D_MODEL), const),
            pl.BlockSpec((D_MODEL, _O_GA), const),
            pl.BlockSpec((1, ATTN_WIDTH), const),
            pl.BlockSpec((1, KV_WIDTH), const),
            pl.BlockSpec((tm, LANES), lambda i: (i % per_b, 0)),
            pl.BlockSpec((tm, LANES), lambda i: (i % per_b, 0)),
            pl.BlockSpec((ATTN_WIDTH, ATTN_WIDTH), const),
        ],
        out_specs=[
            pl.BlockSpec((None, ATTN_WIDTH, tm), lambda i: (i // per_b, 0, i % per_b)),
            pl.BlockSpec((tm, 2 * N_KV_HEADS * LANES), row),
            pl.BlockSpec((None, N_KV_HEADS * LANES, tm), lambda i: (i // per_b, 0, i % per_b)),
            pl.BlockSpec((tm, FOURIER_WIDTH), row),
        ],
        out_shape=[
            jax.ShapeDtypeStruct((batch, ATTN_WIDTH, seq), BF16),
            jax.ShapeDtypeStruct((t, 2 * N_KV_HEADS * LANES), BF16),
            jax.ShapeDtypeStruct((batch, N_KV_HEADS * LANES, seq), BF16),
            jax.ShapeDtypeStruct((t, FOURIER_WIDTH), BF16),
        ],
        scratch_shapes=[pltpu.VMEM((D_MODEL, _O_GA), BF16)],
        compiler_params=_params(1),
        name="in_proj",
    )(x2, mod3, g_pre, w_in, gq, gk, cos_t, sin_t, ones_blk)


_ATTN_SLOTS = 2


def _attn_step(qt_ref, kp_ref, vt_ref, o_ref, p_ref, s_new_ref, s_old_ref):
    outs = []
    for h in range(N_Q_HEADS):
        kv = h // Q_PER_KV
        slot = h % _ATTN_SLOTS
        pair = h // 2
        variant = 2 * kv + h % 2
        s_new_ref[h] = _dot(kp_ref[:, variant * LANES:(variant + 1) * LANES],
                            qt_ref[pair * LANES:(pair + 1) * LANES, :])
        s = s_old_ref[h]
        m = jnp.max(s, axis=0, keepdims=True)
        p_ref[slot] = jnp.exp2(s - m).astype(BF16)
        ov = _dot(vt_ref[kv * LANES:(kv + 1) * LANES, :], p_ref[slot])
        on = ov[0:HEAD_DIM, :] / ov[HEAD_DIM:2 * HEAD_DIM, :]
        outs.append(on.T)
    o_ref[...] = jnp.concatenate(outs, axis=-1).astype(BF16)


def _attn_kernel(qt_ref, kp_ref, vt_ref, o_ref, p_ref, sa_ref, sb_ref):
    g = pl.program_id(0)

    @pl.when(g == 0)
    def _():
        sb_ref[...] = jnp.zeros(sb_ref.shape, F32)

    @pl.when(g % 2 == 0)
    def _():
        _attn_step(qt_ref, kp_ref, vt_ref, o_ref, p_ref, sa_ref, sb_ref)

    @pl.when(g % 2 == 1)
    def _():
        _attn_step(qt_ref, kp_ref, vt_ref, o_ref, p_ref, sb_ref, sa_ref)


def _attention(qt, kp, vt, batch, seq, tq):
    t = batch * seq
    per_b = seq // tq
    n_blk = t // tq
    cur = lambda g: jnp.minimum(g, n_blk - 1)
    prev = lambda g: jnp.maximum(g - 1, 0)
    return pl.pallas_call(
        _attn_kernel,
        grid=(n_blk + 1,),
        in_specs=[
            pl.BlockSpec((None, ATTN_WIDTH, tq), lambda g: (cur(g) // per_b, 0, cur(g) % per_b)),
            pl.BlockSpec((seq, 2 * N_KV_HEADS * LANES), lambda g: (cur(g) // per_b, 0)),
            pl.BlockSpec((None, N_KV_HEADS * LANES, seq), lambda g: (prev(g) // per_b, 0, 0)),
        ],
        out_specs=pl.BlockSpec((tq, ATTN_WIDTH), lambda g: (prev(g), 0)),
        out_shape=jax.ShapeDtypeStruct((t, ATTN_WIDTH), BF16),
        scratch_shapes=[pltpu.VMEM((_ATTN_SLOTS, seq, tq), BF16),
                        pltpu.VMEM((N_Q_HEADS, seq, tq), F32),
                        pltpu.VMEM((N_Q_HEADS, seq, tq), F32)],
        compiler_params=_params(1),
        name="attn",
    )(qt, kp, vt)


_DFT_SPLIT = 32
_REV_BLOCK = 256


def _fourier_kernel(f_ref, cs_ref, jw_ref, ca_ref, sa_ref, cb_ref, sb_ref, o_ref,
                    w_ref, x_ref, r_ref, d_ref):
    seq = f_ref.shape[0]
    half = seq // 2
    gd = FOURIER_GROUP_DIM
    fw = FOURIER_WIDTH

    @pl.when(pl.program_id(0) == 0)
    def _():
        cb = cb_ref[...]
        sb = sb_ref[...]

        def body(s1, carry):
            ca = ca_ref[pl.ds(s1, 1), :]
            sa = sa_ref[pl.ds(s1, 1), :]
            rows = pl.ds(pl.multiple_of(s1 * _DFT_SPLIT, _DFT_SPLIT), _DFT_SPLIT)
            w_ref[rows, 0:half] = (ca * cb - sa * sb).astype(BF16)
            w_ref[rows, half:seq] = (-(sa * cb + ca * sb)).astype(BF16)
            return carry

        lax.fori_loop(0, w_ref.shape[0] // _DFT_SPLIT, body, 0)

    cs = cs_ref[...].astype(BF16)
    for g in range(N_FOURIER_GROUPS):
        xcs = _dot(f_ref[:, g * gd:(g + 1) * gd], cs)
        x_ref[:, g * gd:(g + 1) * gd] = xcs[:, 0:gd].astype(BF16)
        x_ref[:, fw + g * gd:fw + (g + 1) * gd] = xcs[:, gd:2 * gd].astype(BF16)

    n_blk = seq // _REV_BLOCK
    jw = jw_ref[...].astype(BF16)
    for a in range(half // _REV_BLOCK):
        beta = n_blk - 1 - a
        nxt = (beta + 1) % n_blk
        win = jnp.concatenate([x_ref[beta * _REV_BLOCK:(beta + 1) * _REV_BLOCK, :],
                               x_ref[nxt * _REV_BLOCK:(nxt + 1) * _REV_BLOCK, :]], axis=0)
        rev = _dot(jw, win)
        top = x_ref[a * _REV_BLOCK:(a + 1) * _REV_BLOCK, :].astype(F32)
        rows = slice(a * _REV_BLOCK, (a + 1) * _REV_BLOCK)
        r_ref[rows, :] = (top[:, 0:fw] + rev[:, 0:fw]).astype(BF16)
        r_ref[half + a * _REV_BLOCK:half + (a + 1) * _REV_BLOCK, :] = (
            top[:, fw:2 * fw] - rev[:, fw:2 * fw]).astype(BF16)

    m_rows = d_ref.shape[0] - _REV_BLOCK + BF16_SUBLANES
    x_mid = x_ref[half:half + 1, 0:fw].astype(F32)
    d_ref[m_rows:d_ref.shape[0], :] = jnp.zeros((d_ref.shape[0] - m_rows, fw), BF16)
    split = half // 2
    for lo_r, hi_r in ((0, split), (split, m_rows)):
        n = hi_r - lo_r
        odd = (lax.broadcasted_iota(jnp.int32, (n, fw), 0) & 1) == 1
        a_part = _dot(w_ref[lo_r:hi_r, 0:half], r_ref[0:half, :]) + jnp.where(odd, -x_mid, x_mid)
        b_part = _dot(w_ref[lo_r:hi_r, half:seq], r_ref[half:seq, :])
        keep = min(hi_r, half) - lo_r
        o_ref[lo_r:lo_r + keep, :] = (a_part + b_part)[0:keep, :].astype(BF16)
        d_ref[lo_r:hi_r, :] = (a_part - b_part).astype(BF16)
    for a in range(half // _REV_BLOCK):
        beta = half // _REV_BLOCK - 1 - a
        win = d_ref[beta * _REV_BLOCK:(beta + 2) * _REV_BLOCK, :]
        o_ref[half + a * _REV_BLOCK:half + (a + 1) * _REV_BLOCK, :] = _dot(jw, win).astype(BF16)


def _fourier(f, cs_c, jw, ca, sa, cb, sb, batch, seq):
    t = batch * seq
    const = lambda b: (0, 0)
    return pl.pallas_call(
        _fourier_kernel,
        grid=(batch,),
        in_specs=[
            pl.BlockSpec((seq, FOURIER_WIDTH), lambda b: (b, 0)),
            pl.BlockSpec(cs_c.shape, const),
            pl.BlockSpec(jw.shape, const),
            pl.BlockSpec(ca.shape, const),
            pl.BlockSpec(sa.shape, const),
            pl.BlockSpec(cb.shape, const),
            pl.BlockSpec(sb.shape, const),
        ],
        out_specs=pl.BlockSpec((seq, FOURIER_WIDTH), lambda b: (b, 0)),
        out_shape=jax.ShapeDtypeStruct((t, FOURIER_WIDTH), BF16),
        scratch_shapes=[
            pltpu.VMEM(((seq // 2 // _DFT_SPLIT + 1) * _DFT_SPLIT, seq), BF16),
            pltpu.VMEM((seq, 2 * FOURIER_WIDTH), BF16),
            pltpu.VMEM((seq, FOURIER_WIDTH), BF16),
            pltpu.VMEM((seq // 2 + _REV_BLOCK, FOURIER_WIDTH), BF16)],
        compiler_params=_params(1),
        name="fourier",
    )(f, cs_c, jw, ca, sa, cb, sb)


def _merge_kernel(o_ref, yf_ref, x_ref, mod_ref, gpre1_ref, wg_ref, bg_ref,
                  wa32_ref, wf32_ref, wo32_ref, gpost_ref, gpre2_ref,
                  x1_ref, h2_ref, wa_ref, wf_ref, wo_ref):
    _cast_once(wa32_ref, wa_ref)
    _cast_once(wf32_ref, wf_ref)
    _cast_once(wo32_ref, wo_ref)
    shift1 = mod_ref[0:1, :]
    scale1 = mod_ref[1:2, :]
    gate1 = mod_ref[2:3, :]
    shift2 = mod_ref[3:4, :]
    scale2 = mod_ref[4:5, :]
    gmod1 = gpre1_ref[...] * (1.0 + scale1)
    gpost = gate1 * gpost_ref[...]
    gmod2 = gpre2_ref[...] * (1.0 + scale2)
    for r in range(x_ref.shape[0] // _ROW_CHUNK):
        rows = slice(r * _ROW_CHUNK, (r + 1) * _ROW_CHUNK)
        x = x_ref[rows, :]
        hb = ((x * _rms_scale(x)) * gmod1 + shift1).astype(BF16)
        ta = _dot(hb, wg_ref[:, 0:D_MODEL]) + bg_ref[:, 0:D_MODEL]
        ga = 1.0 / (1.0 + jnp.exp(-ta))
        tf = _dot(hb, wg_ref[:, D_MODEL:2 * D_MODEL]) + bg_ref[:, D_MODEL:2 * D_MODEL]
        gf = 1.0 / (1.0 + jnp.exp(-tf))
        ya = _dot(o_ref[rows, :], wa_ref[...])
        yf = _dot(yf_ref[rows, :], wf_ref[...])
        y = (ga * ya + gf * yf).astype(BF16)
        y2 = _dot(y, wo_ref[...])
        x1 = x + (y2 * _rms_scale(y2)) * gpost
        x1_ref[rows, :] = x1
        h2_ref[rows, :] = ((x1 * _rms_scale(x1)) * gmod2 + shift2).astype(BF16)


def _merge(o, yf, x2, mod3, g_pre1, w_gate, b_gate, wa, wf, wo, g_post, g_pre2, seq, tm):
    t = x2.shape[0]
    per_b = seq // tm
    row = lambda i: (i, 0)
    const = lambda i: (0, 0)
    return pl.pallas_call(
        _merge_kernel,
        grid=(t // tm,),
        in_specs=[
            pl.BlockSpec((tm, ATTN_WIDTH), row),
            pl.BlockSpec((tm, FOURIER_WIDTH), row),
            pl.BlockSpec((tm, D_MODEL), row),
            pl.BlockSpec((None, 6, D_MODEL), lambda i: (i // per_b, 0, 0)),
            pl.BlockSpec((1, D_MODEL), const),
            pl.BlockSpec((D_MODEL, 2 * D_MODEL), const),
            pl.BlockSpec((1, 2 * D_MODEL), const),
            pl.BlockSpec((ATTN_WIDTH, D_MODEL), const),
            pl.BlockSpec((FOURIER_WIDTH, D_MODEL), const),
            pl.BlockSpec((D_MODEL, D_MODEL), const),
            pl.BlockSpec((1, D_MODEL), const),
            pl.BlockSpec((1, D_MODEL), const),
        ],
        out_specs=[pl.BlockSpec((tm, D_MODEL), row), pl.BlockSpec((tm, D_MODEL), row)],
        out_shape=[jax.ShapeDtypeStruct((t, D_MODEL), F32),
                   jax.ShapeDtypeStruct((t, D_MODEL), BF16)],
        scratch_shapes=[pltpu.VMEM((ATTN_WIDTH, D_MODEL), BF16),
                        pltpu.VMEM((FOURIER_WIDTH, D_MODEL), BF16),
                        pltpu.VMEM((D_MODEL, D_MODEL), BF16)],
        compiler_params=_params(1),
        name="merge",
    )(o, yf, x2, mod3, g_pre1, w_gate, b_gate, wa, wf, wo, g_post, g_pre2)


_HALO_BLOCK = BF16_SUBLANES
_HALO = 8
_MXU_COLS = 256
_GELU_K1 = math.sqrt(2.0 / math.pi)
_GELU_K2 = _GELU_K1 * 0.044715


def _ffn_up_proj(h_ref, hp_ref, hn_ref, w_ref, hx_ref, hperm_ref, tiles_per_seq):
    tm = h_ref.shape[0]
    ext = tm + 2 * _HALO
    pitch = ext // 8
    i = pl.program_id(0)
    keep_prev = jnp.where(i % tiles_per_seq != 0, 1.0, 0.0)
    keep_next = jnp.where(i % tiles_per_seq != tiles_per_seq - 1, 1.0, 0.0)

    for c in range(D_MODEL // LANES):
        lc = slice(c * LANES, (c + 1) * LANES)
        hx_ref[c, 0:_HALO, :] = (
            hp_ref[:, lc].astype(F32)[_HALO_BLOCK - _HALO:_HALO_BLOCK, :] * keep_prev)
        hx_ref[c, _HALO:_HALO + tm, :] = h_ref[:, lc].astype(F32)
        hx_ref[c, _HALO + tm:ext, :] = hn_ref[:, lc].astype(F32)[0:_HALO, :] * keep_next
    for c in range(D_MODEL // LANES):
        lc = slice(c * LANES, (c + 1) * LANES)
        for b in range(0, pitch, 2):
            pair = jnp.concatenate([hx_ref[c, pl.ds(b, 8, stride=pitch), :],
                                    hx_ref[c, pl.ds(b + 1, 8, stride=pitch), :]], axis=0)
            hperm_ref[8 * b:8 * b + 16, lc] = pair.astype(BF16)
    hperm = hperm_ref[...]

    def pair_thunk(jj):
        def run(u_ref):
            for half in range(2):
                cols = slice(half * D_FF + jj * _MXU_COLS, half * D_FF + (jj + 1) * _MXU_COLS)
                u_ref[:, cols] = _dot(hperm, w_ref[:, cols])
        return run

    return [pair_thunk(jj) for jj in range(D_FF // _MXU_COLS)]


def _ffn_up_conv(u_ref, cw_ref, cb_ref, o_ref, st_ref, jj):
    tm = o_ref.shape[0]
    ext = tm + 2 * _HALO
    pitch = ext // 8
    for k in range(_MXU_COLS // LANES):
        j = 2 * jj + k
        la = slice(j * LANES, (j + 1) * LANES)
        lb = slice(D_FF + j * LANES, D_FF + (j + 1) * LANES)
        bc = lambda ref, r, l: jnp.broadcast_to(ref[r:r + 1, l], (8, LANES))
        wa = [bc(cw_ref, r, la) for r in range(CONV_WIDTH)]
        wb = [bc(cw_ref, r, lb) for r in range(CONV_WIDTH)]
        ba = bc(cb_ref, 0, la)
        bb = bc(cb_ref, 0, lb)

        def vreg(lanes, b):
            if b < 0:
                return pltpu.roll(u_ref[8 * (pitch - 1):8 * pitch, lanes], 1, 0)
            if b >= pitch:
                return pltpu.roll(u_ref[0:8, lanes], 7, 0)
            return u_ref[8 * b:8 * b + 8, lanes]

        for b in range(pitch):
            ca = vreg(la, b - 1) * wa[0] + vreg(la, b) * wa[1] + vreg(la, b + 1) * wa[2] + ba
            cv = vreg(lb, b - 1) * wb[0] + vreg(lb, b) * wb[1] + vreg(lb, b + 1) * wb[2] + bb
            t = jnp.tanh(ca * (_GELU_K1 + _GELU_K2 * (ca * ca)))
            st_ref[j % 2, pl.ds(b, 8, stride=pitch), :] = (ca * cv) * (1.0 + t)
        o_ref[:, la] = st_ref[j % 2, _HALO:_HALO + tm, :].astype(BF16)


def _ffn_up_kernel(h_ref, hp_ref, hn_ref, w_ref, cw_ref, cb_ref, o_ref,
                   hx_ref, hperm_ref, st_ref, u0_ref, u1_ref, *, tiles_per_seq, n_tiles):
    i = pl.program_id(0)
    u_refs = (u0_ref, u1_ref)
    n_pairs = D_FF // _MXU_COLS

    @pl.when(i == 0)
    def _():
        for thunk in _ffn_up_proj(h_ref, hp_ref, hn_ref, w_ref, hx_ref, hperm_ref, tiles_per_seq):
            thunk(u0_ref)

    for par in range(2):
        @pl.when((i > 0) & (i < n_tiles) & (i % 2 == par))
        def _(par=par):
            thunks = _ffn_up_proj(h_ref, hp_ref, hn_ref, w_ref, hx_ref, hperm_ref, tiles_per_seq)
            for jj in range(n_pairs):
                thunks[jj](u_refs[par])
                _ffn_up_conv(u_refs[1 - par], cw_ref, cb_ref, o_ref, st_ref, jj)

    @pl.when(i == n_tiles)
    def _():
        for jj in range(n_pairs):
            _ffn_up_conv(u_refs[(n_tiles - 1) % 2], cw_ref, cb_ref, o_ref, st_ref, jj)


def _ffn_up(h2, w_up, conv_w, conv_b, seq, tm):
    t = h2.shape[0]
    n_tiles = t // tm
    tiles_per_seq = seq // tm
    halo_per_tile = tm // _HALO_BLOCK
    n_halo = t // _HALO_BLOCK
    ext = tm + 2 * _HALO
    const = lambda i: (0, 0)
    cur = lambda i: jnp.minimum(i, n_tiles - 1)
    return pl.pallas_call(
        functools.partial(_ffn_up_kernel, tiles_per_seq=tiles_per_seq, n_tiles=n_tiles),
        grid=(n_tiles + 1,),
        in_specs=[
            pl.BlockSpec((tm, D_MODEL), lambda i: (cur(i), 0)),
            pl.BlockSpec((_HALO_BLOCK, D_MODEL),
                         lambda i: (jnp.maximum(cur(i) * halo_per_tile - 1, 0), 0)),
            pl.BlockSpec((_HALO_BLOCK, D_MODEL),
                         lambda i: (jnp.minimum((cur(i) + 1) * halo_per_tile, n_halo - 1), 0)),
            pl.BlockSpec((D_MODEL, 2 * D_FF), const),
            pl.BlockSpec((CONV_WIDTH, 2 * D_FF), const),
            pl.BlockSpec((1, 2 * D_FF), const),
        ],
        out_specs=pl.BlockSpec((tm, D_FF), lambda i: (jnp.maximum(i - 1, 0), 0)),
        out_shape=jax.ShapeDtypeStruct((t, D_FF), BF16),
        scratch_shapes=[pltpu.VMEM((D_MODEL // LANES, ext, LANES), F32),
                        pltpu.VMEM((ext, D_MODEL), BF16),
                        pltpu.VMEM((2, ext, LANES), F32),
                        pltpu.VMEM((ext, 2 * D_FF), F32),
                        pltpu.VMEM((ext, 2 * D_FF), F32)],
        compiler_params=_params(1),
        name="ffn_up",
    )(h2, h2, h2, w_up, conv_w, conv_b)


def _ffn_down_kernel(a_ref, w32_ref, x1_ref, mod_ref, g_ref, o_ref, w_ref):
    _cast_once(w32_ref, w_ref)
    gpost = mod_ref[5:6, :] * g_ref[...]
    for r in range(a_ref.shape[0] // _ROW_CHUNK):
        rows = slice(r * _ROW_CHUNK, (r + 1) * _ROW_CHUNK)
        y = _dot(a_ref[rows, :], w_ref[...])
        o_ref[rows, :] = x1_ref[rows, :] + (y * _rms_scale(y)) * gpost


def _ffn_down(act, w_down, x1, mod3, g_post, seq, tm):
    t = x1.shape[0]
    per_b = seq // tm
    row = lambda i: (i, 0)
    const = lambda i: (0, 0)
    return pl.pallas_call(
        _ffn_down_kernel,
        grid=(t // tm,),
        in_specs=[
            pl.BlockSpec((tm, D_FF), row),
            pl.BlockSpec((D_FF, D_MODEL), const),
            pl.BlockSpec((tm, D_MODEL), row),
            pl.BlockSpec((None, 6, D_MODEL), lambda i: (i // per_b, 0, 0)),
            pl.BlockSpec((1, D_MODEL), const),
        ],
        out_specs=pl.BlockSpec((tm, D_MODEL), row),
        out_shape=jax.ShapeDtypeStruct((t, D_MODEL), F32),
        scratch_shapes=[pltpu.VMEM((D_FF, D_MODEL), BF16)],
        compiler_params=_params(1),
        name="ffn_down",
    )(act, w_down, x1, mod3, g_post)


@functools.lru_cache(maxsize=None)
def _rope_tables(seq):
    pos = np.arange(seq)
    row = (pos // GRID_W).astype(np.float64)
    col = (pos % GRID_W).astype(np.float64)
    inv = ROPE_THETA ** (-np.arange(ROPE_PAIRS_PER_AXIS, dtype=np.float64) / ROPE_PAIRS_PER_AXIS)
    ang = np.concatenate([row[:, None] * inv, col[:, None] * inv], axis=-1)
    cos = np.repeat(np.cos(ang), 2, axis=-1)
    sin = np.repeat(np.sin(ang), 2, axis=-1)
    sign = np.tile(np.array([-1.0, 1.0]), HEAD_DIM // 2)
    reps = LANES // HEAD_DIM
    return (np.tile(cos, (1, reps)).astype(np.float32),
            np.tile(sin * sign, (1, reps)).astype(np.float32))


@functools.lru_cache(maxsize=None)
def _dft_tables(seq):
    gd = FOURIER_GROUP_DIM
    kc = (np.outer(np.arange(gd), np.arange(gd)) % gd).astype(np.float64) * (2.0 * np.pi / gd)
    norm = 1.0 / math.sqrt(seq * gd)
    cs_c = np.concatenate([np.cos(kc), np.sin(kc)], axis=1) * norm
    n1 = seq // _DFT_SPLIT
    t_idx = np.arange(seq // 2)
    ang_a = (np.outer(np.arange(n1), t_idx) % n1).astype(np.float64) * (2.0 * np.pi / n1)
    ang_b = (np.outer(np.arange(_DFT_SPLIT), t_idx) % seq).astype(np.float64) * (2.0 * np.pi / seq)
    cos_b = np.cos(ang_b)
    cos_b[:, 0] *= 0.5
    jw = np.zeros((_REV_BLOCK, 2 * _REV_BLOCK))
    jw[np.arange(_REV_BLOCK), _REV_BLOCK - np.arange(_REV_BLOCK)] = 1.0
    f32 = lambda a: a.astype(np.float32)
    return (f32(cs_c), f32(jw), f32(np.cos(ang_a)), f32(np.sin(ang_a)), f32(cos_b),
            f32(np.sin(ang_b)))


def kernel(x, c, w_ada, b_ada, mix_pre_g, w_in, q_norm_g, k_norm_g, b_gate, w_attn_branch,
           w_fourier_branch, w_out, mix_post_g, ffn_pre_g, w_up, conv_w, conv_b, w_down,
           ffn_post_g):
    batch, seq, _ = x.shape
    depth = w_ada.shape[0]
    t = batch * seq
    tiles = _TILES

    cos_np, sin_np = _rope_tables(seq)
    cos_t = jnp.asarray(cos_np)
    sin_t = jnp.asarray(sin_np)
    dft_tabs = [jnp.asarray(a) for a in _dft_tables(seq)]
    seg = np.arange(ATTN_WIDTH) // HEAD_DIM
    ones_blk = jnp.asarray((seg[:, None] == seg[None, :]).astype(np.float32), dtype=BF16)

    x2 = x.reshape(t, D_MODEL)
    for l in range(depth):
        mod3 = _adaln(c, w_ada[l], b_ada[l]).reshape(batch, 6, D_MODEL)
        gq = (jnp.tile(q_norm_g[l], N_Q_HEADS) * (HEAD_DIM ** -0.5 * _LOG2_E)).reshape(1, ATTN_WIDTH)
        gk = jnp.tile(k_norm_g[l], N_KV_HEADS).reshape(1, KV_WIDTH)
        g_pre1 = mix_pre_g[l].reshape(1, D_MODEL)
        qt, kp, vt, f = _in_proj(
            x2, mod3, g_pre1, w_in[l], gq, gk, cos_t, sin_t, ones_blk, batch, seq,
            tiles["in_proj"])
        o = _attention(qt, kp, vt, batch, seq, tiles["attn"])
        yf = _fourier(f, *dft_tabs, batch, seq)
        x1, h2 = _merge(
            o, yf, x2, mod3, g_pre1, w_in[l, :, _O_GA:].astype(BF16),
            b_gate[l].reshape(1, 2 * D_MODEL),
            w_attn_branch[l], w_fourier_branch[l], w_out[l],
            mix_post_g[l].reshape(1, D_MODEL), ffn_pre_g[l].reshape(1, D_MODEL), seq,
            tiles["merge"])
        glu_half = jnp.concatenate([jnp.ones((D_FF,), F32), jnp.full((D_FF,), 0.5, F32)])
        act = _ffn_up(h2, w_up[l].astype(BF16), conv_w[l] * glu_half,
                      (conv_b[l] * glu_half).reshape(1, 2 * D_FF), seq, tiles["ffn_up"])
        x2 = _ffn_down(act, w_down[l], x1, mod3,
                       ffn_post_g[l].reshape(1, D_MODEL), seq, tiles["ffn_down"])
    return x2.reshape(batch, seq, D_MODEL)
```

```python
import functools
import math

import numpy as np
import jax
import jax.numpy as jnp
from jax import lax
from jax.experimental import pallas as pl
from jax.experimental.pallas import tpu as pltpu

D_MODEL = 1024
GRID_W = 64
HEAD_DIM = 64
N_Q_HEADS = 8
N_KV_HEADS = 2
Q_PER_KV = N_Q_HEADS // N_KV_HEADS
ATTN_WIDTH = N_Q_HEADS * HEAD_DIM
KV_WIDTH = N_KV_HEADS * HEAD_DIM
N_FOURIER_GROUPS = 4
FOURIER_GROUP_DIM = 128
FOURIER_WIDTH = N_FOURIER_GROUPS * FOURIER_GROUP_DIM
IN_WIDTH = ATTN_WIDTH + 2 * KV_WIDTH + FOURIER_WIDTH + 2 * D_MODEL
ROPE_THETA = 10000.0
ROPE_PAIRS_PER_AXIS = HEAD_DIM // 4
D_FF = 2816
CONV_WIDTH = 3
NORM_EPS = 1e-6

LANES = 128
BF16_SUBLANES = 16
VMEM_LIMIT_BYTES = 56 * 1024 * 1024
_ROW_CHUNK = 256
_TILES = {"in_proj": 1024, "attn": 256, "merge": 1024, "ffn_up": 512, "ffn_down": 1024}

_LOG2_E = math.log2(math.e)

F32 = jnp.float32
BF16 = jnp.bfloat16

_O_K = ATTN_WIDTH
_O_V = _O_K + KV_WIDTH
_O_F = _O_V + KV_WIDTH
_O_GA = _O_F + FOURIER_WIDTH
_O_GF = _O_GA + D_MODEL


def _params(n_grid_dims, flags=None):
    return pltpu.CompilerParams(
        dimension_semantics=("arbitrary",) * n_grid_dims,
        vmem_limit_bytes=VMEM_LIMIT_BYTES,
        flags=flags,
    )


def _dot(a, b):
    return jnp.dot(a, b, preferred_element_type=F32)


def _rms_scale(v):
    return lax.rsqrt(jnp.mean(v * v, axis=-1, keepdims=True) + NORM_EPS)


def _adaln_kernel(c_ref, w_ref, b_ref, o_ref):
    c = c_ref[...]
    a = c / (1.0 + jnp.exp(-c))
    a_hi = a.astype(BF16)
    a_lo = (a - a_hi.astype(F32)).astype(BF16)
    n = a.shape[0]
    res = _dot(jnp.concatenate([a_hi, a_lo], axis=0), w_ref[...].astype(BF16))
    o_ref[...] = res[0:n, :] + res[n:2 * n, :] + b_ref[...]


def _adaln(c, w_ada, b_ada):
    batch = c.shape[0]
    n = w_ada.shape[1]
    tn = 768
    return pl.pallas_call(
        _adaln_kernel,
        grid=(n // tn,),
        in_specs=[
            pl.BlockSpec((batch, D_MODEL), lambda j: (0, 0)),
            pl.BlockSpec((D_MODEL, tn), lambda j: (0, j)),
            pl.BlockSpec((1, tn), lambda j: (0, j)),
        ],
        out_specs=pl.BlockSpec((batch, tn), lambda j: (0, j)),
        out_shape=jax.ShapeDtypeStruct((batch, n), F32),
        compiler_params=_params(1),
        name="adaln",
    )(c, w_ada, b_ada.reshape(1, n))


def _swap_pairs(v):
    n = v.shape[-1]
    lane = lax.broadcasted_iota(jnp.int32, v.shape, v.ndim - 1)
    from_right = pltpu.roll(v, n - 1, v.ndim - 1)
    from_left = pltpu.roll(v, 1, v.ndim - 1)
    return jnp.where((lane & 1) == 0, from_right, from_left)


def _head_norm_rope(z, gain, ones_blk, cos, sin):
    reps = z.shape[-1] // LANES
    ssq = _dot((z * z).astype(BF16), ones_blk)
    r = lax.rsqrt(ssq * (1.0 / HEAD_DIM) + NORM_EPS)
    zg = z * gain
    cos_t = jnp.concatenate([cos] * reps, axis=-1) if reps > 1 else cos
    sin_t = jnp.concatenate([sin] * reps, axis=-1) if reps > 1 else sin
    return (zg * cos_t + _swap_pairs(zg) * sin_t) * r


def _cast_once(src_ref, dst_ref):
    @pl.when(pl.program_id(0) == 0)
    def _():
        dst_ref[...] = src_ref[...].astype(BF16)


def _in_proj_kernel(x_ref, mod_ref, g_ref, w32_ref, gq_ref, gk_ref,
                    cos_ref, sin_ref, ones_ref,
                    qt_ref, kp_ref, vt_ref, f_ref, w_ref):
    _cast_once(w32_ref, w_ref)
    shift = mod_ref[0:1, :]
    scale = mod_ref[1:2, :]
    gmod = g_ref[...] * (1.0 + scale)
    for r in range(x_ref.shape[0] // _ROW_CHUNK):
        rows = slice(r * _ROW_CHUNK, (r + 1) * _ROW_CHUNK)
        x = x_ref[rows, :]
        hb = ((x * _rms_scale(x)) * gmod + shift).astype(BF16)
        cos = cos_ref[rows, :]
        sin = sin_ref[rows, :]

        zq = _dot(hb, w_ref[:, 0:_O_K])
        q = _head_norm_rope(zq, gq_ref[...], ones_ref[...], cos, sin)
        qt_ref[:, rows] = q.T.astype(BF16)

        zkv = _dot(hb, w_ref[:, _O_K:_O_F])
        k = _head_norm_rope(zkv[:, 0:KV_WIDTH], gk_ref[...],
                            ones_ref[0:KV_WIDTH, 0:KV_WIDTH], cos, sin)
        k_sw = pltpu.roll(k, HEAD_DIM, 1)
        first = lax.broadcasted_iota(jnp.int32, k.shape, 1) < HEAD_DIM
        kp_ref[rows, :] = jnp.concatenate(
            [jnp.where(first, k, 0.0), jnp.where(first, 0.0, k_sw),
             jnp.where(first, k_sw, 0.0), jnp.where(first, 0.0, k)], axis=-1).astype(BF16)
        zvt = zkv[:, KV_WIDTH:2 * KV_WIDTH].T
        ones_rows = jnp.ones((HEAD_DIM, zvt.shape[1]), F32)
        vt_ref[:, rows] = jnp.concatenate(
            [piece for kv in range(N_KV_HEADS)
             for piece in (zvt[kv * HEAD_DIM:(kv + 1) * HEAD_DIM, :], ones_rows)],
            axis=0).astype(BF16)

        f_ref[rows, :] = _dot(hb, w_ref[:, _O_F:_O_GA]).astype(BF16)


def _in_proj(x2, mod3, g_pre, w_in, gq, gk, cos_t, sin_t, ones_blk, batch, seq, tm):
    t = batch * seq
    per_b = seq // tm
    row = lambda i: (i, 0)
    const = lambda i: (0, 0)
    return pl.pallas_call(
        _in_proj_kernel,
        grid=(t // tm,),
        in_specs=[
            pl.BlockSpec((tm, D_MODEL), row),
            pl.BlockSpec((None, 6, D_MODEL), lambda i: (i // per_b, 0, 0)),
            pl.BlockSpec((1, D_MODEL), const),
            pl.BlockSpec((D_MODEL, _O_GA), const),
            pl.BlockSpec((1, ATTN_WIDTH), const),
            pl.BlockSpec((1, KV_WIDTH), const),
            pl.BlockSpec((tm, LANES), lambda i: (i % per_b, 0)),
            pl.BlockSpec((tm, LANES), lambda i: (i % per_b, 0)),
            pl.BlockSpec((ATTN_WIDTH, ATTN_WIDTH), const),
        ],
        out_specs=[
            pl.BlockSpec((None, ATTN_WIDTH, tm), lambda i: (i // per_b, 0, i % per_b)),
            pl.BlockSpec((tm, 2 * N_KV_HEADS * LANES), row),
            pl.BlockSpec((None, N_KV_HEADS * LANES, tm), lambda i: (i // per_b, 0, i % per_b)),
            pl.BlockSpec((tm, FOURIER_WIDTH), row),
        ],
        out_shape=[
            jax.ShapeDtypeStruct((batch, ATTN_WIDTH, seq), BF16),
            jax.ShapeDtypeStruct((t, 2 * N_KV_HEADS * LANES), BF16),
            jax.ShapeDtypeStruct((batch, N_KV_HEADS * LANES, seq), BF16),
            jax.ShapeDtypeStruct((t, FOURIER_WIDTH), BF16),
        ],
        scratch_shapes=[pltpu.VMEM((D_MODEL, _O_GA), BF16)],
        compiler_params=_params(1),
        name="in_proj",
    )(x2, mod3, g_pre, w_in, gq, gk, cos_t, sin_t, ones_blk)


_ATTN_SLOTS = 2


def _attn_step(qt_ref, kp_ref, vt_ref, o_ref, p_ref, s_new_ref, s_old_ref):
    outs = []
    for h in range(N_Q_HEADS):
        kv = h // Q_PER_KV
        slot = h % _ATTN_SLOTS
        pair = h // 2
        variant = 2 * kv + h % 2
        s_new_ref[h] = _dot(kp_ref[:, variant * LANES:(variant + 1) * LANES],
                            qt_ref[pair * LANES:(pair + 1) * LANES, :])
        s = s_old_ref[h]
        m = jnp.max(s, axis=0, keepdims=True)
        p_ref[slot] = jnp.exp2(s - m).astype(BF16)
        ov = _dot(vt_ref[kv * LANES:(kv + 1) * LANES, :], p_ref[slot])
        on = ov[0:HEAD_DIM, :] / ov[HEAD_DIM:2 * HEAD_DIM, :]
        outs.append(on.T)
    o_ref[...] = jnp.concatenate(outs, axis=-1).astype(BF16)


def _attn_kernel(qt_ref, kp_ref, vt_ref, o_ref, p_ref, sa_ref, sb_ref):
    g = pl.program_id(0)

    @pl.when(g == 0)
    def _():
        sb_ref[...] = jnp.zeros(sb_ref.shape, F32)

    @pl.when(g % 2 == 0)
    def _():
        _attn_step(qt_ref, kp_ref, vt_ref, o_ref, p_ref, sa_ref, sb_ref)

    @pl.when(g % 2 == 1)
    def _():
        _attn_step(qt_ref, kp_ref, vt_ref, o_ref, p_ref, sb_ref, sa_ref)


def _attention(qt, kp, vt, batch, seq, tq):
    t = batch * seq
    per_b = seq // tq
    n_blk = t // tq
    cur = lambda g: jnp.minimum(g, n_blk - 1)
    prev = lambda g: jnp.maximum(g - 1, 0)
    return pl.pallas_call(
        _attn_kernel,
        grid=(n_blk + 1,),
        in_specs=[
            pl.BlockSpec((None, ATTN_WIDTH, tq), lambda g: (cur(g) // per_b, 0, cur(g) % per_b)),
            pl.BlockSpec((seq, 2 * N_KV_HEADS * LANES), lambda g: (cur(g) // per_b, 0)),
            pl.BlockSpec((None, N_KV_HEADS * LANES, seq), lambda g: (prev(g) // per_b, 0, 0)),
        ],
        out_specs=pl.BlockSpec((tq, ATTN_WIDTH), lambda g: (prev(g), 0)),
        out_shape=jax.ShapeDtypeStruct((t, ATTN_WIDTH), BF16),
        scratch_shapes=[pltpu.VMEM((_ATTN_SLOTS, seq, tq), BF16),
                        pltpu.VMEM((N_Q_HEADS, seq, tq), F32),
                        pltpu.VMEM((N_Q_HEADS, seq, tq), F32)],
        compiler_params=_params(1),
        name="attn",
    )(qt, kp, vt)


_DFT_SPLIT = 32
_REV_BLOCK = 256


def _fourier_kernel(f_ref, cs_ref, jw_ref, ca_ref, sa_ref, cb_ref, sb_ref, o_ref,
                    w_ref, x_ref, r_ref, d_ref):
    seq = f_ref.shape[0]
    half = seq // 2
    gd = FOURIER_GROUP_DIM
    fw = FOURIER_WIDTH

    @pl.when(pl.program_id(0) == 0)
    def _():
        cb = cb_ref[...]
        sb = sb_ref[...]

        def body(s1, carry):
            ca = ca_ref[pl.ds(s1, 1), :]
            sa = sa_ref[pl.ds(s1, 1), :]
            rows = pl.ds(pl.multiple_of(s1 * _DFT_SPLIT, _DFT_SPLIT), _DFT_SPLIT)
            w_ref[rows, 0:half] = (ca * cb - sa * sb).astype(BF16)
            w_ref[rows, half:seq] = (-(sa * cb + ca * sb)).astype(BF16)
            return carry

        lax.fori_loop(0, w_ref.shape[0] // _DFT_SPLIT, body, 0)

    cs = cs_ref[...].astype(BF16)
    for g in range(N_FOURIER_GROUPS):
        xcs = _dot(f_ref[:, g * gd:(g + 1) * gd], cs)
        x_ref[:, g * gd:(g + 1) * gd] = xcs[:, 0:gd].astype(BF16)
        x_ref[:, fw + g * gd:fw + (g + 1) * gd] = xcs[:, gd:2 * gd].astype(BF16)

    n_blk = seq // _REV_BLOCK
    jw = jw_ref[...].astype(BF16)
    for a in range(half // _REV_BLOCK):
        beta = n_blk - 1 - a
        nxt = (beta + 1) % n_blk
        win = jnp.concatenate([x_ref[beta * _REV_BLOCK:(beta + 1) * _REV_BLOCK, :],
                               x_ref[nxt * _REV_BLOCK:(nxt + 1) * _REV_BLOCK, :]], axis=0)
        rev = _dot(jw, win)
        top = x_ref[a * _REV_BLOCK:(a + 1) * _REV_BLOCK, :].astype(F32)
        rows = slice(a * _REV_BLOCK, (a + 1) * _REV_BLOCK)
        r_ref[rows, :] = (top[:, 0:fw] + rev[:, 0:fw]).astype(BF16)
        r_ref[half + a * _REV_BLOCK:half + (a + 1) * _REV_BLOCK, :] = (
            top[:, fw:2 * fw] - rev[:, fw:2 * fw]).astype(BF16)

    m_rows = d_ref.shape[0] - _REV_BLOCK + BF16_SUBLANES
    x_mid = x_ref[half:half + 1, 0:fw].astype(F32)
    d_ref[m_rows:d_ref.shape[0], :] = jnp.zeros((d_ref.shape[0] - m_rows, fw), BF16)
    split = half // 2
    for lo_r, hi_r in ((0, split), (split, m_rows)):
        n = hi_r - lo_r
        odd = (lax.broadcasted_iota(jnp.int32, (n, fw), 0) & 1) == 1
        a_part = _dot(w_ref[lo_r:hi_r, 0:half], r_ref[0:half, :]) + jnp.where(odd, -x_mid, x_mid)
        b_part = _dot(w_ref[lo_r:hi_r, half:seq], r_ref[half:seq, :])
        keep = min(hi_r, half) - lo_r
        o_ref[lo_r:lo_r + keep, :] = (a_part + b_part)[0:keep, :].astype(BF16)
        d_ref[lo_r:hi_r, :] = (a_part - b_part).astype(BF16)
    for a in range(half // _REV_BLOCK):
        beta = half // _REV_BLOCK - 1 - a
        win = d_ref[beta * _REV_BLOCK:(beta + 2) * _REV_BLOCK, :]
        o_ref[half + a * _REV_BLOCK:half + (a + 1) * _REV_BLOCK, :] = _dot(jw, win).astype(BF16)


def _fourier(f, cs_c, jw, ca, sa, cb, sb, batch, seq):
    t = batch * seq
    const = lambda b: (0, 0)
    return pl.pallas_call(
        _fourier_kernel,
        grid=(batch,),
        in_specs=[
            pl.BlockSpec((seq, FOURIER_WIDTH), lambda b: (b, 0)),
            pl.BlockSpec(cs_c.shape, const),
            pl.BlockSpec(jw.shape, const),
            pl.BlockSpec(ca.shape, const),
            pl.BlockSpec(sa.shape, const),
            pl.BlockSpec(cb.shape, const),
            pl.BlockSpec(sb.shape, const),
        ],
        out_specs=pl.BlockSpec((seq, FOURIER_WIDTH), lambda b: (b, 0)),
        out_shape=jax.ShapeDtypeStruct((t, FOURIER_WIDTH), BF16),
        scratch_shapes=[
            pltpu.VMEM(((seq // 2 // _DFT_SPLIT + 1) * _DFT_SPLIT, seq), BF16),
            pltpu.VMEM((seq, 2 * FOURIER_WIDTH), BF16),
            pltpu.VMEM((seq, FOURIER_WIDTH), BF16),
            pltpu.VMEM((seq // 2 + _REV_BLOCK, FOURIER_WIDTH), BF16)],
        compiler_params=_params(1),
        name="fourier",
    )(f, cs_c, jw, ca, sa, cb, sb)


def _merge_kernel(o_ref, yf_ref, x_ref, mod_ref, gpre1_ref, wg_ref, bg_ref,
                  wa32_ref, wf32_ref, wo32_ref, gpost_ref, gpre2_ref,
                  x1_ref, h2_ref, wa_ref, wf_ref, wo_ref):
    _cast_once(wa32_ref, wa_ref)
    _cast_once(wf32_ref, wf_ref)
    _cast_once(wo32_ref, wo_ref)
    shift1 = mod_ref[0:1, :]
    scale1 = mod_ref[1:2, :]
    gate1 = mod_ref[2:3, :]
    shift2 = mod_ref[3:4, :]
    scale2 = mod_ref[4:5, :]
    gmod1 = gpre1_ref[...] * (1.0 + scale1)
    gpost = gate1 * gpost_ref[...]
    gmod2 = gpre2_ref[...] * (1.0 + scale2)
    for r in range(x_ref.shape[0] // _ROW_CHUNK):
        rows = slice(r * _ROW_CHUNK, (r + 1) * _ROW_CHUNK)
        x = x_ref[rows, :]
        hb = ((x * _rms_scale(x)) * gmod1 + shift1).astype(BF16)
        ta = _dot(hb, wg_ref[:, 0:D_MODEL]) + bg_ref[:, 0:D_MODEL]
        ga = 1.0 / (1.0 + jnp.exp(-ta))
        tf = _dot(hb, wg_ref[:, D_MODEL:2 * D_MODEL]) + bg_ref[:, D_MODEL:2 * D_MODEL]
        gf = 1.0 / (1.0 + jnp.exp(-tf))
        ya = _dot(o_ref[rows, :], wa_ref[...])
        yf = _dot(yf_ref[rows, :], wf_ref[...])
        y = (ga * ya + gf * yf).astype(BF16)
        y2 = _dot(y, wo_ref[...])
        x1 = x + (y2 * _rms_scale(y2)) * gpost
        x1_ref[rows, :] = x1
        h2_ref[rows, :] = ((x1 * _rms_scale(x1)) * gmod2 + shift2).astype(BF16)


def _merge(o, yf, x2, mod3, g_pre1, w_gate, b_gate, wa, wf, wo, g_post, g_pre2, seq, tm):
    t = x2.shape[0]
    per_b = seq // tm
    row = lambda i: (i, 0)
    const = lambda i: (0, 0)
    return pl.pallas_call(
        _merge_kernel,
        grid=(t // tm,),
        in_specs=[
            pl.BlockSpec((tm, ATTN_WIDTH), row),
            pl.BlockSpec((tm, FOURIER_WIDTH), row),
            pl.BlockSpec((tm, D_MODEL), row),
            pl.BlockSpec((None, 6, D_MODEL), lambda i: (i // per_b, 0, 0)),
            pl.BlockSpec((1, D_MODEL), const),
            pl.BlockSpec((D_MODEL, 2 * D_MODEL), const),
            pl.BlockSpec((1, 2 * D_MODEL), const),
            pl.BlockSpec((ATTN_WIDTH, D_MODEL), const),
            pl.BlockSpec((FOURIER_WIDTH, D_MODEL), const),
            pl.BlockSpec((D_MODEL, D_MODEL), const),
            pl.BlockSpec((1, D_MODEL), const),
            pl.BlockSpec((1, D_MODEL), const),
        ],
        out_specs=[pl.BlockSpec((tm, D_MODEL), row), pl.BlockSpec((tm, D_MODEL), row)],
        out_shape=[jax.ShapeDtypeStruct((t, D_MODEL), F32),
                   jax.ShapeDtypeStruct((t, D_MODEL), BF16)],
        scratch_shapes=[pltpu.VMEM((ATTN_WIDTH, D_MODEL), BF16),
                        pltpu.VMEM((FOURIER_WIDTH, D_MODEL), BF16),
                        pltpu.VMEM((D_MODEL, D_MODEL), BF16)],
        compiler_params=_params(1),
        name="merge",
    )(o, yf, x2, mod3, g_pre1, w_gate, b_gate, wa, wf, wo, g_post, g_pre2)


_HALO_BLOCK = BF16_SUBLANES
_HALO = 8
_MXU_COLS = 256
_GELU_K1 = math.sqrt(2.0 / math.pi)
_GELU_K2 = _GELU_K1 * 0.044715


def _ffn_up_proj(h_ref, hp_ref, hn_ref, w_ref, hx_ref, hperm_ref, tiles_per_seq):
    tm = h_ref.shape[0]
    ext = tm + 2 * _HALO
    pitch = ext // 8
    i = pl.program_id(0)
    keep_prev = jnp.where(i % tiles_per_seq != 0, 1.0, 0.0)
    keep_next = jnp.where(i % tiles_per_seq != tiles_per_seq - 1, 1.0, 0.0)

    for c in range(D_MODEL // LANES):
        lc = slice(c * LANES, (c + 1) * LANES)
        hx_ref[c, 0:_HALO, :] = (
            hp_ref[:, lc].astype(F32)[_HALO_BLOCK - _HALO:_HALO_BLOCK, :] * keep_prev)
        hx_ref[c, _HALO:_HALO + tm, :] = h_ref[:, lc].astype(F32)
        hx_ref[c, _HALO + tm:ext, :] = hn_ref[:, lc].astype(F32)[0:_HALO, :] * keep_next
    for c in range(D_MODEL // LANES):
        lc = slice(c * LANES, (c + 1) * LANES)
        for b in range(0, pitch, 2):
            pair = jnp.concatenate([hx_ref[c, pl.ds(b, 8, stride=pitch), :],
                                    hx_ref[c, pl.ds(b + 1, 8, stride=pitch), :]], axis=0)
            hperm_ref[8 * b:8 * b + 16, lc] = pair.astype(BF16)
    hperm = hperm_ref[...]

    def pair_thunk(jj):
        def run(u_ref):
            for half in range(2):
                cols = slice(half * D_FF + jj * _MXU_COLS, half * D_FF + (jj + 1) * _MXU_COLS)
                u_ref[:, cols] = _dot(hperm, w_ref[:, cols])
        return run

    return [pair_thunk(jj) for jj in range(D_FF // _MXU_COLS)]


def _ffn_up_conv(u_ref, cw_ref, cb_ref, o_ref, st_ref, jj):
    tm = o_ref.shape[0]
    ext = tm + 2 * _HALO
    pitch = ext // 8
    for k in range(_MXU_COLS // LANES):
        j = 2 * jj + k
        la = slice(j * LANES, (j + 1) * LANES)
        lb = slice(D_FF + j * LANES, D_FF + (j + 1) * LANES)
        bc = lambda ref, r, l: jnp.broadcast_to(ref[r:r + 1, l], (8, LANES))
        wa = [bc(cw_ref, r, la) for r in range(CONV_WIDTH)]
        wb = [bc(cw_ref, r, lb) for r in range(CONV_WIDTH)]
        ba = bc(cb_ref, 0, la)
        bb = bc(cb_ref, 0, lb)

        def vreg(lanes, b):
            if b < 0:
                return pltpu.roll(u_ref[8 * (pitch - 1):8 * pitch, lanes], 1, 0)
            if b >= pitch:
                return pltpu.roll(u_ref[0:8, lanes], 7, 0)
            return u_ref[8 * b:8 * b + 8, lanes]

        for b in range(pitch):
            ca = vreg(la, b - 1) * wa[0] + vreg(la, b) * wa[1] + vreg(la, b + 1) * wa[2] + ba
            cv = vreg(lb, b - 1) * wb[0] + vreg(lb, b) * wb[1] + vreg(lb, b + 1) * wb[2] + bb
            t = jnp.tanh(ca * (_GELU_K1 + _GELU_K2 * (ca * ca)))
            st_ref[j % 2, pl.ds(b, 8, stride=pitch), :] = (ca * cv) * (1.0 + t)
        o_ref[:, la] = st_ref[j % 2, _HALO:_HALO + tm, :].astype(BF16)


def _ffn_up_kernel(h_ref, hp_ref, hn_ref, w_ref, cw_ref, cb_ref, o_ref,
                   hx_ref, hperm_ref, st_ref, u0_ref, u1_ref, *, tiles_per_seq, n_tiles):
    i = pl.program_id(0)
    u_refs = (u0_ref, u1_ref)
    n_pairs = D_FF // _MXU_COLS

    @pl.when(i == 0)
    def _():
        for thunk in _ffn_up_proj(h_ref, hp_ref, hn_ref, w_ref, hx_ref, hperm_ref, tiles_per_seq):
            thunk(u0_ref)

    for par in range(2):
        @pl.when((i > 0) & (i < n_tiles) & (i % 2 == par))
        def _(par=par):
            thunks = _ffn_up_proj(h_ref, hp_ref, hn_ref, w_ref, hx_ref, hperm_ref, tiles_per_seq)
            for jj in range(n_pairs):
                thunks[jj](u_refs[par])
                _ffn_up_conv(u_refs[1 - par], cw_ref, cb_ref, o_ref, st_ref, jj)

    @pl.when(i == n_tiles)
    def _():
        for jj in range(n_pairs):
            _ffn_up_conv(u_refs[(n_tiles - 1) % 2], cw_ref, cb_ref, o_ref, st_ref, jj)


def _ffn_up(h2, w_up, conv_w, conv_b, seq, tm):
    t = h2.shape[0]
    n_tiles = t // tm
    tiles_per_seq = seq // tm
    halo_per_tile = tm // _HALO_BLOCK
    n_halo = t // _HALO_BLOCK
    ext = tm + 2 * _HALO
    pitch = ext // 8
    assert ext % BF16_SUBLANES == 0 and pitch % 2 == 0 and pitch % 8 != 0, (tm, ext)
    const = lambda i: (0, 0)
    cur = lambda i: jnp.minimum(i, n_tiles - 1)
    return pl.pallas_call(
        functools.partial(_ffn_up_kernel, tiles_per_seq=tiles_per_seq, n_tiles=n_tiles),
        grid=(n_tiles + 1,),
        in_specs=[
            pl.BlockSpec((tm, D_MODEL), lambda i: (cur(i), 0)),
            pl.BlockSpec((_HALO_BLOCK, D_MODEL),
                         lambda i: (jnp.maximum(cur(i) * halo_per_tile - 1, 0), 0)),
            pl.BlockSpec((_HALO_BLOCK, D_MODEL),
                         lambda i: (jnp.minimum((cur(i) + 1) * halo_per_tile, n_halo - 1), 0)),
            pl.BlockSpec((D_MODEL, 2 * D_FF), const),
            pl.BlockSpec((CONV_WIDTH, 2 * D_FF), const),
            pl.BlockSpec((1, 2 * D_FF), const),
        ],
        out_specs=pl.BlockSpec((tm, D_FF), lambda i: (jnp.maximum(i - 1, 0), 0)),
        out_shape=jax.ShapeDtypeStruct((t, D_FF), BF16),
        scratch_shapes=[pltpu.VMEM((D_MODEL // LANES, ext, LANES), F32),
                        pltpu.VMEM((ext, D_MODEL), BF16),
                        pltpu.VMEM((2, ext, LANES), F32),
                        pltpu.VMEM((ext, 2 * D_FF), F32),
                        pltpu.VMEM((ext, 2 * D_FF), F32)],
        compiler_params=_params(1),
        name="ffn_up",
    )(h2, h2, h2, w_up, conv_w, conv_b)


def _ffn_down_kernel(a_ref, w32_ref, x1_ref, mod_ref, g_ref, o_ref, w_ref):
    _cast_once(w32_ref, w_ref)
    gpost = mod_ref[5:6, :] * g_ref[...]
    for r in range(a_ref.shape[0] // _ROW_CHUNK):
        rows = slice(r * _ROW_CHUNK, (r + 1) * _ROW_CHUNK)
        y = _dot(a_ref[rows, :], w_ref[...])
        o_ref[rows, :] = x1_ref[rows, :] + (y * _rms_scale(y)) * gpost


def _ffn_down(act, w_down, x1, mod3, g_post, seq, tm):
    t = x1.shape[0]
    per_b = seq // tm
    row = lambda i: (i, 0)
    const = lambda i: (0, 0)
    return pl.pallas_call(
        _ffn_down_kernel,
        grid=(t // tm,),
        in_specs=[
            pl.BlockSpec((tm, D_FF), row),
            pl.BlockSpec((D_FF, D_MODEL), const),
            pl.BlockSpec((tm, D_MODEL), row),
            pl.BlockSpec((None, 6, D_MODEL), lambda i: (i // per_b, 0, 0)),
            pl.BlockSpec((1, D_MODEL), const),
        ],
        out_specs=pl.BlockSpec((tm, D_MODEL), row),
        out_shape=jax.ShapeDtypeStruct((t, D_MODEL), F32),
        scratch_shapes=[pltpu.VMEM((D_FF, D_MODEL), BF16)],
        compiler_params=_params(1),
        name="ffn_down",
    )(act, w_down, x1, mod3, g_post)


@functools.lru_cache(maxsize=None)
def _rope_tables(seq):
    pos = np.arange(seq)
    row = (pos // GRID_W).astype(np.float64)
    col = (pos % GRID_W).astype(np.float64)
    inv = ROPE_THETA ** (-np.arange(ROPE_PAIRS_PER_AXIS, dtype=np.float64) / ROPE_PAIRS_PER_AXIS)
    ang = np.concatenate([row[:, None] * inv, col[:, None] * inv], axis=-1)
    cos = np.repeat(np.cos(ang), 2, axis=-1)
    sin = np.repeat(np.sin(ang), 2, axis=-1)
    sign = np.tile(np.array([-1.0, 1.0]), HEAD_DIM // 2)
    reps = LANES // HEAD_DIM
    return (np.tile(cos, (1, reps)).astype(np.float32),
            np.tile(sin * sign, (1, reps)).astype(np.float32))


@functools.lru_cache(maxsize=None)
def _dft_tables(seq):
    gd = FOURIER_GROUP_DIM
    kc = (np.outer(np.arange(gd), np.arange(gd)) % gd).astype(np.float64) * (2.0 * np.pi / gd)
    norm = 1.0 / math.sqrt(seq * gd)
    cs_c = np.concatenate([np.cos(kc), np.sin(kc)], axis=1) * norm
    n1 = seq // _DFT_SPLIT
    t_idx = np.arange(seq // 2)
    ang_a = (np.outer(np.arange(n1), t_idx) % n1).astype(np.float64) * (2.0 * np.pi / n1)
    ang_b = (np.outer(np.arange(_DFT_SPLIT), t_idx) % seq).astype(np.float64) * (2.0 * np.pi / seq)
    cos_b = np.cos(ang_b)
    cos_b[:, 0] *= 0.5
    jw = np.zeros((_REV_BLOCK, 2 * _REV_BLOCK))
    jw[np.arange(_REV_BLOCK), _REV_BLOCK - np.arange(_REV_BLOCK)] = 1.0
    f32 = lambda a: a.astype(np.float32)
    return (f32(cs_c), f32(jw), f32(np.cos(ang_a)), f32(np.sin(ang_a)), f32(cos_b),
            f32(np.sin(ang_b)))


def kernel(x, c, w_ada, b_ada, mix_pre_g, w_in, q_norm_g, k_norm_g, b_gate, w_attn_branch,
           w_fourier_branch, w_out, mix_post_g, ffn_pre_g, w_up, conv_w, conv_b, w_down,
           ffn_post_g):
    batch, seq, _ = x.shape
    depth = w_ada.shape[0]
    t = batch * seq
    tiles = _TILES

    cos_np, sin_np = _rope_tables(seq)
    cos_t = jnp.asarray(cos_np)
    sin_t = jnp.asarray(sin_np)
    dft_tabs = [jnp.asarray(a) for a in _dft_tables(seq)]
    seg = np.arange(ATTN_WIDTH) // HEAD_DIM
    ones_blk = jnp.asarray((seg[:, None] == seg[None, :]).astype(np.float32), dtype=BF16)

    x2 = x.reshape(t, D_MODEL)
    for l in range(depth):
        mod3 = _adaln(c, w_ada[l], b_ada[l]).reshape(batch, 6, D_MODEL)
        gq = (jnp.tile(q_norm_g[l], N_Q_HEADS) * (HEAD_DIM ** -0.5 * _LOG2_E)).reshape(1, ATTN_WIDTH)
        gk = jnp.tile(k_norm_g[l], N_KV_HEADS).reshape(1, KV_WIDTH)
        g_pre1 = mix_pre_g[l].reshape(1, D_MODEL)
        qt, kp, vt, f = _in_proj(
            x2, mod3, g_pre1, w_in[l], gq, gk, cos_t, sin_t, ones_blk, batch, seq,
            tiles["in_proj"])
        o = _attention(qt, kp, vt, batch, seq, tiles["attn"])
        yf = _fourier(f, *dft_tabs, batch, seq)
        x1, h2 = _merge(
            o, yf, x2, mod3, g_pre1, w_in[l, :, _O_GA:].astype(BF16),
            b_gate[l].reshape(1, 2 * D_MODEL),
            w_attn_branch[l], w_fourier_branch[l], w_out[l],
            mix_post_g[l].reshape(1, D_MODEL), ffn_pre_g[l].reshape(1, D_MODEL), seq,
            tiles["merge"])
        glu_half = jnp.concatenate([jnp.ones((D_FF,), F32), jnp.full((D_FF,), 0.5, F32)])
        act = _ffn_up(h2, w_up[l].astype(BF16), conv_w[l] * glu_half,
                      (conv_b[l] * glu_half).reshape(1, 2 * D_FF), seq, tiles["ffn_up"])
        x2 = _ffn_down(act, w_down[l], x1, mod3,
                       ffn_post_g[l].reshape(1, D_MODEL), seq, tiles["ffn_down"])
    return x2.reshape(batch, seq, D_MODEL)
```

```python
import functools
import math

import numpy as np
import jax
import jax.numpy as jnp
from jax import lax
from jax.experimental import pallas as pl
from jax.experimental.pallas import tpu as pltpu

D_MODEL = 1024
GRID_W = 64
HEAD_DIM = 64
N_Q_HEADS = 8
N_KV_HEADS = 2
Q_PER_KV = N_Q_HEADS // N_KV_HEADS
ATTN_WIDTH = N_Q_HEADS * HEAD_DIM
KV_WIDTH = N_KV_HEADS * HEAD_DIM
N_FOURIER_GROUPS = 4
FOURIER_GROUP_DIM = 128
FOURIER_WIDTH = N_FOURIER_GROUPS * FOURIER_GROUP_DIM
IN_WIDTH = ATTN_WIDTH + 2 * KV_WIDTH + FOURIER_WIDTH + 2 * D_MODEL
ROPE_THETA = 10000.0
ROPE_PAIRS_PER_AXIS = HEAD_DIM // 4
D_FF = 2816
CONV_WIDTH = 3
NORM_EPS = 1e-6

LANES = 128
BF16_SUBLANES = 16
VMEM_LIMIT_BYTES = 56 * 1024 * 1024
_ROW_CHUNK = 256
_V_ONES_ROWS = BF16_SUBLANES
_V_ROWS = HEAD_DIM + _V_ONES_ROWS
_TILES = {"in_proj": 1024, "attn": 256, "merge": 1024, "ffn_up": 512, "ffn_down": 1024}

_LOG2_E = math.log2(math.e)

F32 = jnp.float32
BF16 = jnp.bfloat16

_O_K = ATTN_WIDTH
_O_V = _O_K + KV_WIDTH
_O_F = _O_V + KV_WIDTH
_O_GA = _O_F + FOURIER_WIDTH
_O_GF = _O_GA + D_MODEL


def _params(n_grid_dims, flags=None):
    return pltpu.CompilerParams(
        dimension_semantics=("arbitrary",) * n_grid_dims,
        vmem_limit_bytes=VMEM_LIMIT_BYTES,
        flags=flags,
    )


def _dot(a, b):
    return jnp.dot(a, b, preferred_element_type=F32)


def _rms_scale(v):
    return lax.rsqrt(jnp.mean(v * v, axis=-1, keepdims=True) + NORM_EPS)


def _adaln_kernel(c_ref, w_ref, b_ref, o_ref):
    c = c_ref[...]
    a = c / (1.0 + jnp.exp(-c))
    a_hi = a.astype(BF16)
    a_lo = (a - a_hi.astype(F32)).astype(BF16)
    n = a.shape[0]
    res = _dot(jnp.concatenate([a_hi, a_lo], axis=0), w_ref[...].astype(BF16))
    o_ref[...] = res[0:n, :] + res[n:2 * n, :] + b_ref[...]


def _adaln(c, w_ada, b_ada):
    batch = c.shape[0]
    n = w_ada.shape[1]
    tn = 768
    return pl.pallas_call(
        _adaln_kernel,
        grid=(n // tn,),
        in_specs=[
            pl.BlockSpec((batch, D_MODEL), lambda j: (0, 0)),
            pl.BlockSpec((D_MODEL, tn), lambda j: (0, j)),
            pl.BlockSpec((1, tn), lambda j: (0, j)),
        ],
        out_specs=pl.BlockSpec((batch, tn), lambda j: (0, j)),
        out_shape=jax.ShapeDtypeStruct((batch, n), F32),
        compiler_params=_params(1),
        name="adaln",
    )(c, w_ada, b_ada.reshape(1, n))


def _swap_pairs(v):
    n = v.shape[-1]
    lane = lax.broadcasted_iota(jnp.int32, v.shape, v.ndim - 1)
    from_right = pltpu.roll(v, n - 1, v.ndim - 1)
    from_left = pltpu.roll(v, 1, v.ndim - 1)
    return jnp.where((lane & 1) == 0, from_right, from_left)


def _head_norm_rope(z, gain, ones_blk, cos, sin):
    reps = z.shape[-1] // LANES
    ssq = _dot((z * z).astype(BF16), ones_blk)
    r = lax.rsqrt(ssq * (1.0 / HEAD_DIM) + NORM_EPS)
    zg = z * gain
    cos_t = jnp.concatenate([cos] * reps, axis=-1) if reps > 1 else cos
    sin_t = jnp.concatenate([sin] * reps, axis=-1) if reps > 1 else sin
    return (zg * cos_t + _swap_pairs(zg) * sin_t) * r


def _cast_once(src_ref, dst_ref):
    @pl.when(pl.program_id(0) == 0)
    def _():
        dst_ref[...] = src_ref[...].astype(BF16)


def _in_proj_kernel(x_ref, mod_ref, g_ref, w32_ref, gq_ref, gk_ref,
                    cos_ref, sin_ref, ones_ref,
                    qt_ref, kp_ref, vt_ref, f_ref, w_ref):
    _cast_once(w32_ref, w_ref)
    shift = mod_ref[0:1, :]
    scale = mod_ref[1:2, :]
    gmod = g_ref[...] * (1.0 + scale)
    for r in range(x_ref.shape[0] // _ROW_CHUNK):
        rows = slice(r * _ROW_CHUNK, (r + 1) * _ROW_CHUNK)
        x = x_ref[rows, :]
        hb = ((x * _rms_scale(x)) * gmod + shift).astype(BF16)
        cos = cos_ref[rows, :]
        sin = sin_ref[rows, :]

        zq = _dot(hb, w_ref[:, 0:_O_K])
        q = _head_norm_rope(zq, gq_ref[...], ones_ref[...], cos, sin)
        qt_ref[:, rows] = q.T.astype(BF16)

        zkv = _dot(hb, w_ref[:, _O_K:_O_F])
        k = _head_norm_rope(zkv[:, 0:KV_WIDTH], gk_ref[...],
                            ones_ref[0:KV_WIDTH, 0:KV_WIDTH], cos, sin)
        k_sw = pltpu.roll(k, HEAD_DIM, 1)
        first = lax.broadcasted_iota(jnp.int32, k.shape, 1) < HEAD_DIM
        kp_ref[rows, :] = jnp.concatenate(
            [jnp.where(first, k, 0.0), jnp.where(first, 0.0, k_sw),
             jnp.where(first, k_sw, 0.0), jnp.where(first, 0.0, k)], axis=-1).astype(BF16)
        zvt = zkv[:, KV_WIDTH:2 * KV_WIDTH].T
        ones_rows = jnp.ones((_V_ONES_ROWS, zvt.shape[1]), F32)
        vt_ref[:, rows] = jnp.concatenate(
            [piece for kv in range(N_KV_HEADS)
             for piece in (zvt[kv * HEAD_DIM:(kv + 1) * HEAD_DIM, :], ones_rows)],
            axis=0).astype(BF16)

        f_ref[rows, :] = _dot(hb, w_ref[:, _O_F:_O_GA]).astype(BF16)


def _in_proj(x2, mod3, g_pre, w_in, gq, gk, cos_t, sin_t, ones_blk, batch, seq, tm):
    t = batch * seq
    per_b = seq // tm
    row = lambda i: (i, 0)
    const = lambda i: (0, 0)
    return pl.pallas_call(
        _in_proj_kernel,
        grid=(t // tm,),
        in_specs=[
            pl.BlockSpec((tm, D_MODEL), row),
            pl.BlockSpec((None, 6, D_MODEL), lambda i: (i // per_b, 0, 0)),
            pl.BlockSpec((1, D_MODEL), const),
            pl.BlockSpec((D_MODEL, _O_GA), const),
            pl.BlockSpec((1, ATTN_WIDTH), const),
            pl.BlockSpec((1, KV_WIDTH), const),
            pl.BlockSpec((tm, LANES), lambda i: (i % per_b, 0)),
            pl.BlockSpec((tm, LANES), lambda i: (i % per_b, 0)),
            pl.BlockSpec((ATTN_WIDTH, ATTN_WIDTH), const),
        ],
        out_specs=[
            pl.BlockSpec((None, ATTN_WIDTH, tm), lambda i: (i // per_b, 0, i % per_b)),
            pl.BlockSpec((tm, 2 * N_KV_HEADS * LANES), row),
            pl.BlockSpec((None, N_KV_HEADS * _V_ROWS, tm), lambda i: (i // per_b, 0, i % per_b)),
            pl.BlockSpec((tm, FOURIER_WIDTH), row),
        ],
        out_shape=[
            jax.ShapeDtypeStruct((batch, ATTN_WIDTH, seq), BF16),
            jax.ShapeDtypeStruct((t, 2 * N_KV_HEADS * LANES), BF16),
            jax.ShapeDtypeStruct((batch, N_KV_HEADS * _V_ROWS, seq), BF16),
            jax.ShapeDtypeStruct((t, FOURIER_WIDTH), BF16),
        ],
        scratch_shapes=[pltpu.VMEM((D_MODEL, _O_GA), BF16)],
        compiler_params=_params(1),
        name="in_proj",
    )(x2, mod3, g_pre, w_in, gq, gk, cos_t, sin_t, ones_blk)


_ATTN_SLOTS = 2


def _attn_step(qt_ref, kp_ref, vt_ref, o_ref, p_ref, s_new_ref, s_old_ref):
    outs = []
    for h in range(N_Q_HEADS):
        kv = h // Q_PER_KV
        slot = h % _ATTN_SLOTS
        pair = h // 2
        variant = 2 * kv + h % 2
        s_new_ref[h] = _dot(kp_ref[:, variant * LANES:(variant + 1) * LANES],
                            qt_ref[pair * LANES:(pair + 1) * LANES, :])
        s = s_old_ref[h]
        m = jnp.max(s, axis=0, keepdims=True)
        p_ref[slot] = jnp.exp2(s - m).astype(BF16)
        ov = _dot(vt_ref[kv * _V_ROWS:(kv + 1) * _V_ROWS, :], p_ref[slot])
        on = ov[0:HEAD_DIM, :] / ov[HEAD_DIM:HEAD_DIM + 1, :]
        outs.append(on.T)
    o_ref[...] = jnp.concatenate(outs, axis=-1).astype(BF16)


def _attn_kernel(qt_ref, kp_ref, vt_ref, o_ref, p_ref, sa_ref, sb_ref):
    g = pl.program_id(0)

    @pl.when(g == 0)
    def _():
        sb_ref[...] = jnp.zeros(sb_ref.shape, F32)

    @pl.when(g % 2 == 0)
    def _():
        _attn_step(qt_ref, kp_ref, vt_ref, o_ref, p_ref, sa_ref, sb_ref)

    @pl.when(g % 2 == 1)
    def _():
        _attn_step(qt_ref, kp_ref, vt_ref, o_ref, p_ref, sb_ref, sa_ref)


def _attention(qt, kp, vt, batch, seq, tq):
    t = batch * seq
    per_b = seq // tq
    n_blk = t // tq
    cur = lambda g: jnp.minimum(g, n_blk - 1)
    prev = lambda g: jnp.maximum(g - 1, 0)
    return pl.pallas_call(
        _attn_kernel,
        grid=(n_blk + 1,),
        in_specs=[
            pl.BlockSpec((None, ATTN_WIDTH, tq), lambda g: (cur(g) // per_b, 0, cur(g) % per_b)),
            pl.BlockSpec((seq, 2 * N_KV_HEADS * LANES), lambda g: (cur(g) // per_b, 0)),
            pl.BlockSpec((None, N_KV_HEADS * _V_ROWS, seq), lambda g: (prev(g) // per_b, 0, 0)),
        ],
        out_specs=pl.BlockSpec((tq, ATTN_WIDTH), lambda g: (prev(g), 0)),
        out_shape=jax.ShapeDtypeStruct((t, ATTN_WIDTH), BF16),
        scratch_shapes=[pltpu.VMEM((_ATTN_SLOTS, seq, tq), BF16),
                        pltpu.VMEM((N_Q_HEADS, seq, tq), F32),
                        pltpu.VMEM((N_Q_HEADS, seq, tq), F32)],
        compiler_params=_params(1),
        name="attn",
    )(qt, kp, vt)


_DFT_SPLIT = 32
_REV_BLOCK = 256


def _fourier_kernel(f_ref, cs_ref, jw_ref, ca_ref, sa_ref, cb_ref, sb_ref, o_ref,
                    w_ref, x_ref, r_ref, d_ref):
    seq = f_ref.shape[0]
    half = seq // 2
    gd = FOURIER_GROUP_DIM
    fw = FOURIER_WIDTH

    @pl.when(pl.program_id(0) == 0)
    def _():
        cb = cb_ref[...]
        sb = sb_ref[...]

        def body(s1, carry):
            ca = ca_ref[pl.ds(s1, 1), :]
            sa = sa_ref[pl.ds(s1, 1), :]
            rows = pl.ds(pl.multiple_of(s1 * _DFT_SPLIT, _DFT_SPLIT), _DFT_SPLIT)
            w_ref[rows, 0:half] = (ca * cb - sa * sb).astype(BF16)
            w_ref[rows, half:seq] = (-(sa * cb + ca * sb)).astype(BF16)
            return carry

        lax.fori_loop(0, w_ref.shape[0] // _DFT_SPLIT, body, 0)

    cs = cs_ref[...].astype(BF16)
    for g in range(N_FOURIER_GROUPS):
        xcs = _dot(f_ref[:, g * gd:(g + 1) * gd], cs)
        x_ref[:, g * gd:(g + 1) * gd] = xcs[:, 0:gd].astype(BF16)
        x_ref[:, fw + g * gd:fw + (g + 1) * gd] = xcs[:, gd:2 * gd].astype(BF16)

    n_blk = seq // _REV_BLOCK
    jw = jw_ref[...].astype(BF16)
    for a in range(half // _REV_BLOCK):
        beta = n_blk - 1 - a
        nxt = (beta + 1) % n_blk
        win = jnp.concatenate([x_ref[beta * _REV_BLOCK:(beta + 1) * _REV_BLOCK, :],
                               x_ref[nxt * _REV_BLOCK:(nxt + 1) * _REV_BLOCK, :]], axis=0)
        rev = _dot(jw, win)
        top = x_ref[a * _REV_BLOCK:(a + 1) * _REV_BLOCK, :].astype(F32)
        rows = slice(a * _REV_BLOCK, (a + 1) * _REV_BLOCK)
        r_ref[rows, :] = (top[:, 0:fw] + rev[:, 0:fw]).astype(BF16)
        r_ref[half + a * _REV_BLOCK:half + (a + 1) * _REV_BLOCK, :] = (
            top[:, fw:2 * fw] - rev[:, fw:2 * fw]).astype(BF16)

    m_rows = d_ref.shape[0] - _REV_BLOCK + BF16_SUBLANES
    x_mid = x_ref[half:half + 1, 0:fw].astype(F32)
    d_ref[m_rows:d_ref.shape[0], :] = jnp.zeros((d_ref.shape[0] - m_rows, fw), BF16)
    split = half // 2
    for lo_r, hi_r in ((0, split), (split, m_rows)):
        n = hi_r - lo_r
        odd = (lax.broadcasted_iota(jnp.int32, (n, fw), 0) & 1) == 1
        a_part = _dot(w_ref[lo_r:hi_r, 0:half], r_ref[0:half, :]) + jnp.where(odd, -x_mid, x_mid)
        b_part = _dot(w_ref[lo_r:hi_r, half:seq], r_ref[half:seq, :])
        keep = min(hi_r, half) - lo_r
        o_ref[lo_r:lo_r + keep, :] = (a_part + b_part)[0:keep, :].astype(BF16)
        d_ref[lo_r:hi_r, :] = (a_part - b_part).astype(BF16)
    for a in range(half // _REV_BLOCK):
        beta = half // _REV_BLOCK - 1 - a
        win = d_ref[beta * _REV_BLOCK:(beta + 2) * _REV_BLOCK, :]
        o_ref[half + a * _REV_BLOCK:half + (a + 1) * _REV_BLOCK, :] = _dot(jw, win).astype(BF16)


def _fourier(f, cs_c, jw, ca, sa, cb, sb, batch, seq):
    t = batch * seq
    const = lambda b: (0, 0)
    return pl.pallas_call(
        _fourier_kernel,
        grid=(batch,),
        in_specs=[
            pl.BlockSpec((seq, FOURIER_WIDTH), lambda b: (b, 0)),
            pl.BlockSpec(cs_c.shape, const),
            pl.BlockSpec(jw.shape, const),
            pl.BlockSpec(ca.shape, const),
            pl.BlockSpec(sa.shape, const),
            pl.BlockSpec(cb.shape, const),
            pl.BlockSpec(sb.shape, const),
        ],
        out_specs=pl.BlockSpec((seq, FOURIER_WIDTH), lambda b: (b, 0)),
        out_shape=jax.ShapeDtypeStruct((t, FOURIER_WIDTH), BF16),
        scratch_shapes=[
            pltpu.VMEM(((seq // 2 // _DFT_SPLIT + 1) * _DFT_SPLIT, seq), BF16),
            pltpu.VMEM((seq, 2 * FOURIER_WIDTH), BF16),
            pltpu.VMEM((seq, FOURIER_WIDTH), BF16),
            pltpu.VMEM((seq // 2 + _REV_BLOCK, FOURIER_WIDTH), BF16)],
        compiler_params=_params(1),
        name="fourier",
    )(f, cs_c, jw, ca, sa, cb, sb)


def _merge_kernel(o_ref, yf_ref, x_ref, mod_ref, gpre1_ref, wg_ref, bg_ref,
                  wa32_ref, wf32_ref, wo32_ref, gpost_ref, gpre2_ref,
                  x1_ref, h2_ref, wa_ref, wf_ref, wo_ref):
    _cast_once(wa32_ref, wa_ref)
    _cast_once(wf32_ref, wf_ref)
    _cast_once(wo32_ref, wo_ref)
    shift1 = mod_ref[0:1, :]
    scale1 = mod_ref[1:2, :]
    gate1 = mod_ref[2:3, :]
    shift2 = mod_ref[3:4, :]
    scale2 = mod_ref[4:5, :]
    gmod1 = gpre1_ref[...] * (1.0 + scale1)
    gpost = gate1 * gpost_ref[...]
    gmod2 = gpre2_ref[...] * (1.0 + scale2)
    for r in range(x_ref.shape[0] // _ROW_CHUNK):
        rows = slice(r * _ROW_CHUNK, (r + 1) * _ROW_CHUNK)
        x = x_ref[rows, :]
        hb = ((x * _rms_scale(x)) * gmod1 + shift1).astype(BF16)
        ta = _dot(hb, wg_ref[:, 0:D_MODEL]) + bg_ref[:, 0:D_MODEL]
        ga = 1.0 / (1.0 + jnp.exp(-ta))
        tf = _dot(hb, wg_ref[:, D_MODEL:2 * D_MODEL]) + bg_ref[:, D_MODEL:2 * D_MODEL]
        gf = 1.0 / (1.0 + jnp.exp(-tf))
        ya = _dot(o_ref[rows, :], wa_ref[...])
        yf = _dot(yf_ref[rows, :], wf_ref[...])
        y = (ga * ya + gf * yf).astype(BF16)
        y2 = _dot(y, wo_ref[...])
        x1 = x + (y2 * _rms_scale(y2)) * gpost
        x1_ref[rows, :] = x1
        h2_ref[rows, :] = ((x1 * _rms_scale(x1)) * gmod2 + shift2).astype(BF16)


def _merge(o, yf, x2, mod3, g_pre1, w_gate, b_gate, wa, wf, wo, g_post, g_pre2, seq, tm):
    t = x2.shape[0]
    per_b = seq // tm
    row = lambda i: (i, 0)
    const = lambda i: (0, 0)
    return pl.pallas_call(
        _merge_kernel,
        grid=(t // tm,),
        in_specs=[
            pl.BlockSpec((tm, ATTN_WIDTH), row),
            pl.BlockSpec((tm, FOURIER_WIDTH), row),
            pl.BlockSpec((tm, D_MODEL), row),
            pl.BlockSpec((None, 6, D_MODEL), lambda i: (i // per_b, 0, 0)),
            pl.BlockSpec((1, D_MODEL), const),
            pl.BlockSpec((D_MODEL, 2 * D_MODEL), const),
            pl.BlockSpec((1, 2 * D_MODEL), const),
            pl.BlockSpec((ATTN_WIDTH, D_MODEL), const),
            pl.BlockSpec((FOURIER_WIDTH, D_MODEL), const),
            pl.BlockSpec((D_MODEL, D_MODEL), const),
            pl.BlockSpec((1, D_MODEL), const),
            pl.BlockSpec((1, D_MODEL), const),
        ],
        out_specs=[pl.BlockSpec((tm, D_MODEL), row), pl.BlockSpec((tm, D_MODEL), row)],
        out_shape=[jax.ShapeDtypeStruct((t, D_MODEL), F32),
                   jax.ShapeDtypeStruct((t, D_MODEL), BF16)],
        scratch_shapes=[pltpu.VMEM((ATTN_WIDTH, D_MODEL), BF16),
                        pltpu.VMEM((FOURIER_WIDTH, D_MODEL), BF16),
                        pltpu.VMEM((D_MODEL, D_MODEL), BF16)],
        compiler_params=_params(1),
        name="merge",
    )(o, yf, x2, mod3, g_pre1, w_gate, b_gate, wa, wf, wo, g_post, g_pre2)


_HALO_BLOCK = BF16_SUBLANES
_HALO = 8
_MXU_COLS = 256
_GELU_K1 = math.sqrt(2.0 / math.pi)
_GELU_K2 = _GELU_K1 * 0.044715


def _ffn_up_proj(h_ref, hp_ref, hn_ref, w_ref, hx_ref, hperm_ref, tiles_per_seq):
    tm = h_ref.shape[0]
    ext = tm + 2 * _HALO
    pitch = ext // 8
    i = pl.program_id(0)
    keep_prev = jnp.where(i % tiles_per_seq != 0, 1.0, 0.0)
    keep_next = jnp.where(i % tiles_per_seq != tiles_per_seq - 1, 1.0, 0.0)

    for c in range(D_MODEL // LANES):
        lc = slice(c * LANES, (c + 1) * LANES)
        hx_ref[c, 0:_HALO, :] = (
            hp_ref[:, lc].astype(F32)[_HALO_BLOCK - _HALO:_HALO_BLOCK, :] * keep_prev)
        hx_ref[c, _HALO:_HALO + tm, :] = h_ref[:, lc].astype(F32)
        hx_ref[c, _HALO + tm:ext, :] = hn_ref[:, lc].astype(F32)[0:_HALO, :] * keep_next
    for c in range(D_MODEL // LANES):
        lc = slice(c * LANES, (c + 1) * LANES)
        for b in range(0, pitch, 2):
            pair = jnp.concatenate([hx_ref[c, pl.ds(b, 8, stride=pitch), :],
                                    hx_ref[c, pl.ds(b + 1, 8, stride=pitch), :]], axis=0)
            hperm_ref[8 * b:8 * b + 16, lc] = pair.astype(BF16)
    hperm = hperm_ref[...]

    def pair_thunk(jj):
        def run(u_ref):
            for half in range(2):
                cols = slice(half * D_FF + jj * _MXU_COLS, half * D_FF + (jj + 1) * _MXU_COLS)
                u_ref[:, cols] = _dot(hperm, w_ref[:, cols])
        return run

    return [pair_thunk(jj) for jj in range(D_FF // _MXU_COLS)]


def _ffn_up_conv(u_ref, cw_ref, cb_ref, o_ref, st_ref, jj):
    tm = o_ref.shape[0]
    ext = tm + 2 * _HALO
    pitch = ext // 8
    for k in range(_MXU_COLS // LANES):
        j = 2 * jj + k
        la = slice(j * LANES, (j + 1) * LANES)
        lb = slice(D_FF + j * LANES, D_FF + (j + 1) * LANES)
        bc = lambda ref, r, l: jnp.broadcast_to(ref[r:r + 1, l], (8, LANES))
        wa = [bc(cw_ref, r, la) for r in range(CONV_WIDTH)]
        wb = [bc(cw_ref, r, lb) for r in range(CONV_WIDTH)]
        ba = bc(cb_ref, 0, la)
        bb = bc(cb_ref, 0, lb)

        def vreg(lanes, b):
            if b < 0:
                return pltpu.roll(u_ref[8 * (pitch - 1):8 * pitch, lanes], 1, 0)
            if b >= pitch:
                return pltpu.roll(u_ref[0:8, lanes], 7, 0)
            return u_ref[8 * b:8 * b + 8, lanes]

        for b in range(pitch):
            ca = vreg(la, b - 1) * wa[0] + vreg(la, b) * wa[1] + vreg(la, b + 1) * wa[2] + ba
            cv = vreg(lb, b - 1) * wb[0] + vreg(lb, b) * wb[1] + vreg(lb, b + 1) * wb[2] + bb
            t = jnp.tanh(ca * (_GELU_K1 + _GELU_K2 * (ca * ca)))
            st_ref[j % 2, pl.ds(b, 8, stride=pitch), :] = (ca * cv) * (1.0 + t)
        o_ref[:, la] = st_ref[j % 2, _HALO:_HALO + tm, :].astype(BF16)


def _ffn_up_kernel(h_ref, hp_ref, hn_ref, w_ref, cw_ref, cb_ref, o_ref,
                   hx_ref, hperm_ref, st_ref, u0_ref, u1_ref, *, tiles_per_seq, n_tiles):
    i = pl.program_id(0)
    u_refs = (u0_ref, u1_ref)
    n_pairs = D_FF // _MXU_COLS

    @pl.when(i == 0)
    def _():
        for thunk in _ffn_up_proj(h_ref, hp_ref, hn_ref, w_ref, hx_ref, hperm_ref, tiles_per_seq):
            thunk(u0_ref)

    for par in range(2):
        @pl.when((i > 0) & (i < n_tiles) & (i % 2 == par))
        def _(par=par):
            thunks = _ffn_up_proj(h_ref, hp_ref, hn_ref, w_ref, hx_ref, hperm_ref, tiles_per_seq)
            for jj in range(n_pairs):
                thunks[jj](u_refs[par])
                _ffn_up_conv(u_refs[1 - par], cw_ref, cb_ref, o_ref, st_ref, jj)

    @pl.when(i == n_tiles)
    def _():
        for jj in range(n_pairs):
            _ffn_up_conv(u_refs[(n_tiles - 1) % 2], cw_ref, cb_ref, o_ref, st_ref, jj)


def _ffn_up(h2, w_up, conv_w, conv_b, seq, tm):
    t = h2.shape[0]
    n_tiles = t // tm
    tiles_per_seq = seq // tm
    halo_per_tile = tm // _HALO_BLOCK
    n_halo = t // _HALO_BLOCK
    ext = tm + 2 * _HALO
    pitch = ext // 8
    assert ext % BF16_SUBLANES == 0 and pitch % 2 == 0 and pitch % 8 != 0, (tm, ext)
    const = lambda i: (0, 0)
    cur = lambda i: jnp.minimum(i, n_tiles - 1)
    return pl.pallas_call(
        functools.partial(_ffn_up_kernel, tiles_per_seq=tiles_per_seq, n_tiles=n_tiles),
        grid=(n_tiles + 1,),
        in_specs=[
            pl.BlockSpec((tm, D_MODEL), lambda i: (cur(i), 0)),
            pl.BlockSpec((_HALO_BLOCK, D_MODEL),
                         lambda i: (jnp.maximum(cur(i) * halo_per_tile - 1, 0), 0)),
            pl.BlockSpec((_HALO_BLOCK, D_MODEL),
                         lambda i: (jnp.minimum((cur(i) + 1) * halo_per_tile, n_halo - 1), 0)),
            pl.BlockSpec((D_MODEL, 2 * D_FF), const),
            pl.BlockSpec((CONV_WIDTH, 2 * D_FF), const),
            pl.BlockSpec((1, 2 * D_FF), const),
        ],
        out_specs=pl.BlockSpec((tm, D_FF), lambda i: (jnp.maximum(i - 1, 0), 0)),
        out_shape=jax.ShapeDtypeStruct((t, D_FF), BF16),
        scratch_shapes=[pltpu.VMEM((D_MODEL // LANES, ext, LANES), F32),
                        pltpu.VMEM((ext, D_MODEL), BF16),
                        pltpu.VMEM((2, ext, LANES), F32),
                        pltpu.VMEM((ext, 2 * D_FF), F32),
                        pltpu.VMEM((ext, 2 * D_FF), F32)],
        compiler_params=_params(1),
        name="ffn_up",
    )(h2, h2, h2, w_up, conv_w, conv_b)


def _ffn_down_kernel(a_ref, w32_ref, x1_ref, mod_ref, g_ref, o_ref, w_ref):
    _cast_once(w32_ref, w_ref)
    gpost = mod_ref[5:6, :] * g_ref[...]
    for r in range(a_ref.shape[0] // _ROW_CHUNK):
        rows = slice(r * _ROW_CHUNK, (r + 1) * _ROW_CHUNK)
        y = _dot(a_ref[rows, :], w_ref[...])
        o_ref[rows, :] = x1_ref[rows, :] + (y * _rms_scale(y)) * gpost


def _ffn_down(act, w_down, x1, mod3, g_post, seq, tm):
    t = x1.shape[0]
    per_b = seq // tm
    row = lambda i: (i, 0)
    const = lambda i: (0, 0)
    return pl.pallas_call(
        _ffn_down_kernel,
        grid=(t // tm,),
        in_specs=[
            pl.BlockSpec((tm, D_FF), row),
            pl.BlockSpec((D_FF, D_MODEL), const),
            pl.BlockSpec((tm, D_MODEL), row),
            pl.BlockSpec((None, 6, D_MODEL), lambda i: (i // per_b, 0, 0)),
            pl.BlockSpec((1, D_MODEL), const),
        ],
        out_specs=pl.BlockSpec((tm, D_MODEL), row),
        out_shape=jax.ShapeDtypeStruct((t, D_MODEL), F32),
        scratch_shapes=[pltpu.VMEM((D_FF, D_MODEL), BF16)],
        compiler_params=_params(1),
        name="ffn_down",
    )(act, w_down, x1, mod3, g_post)


@functools.lru_cache(maxsize=None)
def _rope_tables(seq):
    pos = np.arange(seq)
    row = (pos // GRID_W).astype(np.float64)
    col = (pos % GRID_W).astype(np.float64)
    inv = ROPE_THETA ** (-np.arange(ROPE_PAIRS_PER_AXIS, dtype=np.float64) / ROPE_PAIRS_PER_AXIS)
    ang = np.concatenate([row[:, None] * inv, col[:, None] * inv], axis=-1)
    cos = np.repeat(np.cos(ang), 2, axis=-1)
    sin = np.repeat(np.sin(ang), 2, axis=-1)
    sign = np.tile(np.array([-1.0, 1.0]), HEAD_DIM // 2)
    reps = LANES // HEAD_DIM
    return (np.tile(cos, (1, reps)).astype(np.float32),
            np.tile(sin * sign, (1, reps)).astype(np.float32))


@functools.lru_cache(maxsize=None)
def _dft_tables(seq):
    gd = FOURIER_GROUP_DIM
    kc = (np.outer(np.arange(gd), np.arange(gd)) % gd).astype(np.float64) * (2.0 * np.pi / gd)
    norm = 1.0 / math.sqrt(seq * gd)
    cs_c = np.concatenate([np.cos(kc), np.sin(kc)], axis=1) * norm
    n1 = seq // _DFT_SPLIT
    t_idx = np.arange(seq // 2)
    ang_a = (np.outer(np.arange(n1), t_idx) % n1).astype(np.float64) * (2.0 * np.pi / n1)
    ang_b = (np.outer(np.arange(_DFT_SPLIT), t_idx) % seq).astype(np.float64) * (2.0 * np.pi / seq)
    cos_b = np.cos(ang_b)
    cos_b[:, 0] *= 0.5
    jw = np.zeros((_REV_BLOCK, 2 * _REV_BLOCK))
    jw[np.arange(_REV_BLOCK), _REV_BLOCK - np.arange(_REV_BLOCK)] = 1.0
    f32 = lambda a: a.astype(np.float32)
    return (f32(cs_c), f32(jw), f32(np.cos(ang_a)), f32(np.sin(ang_a)), f32(cos_b),
            f32(np.sin(ang_b)))


def kernel(x, c, w_ada, b_ada, mix_pre_g, w_in, q_norm_g, k_norm_g, b_gate, w_attn_branch,
           w_fourier_branch, w_out, mix_post_g, ffn_pre_g, w_up, conv_w, conv_b, w_down,
           ffn_post_g):
    batch, seq, _ = x.shape
    depth = w_ada.shape[0]
    t = batch * seq
    tiles = _TILES

    cos_np, sin_np = _rope_tables(seq)
    cos_t = jnp.asarray(cos_np)
    sin_t = jnp.asarray(sin_np)
    dft_tabs = [jnp.asarray(a) for a in _dft_tables(seq)]
    seg = np.arange(ATTN_WIDTH) // HEAD_DIM
    ones_blk = jnp.asarray((seg[:, None] == seg[None, :]).astype(np.float32), dtype=BF16)

    x2 = x.reshape(t, D_MODEL)
    for l in range(depth):
        mod3 = _adaln(c, w_ada[l], b_ada[l]).reshape(batch, 6, D_MODEL)
        gq = (jnp.tile(q_norm_g[l], N_Q_HEADS) * (HEAD_DIM ** -0.5 * _LOG2_E)).reshape(1, ATTN_WIDTH)
        gk = jnp.tile(k_norm_g[l], N_KV_HEADS).reshape(1, KV_WIDTH)
        g_pre1 = mix_pre_g[l].reshape(1, D_MODEL)
        qt, kp, vt, f = _in_proj(
            x2, mod3, g_pre1, w_in[l], gq, gk, cos_t, sin_t, ones_blk, batch, seq,
            tiles["in_proj"])
        o = _attention(qt, kp, vt, batch, seq, tiles["attn"])
        yf = _fourier(f, *dft_tabs, batch, seq)
        x1, h2 = _merge(
            o, yf, x2, mod3, g_pre1, w_in[l, :, _O_GA:].astype(BF16),
            b_gate[l].reshape(1, 2 * D_MODEL),
            w_attn_branch[l], w_fourier_branch[l], w_out[l],
            mix_post_g[l].reshape(1, D_MODEL), ffn_pre_g[l].reshape(1, D_MODEL), seq,
            tiles["merge"])
        glu_half = jnp.concatenate([jnp.ones((D_FF,), F32), jnp.full((D_FF,), 0.5, F32)])
        act = _ffn_up(h2, w_up[l].astype(BF16), conv_w[l] * glu_half,
                      (conv_b[l] * glu_half).reshape(1, 2 * D_FF), seq, tiles["ffn_up"])
        x2 = _ffn_down(act, w_down[l], x1, mod3,
                       ffn_post_g[l].reshape(1, D_MODEL), seq, tiles["ffn_down"])
    return x2.reshape(batch, seq, D_MODEL)
```

```python
import functools
import math

import numpy as np
import jax
import jax.numpy as jnp
from jax import lax
from jax.experimental import pallas as pl
from jax.experimental.pallas import tpu as pltpu

D_MODEL = 1024
GRID_W = 64
HEAD_DIM = 64
N_Q_HEADS = 8
N_KV_HEADS = 2
Q_PER_KV = N_Q_HEADS // N_KV_HEADS
ATTN_WIDTH = N_Q_HEADS * HEAD_DIM
KV_WIDTH = N_KV_HEADS * HEAD_DIM
N_FOURIER_GROUPS = 4
FOURIER_GROUP_DIM = 128
FOURIER_WIDTH = N_FOURIER_GROUPS * FOURIER_GROUP_DIM
IN_WIDTH = ATTN_WIDTH + 2 * KV_WIDTH + FOURIER_WIDTH + 2 * D_MODEL
ROPE_THETA = 10000.0
ROPE_PAIRS_PER_AXIS = HEAD_DIM // 4
D_FF = 2816
CONV_WIDTH = 3
NORM_EPS = 1e-6

LANES = 128
BF16_SUBLANES = 16
VMEM_LIMIT_BYTES = 56 * 1024 * 1024
_ROW_CHUNK = 256
_V_ONES_ROWS = BF16_SUBLANES
_V_ROWS = HEAD_DIM + _V_ONES_ROWS
_TILES = {"in_proj": 1024, "attn": 256, "merge": 1024, "ffn_up": 512, "ffn_down": 512}

_LOG2_E = math.log2(math.e)

F32 = jnp.float32
BF16 = jnp.bfloat16

_O_K = ATTN_WIDTH
_O_V = _O_K + KV_WIDTH
_O_F = _O_V + KV_WIDTH
_O_GA = _O_F + FOURIER_WIDTH
_O_GF = _O_GA + D_MODEL


def _params(n_grid_dims, flags=None):
    return pltpu.CompilerParams(
        dimension_semantics=("arbitrary",) * n_grid_dims,
        vmem_limit_bytes=VMEM_LIMIT_BYTES,
        flags=flags,
    )


def _dot(a, b):
    return jnp.dot(a, b, preferred_element_type=F32)


def _rms_scale(v):
    return lax.rsqrt(jnp.mean(v * v, axis=-1, keepdims=True) + NORM_EPS)


def _adaln_kernel(c_ref, w_ref, b_ref, o_ref):
    c = c_ref[...]
    a = c / (1.0 + jnp.exp(-c))
    a_hi = a.astype(BF16)
    a_lo = (a - a_hi.astype(F32)).astype(BF16)
    n = a.shape[0]
    res = _dot(jnp.concatenate([a_hi, a_lo], axis=0), w_ref[...].astype(BF16))
    o_ref[...] = res[0:n, :] + res[n:2 * n, :] + b_ref[...]


def _adaln(c, w_ada, b_ada):
    batch = c.shape[0]
    n = w_ada.shape[1]
    tn = 1536
    return pl.pallas_call(
        _adaln_kernel,
        grid=(n // tn,),
        in_specs=[
            pl.BlockSpec((batch, D_MODEL), lambda j: (0, 0)),
            pl.BlockSpec((D_MODEL, tn), lambda j: (0, j)),
            pl.BlockSpec((1, tn), lambda j: (0, j)),
        ],
        out_specs=pl.BlockSpec((batch, tn), lambda j: (0, j)),
        out_shape=jax.ShapeDtypeStruct((batch, n), F32),
        compiler_params=_params(1),
        name="adaln",
    )(c, w_ada, b_ada.reshape(1, n))


def _swap_pairs(v):
    n = v.shape[-1]
    lane = lax.broadcasted_iota(jnp.int32, v.shape, v.ndim - 1)
    from_right = pltpu.roll(v, n - 1, v.ndim - 1)
    from_left = pltpu.roll(v, 1, v.ndim - 1)
    return jnp.where((lane & 1) == 0, from_right, from_left)


def _head_norm_rope(z, gain, ones_blk, cos, sin):
    reps = z.shape[-1] // LANES
    ssq = _dot((z * z).astype(BF16), ones_blk)
    r = lax.rsqrt(ssq * (1.0 / HEAD_DIM) + NORM_EPS)
    zg = z * gain
    cos_t = jnp.concatenate([cos] * reps, axis=-1) if reps > 1 else cos
    sin_t = jnp.concatenate([sin] * reps, axis=-1) if reps > 1 else sin
    return (zg * cos_t + _swap_pairs(zg) * sin_t) * r


def _cast_once(src_ref, dst_ref):
    @pl.when(pl.program_id(0) == 0)
    def _():
        dst_ref[...] = src_ref[...].astype(BF16)


def _in_proj_kernel(x_ref, mod_ref, g_ref, w32_ref, gq_ref, gk_ref,
                    cos_ref, sin_ref, ones_ref,
                    qt_ref, kp_ref, vt_ref, f_ref, w_ref):
    _cast_once(w32_ref, w_ref)
    shift = mod_ref[0:1, :]
    scale = mod_ref[1:2, :]
    gmod = g_ref[...] * (1.0 + scale)
    for r in range(x_ref.shape[0] // _ROW_CHUNK):
        rows = slice(r * _ROW_CHUNK, (r + 1) * _ROW_CHUNK)
        x = x_ref[rows, :]
        hb = ((x * _rms_scale(x)) * gmod + shift).astype(BF16)
        cos = cos_ref[rows, :]
        sin = sin_ref[rows, :]

        zq = _dot(hb, w_ref[:, 0:_O_K])
        q = _head_norm_rope(zq, gq_ref[...], ones_ref[...], cos, sin)
        qt_ref[:, rows] = q.T.astype(BF16)

        zkv = _dot(hb, w_ref[:, _O_K:_O_F])
        k = _head_norm_rope(zkv[:, 0:KV_WIDTH], gk_ref[...],
                            ones_ref[0:KV_WIDTH, 0:KV_WIDTH], cos, sin)
        k_sw = pltpu.roll(k, HEAD_DIM, 1)
        first = lax.broadcasted_iota(jnp.int32, k.shape, 1) < HEAD_DIM
        kp_ref[rows, :] = jnp.concatenate(
            [jnp.where(first, k, 0.0), jnp.where(first, 0.0, k_sw),
             jnp.where(first, k_sw, 0.0), jnp.where(first, 0.0, k)], axis=-1).astype(BF16)
        zvt = zkv[:, KV_WIDTH:2 * KV_WIDTH].T
        ones_rows = jnp.ones((_V_ONES_ROWS, zvt.shape[1]), F32)
        vt_ref[:, rows] = jnp.concatenate(
            [piece for kv in range(N_KV_HEADS)
             for piece in (zvt[kv * HEAD_DIM:(kv + 1) * HEAD_DIM, :], ones_rows)],
            axis=0).astype(BF16)

        f_ref[rows, :] = _dot(hb, w_ref[:, _O_F:_O_GA]).astype(BF16)


def _in_proj(x2, mod3, g_pre, w_in, gq, gk, cos_t, sin_t, ones_blk, batch, seq, tm):
    t = batch * seq
    per_b = seq // tm
    row = lambda i: (i, 0)
    const = lambda i: (0, 0)
    return pl.pallas_call(
        _in_proj_kernel,
        grid=(t // tm,),
        in_specs=[
            pl.BlockSpec((tm, D_MODEL), row),
            pl.BlockSpec((None, 6, D_MODEL), lambda i: (i // per_b, 0, 0)),
            pl.BlockSpec((1, D_MODEL), const),
            pl.BlockSpec((D_MODEL, _O_GA), const),
            pl.BlockSpec((1, ATTN_WIDTH), const),
            pl.BlockSpec((1, KV_WIDTH), const),
            pl.BlockSpec((tm, LANES), lambda i: (i % per_b, 0)),
            pl.BlockSpec((tm, LANES), lambda i: (i % per_b, 0)),
            pl.BlockSpec((ATTN_WIDTH, ATTN_WIDTH), const),
        ],
        out_specs=[
            pl.BlockSpec((None, ATTN_WIDTH, tm), lambda i: (i // per_b, 0, i % per_b)),
            pl.BlockSpec((tm, 2 * N_KV_HEADS * LANES), row),
            pl.BlockSpec((None, N_KV_HEADS * _V_ROWS, tm), lambda i: (i // per_b, 0, i % per_b)),
            pl.BlockSpec((tm, FOURIER_WIDTH), row),
        ],
        out_shape=[
            jax.ShapeDtypeStruct((batch, ATTN_WIDTH, seq), BF16),
            jax.ShapeDtypeStruct((t, 2 * N_KV_HEADS * LANES), BF16),
            jax.ShapeDtypeStruct((batch, N_KV_HEADS * _V_ROWS, seq), BF16),
            jax.ShapeDtypeStruct((t, FOURIER_WIDTH), BF16),
        ],
        scratch_shapes=[pltpu.VMEM((D_MODEL, _O_GA), BF16)],
        compiler_params=_params(1),
        name="in_proj",
    )(x2, mod3, g_pre, w_in, gq, gk, cos_t, sin_t, ones_blk)


_ATTN_SLOTS = 2


def _attn_step(qt_ref, kp_ref, vt_ref, o_ref, p_ref, s_new_ref, s_old_ref):
    outs = []
    for h in range(N_Q_HEADS):
        kv = h // Q_PER_KV
        slot = h % _ATTN_SLOTS
        pair = h // 2
        variant = 2 * kv + h % 2
        s_new_ref[h] = _dot(kp_ref[:, variant * LANES:(variant + 1) * LANES],
                            qt_ref[pair * LANES:(pair + 1) * LANES, :])
        s = s_old_ref[h]
        m = jnp.max(s, axis=0, keepdims=True)
        p_ref[slot] = jnp.exp2(s - m).astype(BF16)
        ov = _dot(vt_ref[kv * _V_ROWS:(kv + 1) * _V_ROWS, :], p_ref[slot])
        on = ov[0:HEAD_DIM, :] / ov[HEAD_DIM:HEAD_DIM + 1, :]
        outs.append(on.T)
    o_ref[...] = jnp.concatenate(outs, axis=-1).astype(BF16)


def _attn_kernel(qt_ref, kp_ref, vt_ref, o_ref, p_ref, sa_ref, sb_ref):
    g = pl.program_id(0)

    @pl.when(g == 0)
    def _():
        sb_ref[...] = jnp.zeros(sb_ref.shape, F32)

    @pl.when(g % 2 == 0)
    def _():
        _attn_step(qt_ref, kp_ref, vt_ref, o_ref, p_ref, sa_ref, sb_ref)

    @pl.when(g % 2 == 1)
    def _():
        _attn_step(qt_ref, kp_ref, vt_ref, o_ref, p_ref, sb_ref, sa_ref)


def _attention(qt, kp, vt, batch, seq, tq):
    t = batch * seq
    per_b = seq // tq
    n_blk = t // tq
    cur = lambda g: jnp.minimum(g, n_blk - 1)
    prev = lambda g: jnp.maximum(g - 1, 0)
    return pl.pallas_call(
        _attn_kernel,
        grid=(n_blk + 1,),
        in_specs=[
            pl.BlockSpec((None, ATTN_WIDTH, tq), lambda g: (cur(g) // per_b, 0, cur(g) % per_b)),
            pl.BlockSpec((seq, 2 * N_KV_HEADS * LANES), lambda g: (cur(g) // per_b, 0)),
            pl.BlockSpec((None, N_KV_HEADS * _V_ROWS, seq), lambda g: (prev(g) // per_b, 0, 0)),
        ],
        out_specs=pl.BlockSpec((tq, ATTN_WIDTH), lambda g: (prev(g), 0)),
        out_shape=jax.ShapeDtypeStruct((t, ATTN_WIDTH), BF16),
        scratch_shapes=[pltpu.VMEM((_ATTN_SLOTS, seq, tq), BF16),
                        pltpu.VMEM((N_Q_HEADS, seq, tq), F32),
                        pltpu.VMEM((N_Q_HEADS, seq, tq), F32)],
        compiler_params=_params(1),
        name="attn",
    )(qt, kp, vt)


_DFT_SPLIT = 32
_REV_BLOCK = 256


def _fourier_kernel(f_ref, cs_ref, jw_ref, ca_ref, sa_ref, cb_ref, sb_ref, o_ref,
                    w_ref, x_ref, r_ref, d_ref):
    seq = f_ref.shape[0]
    half = seq // 2
    gd = FOURIER_GROUP_DIM
    fw = FOURIER_WIDTH

    @pl.when(pl.program_id(0) == 0)
    def _():
        cb = cb_ref[...]
        sb = sb_ref[...]

        def body(s1, carry):
            ca = ca_ref[pl.ds(s1, 1), :]
            sa = sa_ref[pl.ds(s1, 1), :]
            rows = pl.ds(pl.multiple_of(s1 * _DFT_SPLIT, _DFT_SPLIT), _DFT_SPLIT)
            w_ref[rows, 0:half] = (ca * cb - sa * sb).astype(BF16)
            w_ref[rows, half:seq] = (-(sa * cb + ca * sb)).astype(BF16)
            return carry

        lax.fori_loop(0, w_ref.shape[0] // _DFT_SPLIT, body, 0)

    cs = cs_ref[...].astype(BF16)
    for g in range(N_FOURIER_GROUPS):
        xcs = _dot(f_ref[:, g * gd:(g + 1) * gd], cs)
        x_ref[:, g * gd:(g + 1) * gd] = xcs[:, 0:gd].astype(BF16)
        x_ref[:, fw + g * gd:fw + (g + 1) * gd] = xcs[:, gd:2 * gd].astype(BF16)

    n_blk = seq // _REV_BLOCK
    jw = jw_ref[...].astype(BF16)
    for a in range(half // _REV_BLOCK):
        beta = n_blk - 1 - a
        nxt = (beta + 1) % n_blk
        win = jnp.concatenate([x_ref[beta * _REV_BLOCK:(beta + 1) * _REV_BLOCK, :],
                               x_ref[nxt * _REV_BLOCK:(nxt + 1) * _REV_BLOCK, :]], axis=0)
        rev = _dot(jw, win)
        top = x_ref[a * _REV_BLOCK:(a + 1) * _REV_BLOCK, :].astype(F32)
        rows = slice(a * _REV_BLOCK, (a + 1) * _REV_BLOCK)
        r_ref[rows, :] = (top[:, 0:fw] + rev[:, 0:fw]).astype(BF16)
        r_ref[half + a * _REV_BLOCK:half + (a + 1) * _REV_BLOCK, :] = (
            top[:, fw:2 * fw] - rev[:, fw:2 * fw]).astype(BF16)

    m_rows = d_ref.shape[0] - _REV_BLOCK + BF16_SUBLANES
    x_mid = x_ref[half:half + 1, 0:fw].astype(F32)
    d_ref[m_rows:d_ref.shape[0], :] = jnp.zeros((d_ref.shape[0] - m_rows, fw), BF16)
    split = half // 2
    for lo_r, hi_r in ((0, split), (split, m_rows)):
        n = hi_r - lo_r
        odd = (lax.broadcasted_iota(jnp.int32, (n, fw), 0) & 1) == 1
        a_part = _dot(w_ref[lo_r:hi_r, 0:half], r_ref[0:half, :]) + jnp.where(odd, -x_mid, x_mid)
        b_part = _dot(w_ref[lo_r:hi_r, half:seq], r_ref[half:seq, :])
        keep = min(hi_r, half) - lo_r
        o_ref[lo_r:lo_r + keep, :] = (a_part + b_part)[0:keep, :].astype(BF16)
        d_ref[lo_r:hi_r, :] = (a_part - b_part).astype(BF16)
    for a in range(half // _REV_BLOCK):
        beta = half // _REV_BLOCK - 1 - a
        win = d_ref[beta * _REV_BLOCK:(beta + 2) * _REV_BLOCK, :]
        o_ref[half + a * _REV_BLOCK:half + (a + 1) * _REV_BLOCK, :] = _dot(jw, win).astype(BF16)


def _fourier(f, cs_c, jw, ca, sa, cb, sb, batch, seq):
    t = batch * seq
    const = lambda b: (0, 0)
    return pl.pallas_call(
        _fourier_kernel,
        grid=(batch,),
        in_specs=[
            pl.BlockSpec((seq, FOURIER_WIDTH), lambda b: (b, 0)),
            pl.BlockSpec(cs_c.shape, const),
            pl.BlockSpec(jw.shape, const),
            pl.BlockSpec(ca.shape, const),
            pl.BlockSpec(sa.shape, const),
            pl.BlockSpec(cb.shape, const),
            pl.BlockSpec(sb.shape, const),
        ],
        out_specs=pl.BlockSpec((seq, FOURIER_WIDTH), lambda b: (b, 0)),
        out_shape=jax.ShapeDtypeStruct((t, FOURIER_WIDTH), BF16),
        scratch_shapes=[
            pltpu.VMEM(((seq // 2 // _DFT_SPLIT + 1) * _DFT_SPLIT, seq), BF16),
            pltpu.VMEM((seq, 2 * FOURIER_WIDTH), BF16),
            pltpu.VMEM((seq, FOURIER_WIDTH), BF16),
            pltpu.VMEM((seq // 2 + _REV_BLOCK, FOURIER_WIDTH), BF16)],
        compiler_params=_params(1),
        name="fourier",
    )(f, cs_c, jw, ca, sa, cb, sb)


def _merge_kernel(o_ref, yf_ref, x_ref, mod_ref, gpre1_ref, wg_ref, bg_ref,
                  wa32_ref, wf32_ref, wo32_ref, gpost_ref, gpre2_ref,
                  x1_ref, h2_ref, wa_ref, wf_ref, wo_ref):
    _cast_once(wa32_ref, wa_ref)
    _cast_once(wf32_ref, wf_ref)
    _cast_once(wo32_ref, wo_ref)
    shift1 = mod_ref[0:1, :]
    scale1 = mod_ref[1:2, :]
    gate1 = mod_ref[2:3, :]
    shift2 = mod_ref[3:4, :]
    scale2 = mod_ref[4:5, :]
    gmod1 = gpre1_ref[...] * (1.0 + scale1)
    gpost = gate1 * gpost_ref[...]
    gmod2 = gpre2_ref[...] * (1.0 + scale2)
    for r in range(x_ref.shape[0] // _ROW_CHUNK):
        rows = slice(r * _ROW_CHUNK, (r + 1) * _ROW_CHUNK)
        x = x_ref[rows, :]
        hb = ((x * _rms_scale(x)) * gmod1 + shift1).astype(BF16)
        ta = _dot(hb, wg_ref[:, 0:D_MODEL]) + bg_ref[:, 0:D_MODEL]
        ga = 1.0 / (1.0 + jnp.exp(-ta))
        tf = _dot(hb, wg_ref[:, D_MODEL:2 * D_MODEL]) + bg_ref[:, D_MODEL:2 * D_MODEL]
        gf = 1.0 / (1.0 + jnp.exp(-tf))
        ya = _dot(o_ref[rows, :], wa_ref[...])
        yf = _dot(yf_ref[rows, :], wf_ref[...])
        y = (ga * ya + gf * yf).astype(BF16)
        y2 = _dot(y, wo_ref[...])
        x1 = x + (y2 * _rms_scale(y2)) * gpost
        x1_ref[rows, :] = x1
        h2_ref[rows, :] = ((x1 * _rms_scale(x1)) * gmod2 + shift2).astype(BF16)


def _merge(o, yf, x2, mod3, g_pre1, w_gate, b_gate, wa, wf, wo, g_post, g_pre2, seq, tm):
    t = x2.shape[0]
    per_b = seq // tm
    row = lambda i: (i, 0)
    const = lambda i: (0, 0)
    return pl.pallas_call(
        _merge_kernel,
        grid=(t // tm,),
        in_specs=[
            pl.BlockSpec((tm, ATTN_WIDTH), row),
            pl.BlockSpec((tm, FOURIER_WIDTH), row),
            pl.BlockSpec((tm, D_MODEL), row),
            pl.BlockSpec((None, 6, D_MODEL), lambda i: (i // per_b, 0, 0)),
            pl.BlockSpec((1, D_MODEL), const),
            pl.BlockSpec((D_MODEL, 2 * D_MODEL), const),
            pl.BlockSpec((1, 2 * D_MODEL), const),
            pl.BlockSpec((ATTN_WIDTH, D_MODEL), const),
            pl.BlockSpec((FOURIER_WIDTH, D_MODEL), const),
            pl.BlockSpec((D_MODEL, D_MODEL), const),
            pl.BlockSpec((1, D_MODEL), const),
            pl.BlockSpec((1, D_MODEL), const),
        ],
        out_specs=[pl.BlockSpec((tm, D_MODEL), row), pl.BlockSpec((tm, D_MODEL), row)],
        out_shape=[jax.ShapeDtypeStruct((t, D_MODEL), F32),
                   jax.ShapeDtypeStruct((t, D_MODEL), BF16)],
        scratch_shapes=[pltpu.VMEM((ATTN_WIDTH, D_MODEL), BF16),
                        pltpu.VMEM((FOURIER_WIDTH, D_MODEL), BF16),
                        pltpu.VMEM((D_MODEL, D_MODEL), BF16)],
        compiler_params=_params(1),
        name="merge",
    )(o, yf, x2, mod3, g_pre1, w_gate, b_gate, wa, wf, wo, g_post, g_pre2)


_HALO_BLOCK = BF16_SUBLANES
_HALO = 8
_MXU_COLS = 256
_GELU_K1 = math.sqrt(2.0 / math.pi)
_GELU_K2 = _GELU_K1 * 0.044715


def _ffn_up_proj(h_ref, hp_ref, hn_ref, w_ref, hx_ref, hperm_ref, tiles_per_seq):
    tm = h_ref.shape[0]
    ext = tm + 2 * _HALO
    pitch = ext // 8
    i = pl.program_id(0)
    keep_prev = jnp.where(i % tiles_per_seq != 0, 1.0, 0.0)
    keep_next = jnp.where(i % tiles_per_seq != tiles_per_seq - 1, 1.0, 0.0)

    for c in range(D_MODEL // LANES):
        lc = slice(c * LANES, (c + 1) * LANES)
        hx_ref[c, 0:_HALO, :] = (
            hp_ref[:, lc].astype(F32)[_HALO_BLOCK - _HALO:_HALO_BLOCK, :] * keep_prev)
        hx_ref[c, _HALO:_HALO + tm, :] = h_ref[:, lc].astype(F32)
        hx_ref[c, _HALO + tm:ext, :] = hn_ref[:, lc].astype(F32)[0:_HALO, :] * keep_next
    for c in range(D_MODEL // LANES):
        lc = slice(c * LANES, (c + 1) * LANES)
        for b in range(0, pitch, 2):
            pair = jnp.concatenate([hx_ref[c, pl.ds(b, 8, stride=pitch), :],
                                    hx_ref[c, pl.ds(b + 1, 8, stride=pitch), :]], axis=0)
            hperm_ref[8 * b:8 * b + 16, lc] = pair.astype(BF16)
    hperm = hperm_ref[...]

    def pair_thunk(jj):
        def run(u_ref):
            for half in range(2):
                cols = slice(half * D_FF + jj * _MXU_COLS, half * D_FF + (jj + 1) * _MXU_COLS)
                u_ref[:, cols] = _dot(hperm, w_ref[:, cols])
        return run

    return [pair_thunk(jj) for jj in range(D_FF // _MXU_COLS)]


def _ffn_up_conv(u_ref, cw_ref, cb_ref, o_ref, st_ref, jj):
    tm = o_ref.shape[0]
    ext = tm + 2 * _HALO
    pitch = ext // 8
    for k in range(_MXU_COLS // LANES):
        j = 2 * jj + k
        la = slice(j * LANES, (j + 1) * LANES)
        lb = slice(D_FF + j * LANES, D_FF + (j + 1) * LANES)
        bc = lambda ref, r, l: jnp.broadcast_to(ref[r:r + 1, l], (8, LANES))
        wa = [bc(cw_ref, r, la) for r in range(CONV_WIDTH)]
        wb = [bc(cw_ref, r, lb) for r in range(CONV_WIDTH)]
        ba = bc(cb_ref, 0, la)
        bb = bc(cb_ref, 0, lb)

        def vreg(lanes, b):
            if b < 0:
                return pltpu.roll(u_ref[8 * (pitch - 1):8 * pitch, lanes], 1, 0)
            if b >= pitch:
                return pltpu.roll(u_ref[0:8, lanes], 7, 0)
            return u_ref[8 * b:8 * b + 8, lanes]

        for b in range(pitch):
            ca = vreg(la, b - 1) * wa[0] + vreg(la, b) * wa[1] + vreg(la, b + 1) * wa[2] + ba
            cv = vreg(lb, b - 1) * wb[0] + vreg(lb, b) * wb[1] + vreg(lb, b + 1) * wb[2] + bb
            t = jnp.tanh(ca * (_GELU_K1 + _GELU_K2 * (ca * ca)))
            st_ref[j % 2, pl.ds(b, 8, stride=pitch), :] = (ca * cv) * (1.0 + t)
        o_ref[:, la] = st_ref[j % 2, _HALO:_HALO + tm, :].astype(BF16)


def _ffn_up_kernel(h_ref, hp_ref, hn_ref, w_ref, cw_ref, cb_ref, o_ref,
                   hx_ref, hperm_ref, st_ref, u0_ref, u1_ref, *, tiles_per_seq, n_tiles):
    i = pl.program_id(0)
    u_refs = (u0_ref, u1_ref)
    n_pairs = D_FF // _MXU_COLS

    @pl.when(i == 0)
    def _():
        for thunk in _ffn_up_proj(h_ref, hp_ref, hn_ref, w_ref, hx_ref, hperm_ref, tiles_per_seq):
            thunk(u0_ref)

    for par in range(2):
        @pl.when((i > 0) & (i < n_tiles) & (i % 2 == par))
        def _(par=par):
            thunks = _ffn_up_proj(h_ref, hp_ref, hn_ref, w_ref, hx_ref, hperm_ref, tiles_per_seq)
            for jj in range(n_pairs):
                thunks[jj](u_refs[par])
                _ffn_up_conv(u_refs[1 - par], cw_ref, cb_ref, o_ref, st_ref, jj)

    @pl.when(i == n_tiles)
    def _():
        for jj in range(n_pairs):
            _ffn_up_conv(u_refs[(n_tiles - 1) % 2], cw_ref, cb_ref, o_ref, st_ref, jj)


def _ffn_up(h2, w_up, conv_w, conv_b, seq, tm):
    t = h2.shape[0]
    n_tiles = t // tm
    tiles_per_seq = seq // tm
    halo_per_tile = tm // _HALO_BLOCK
    n_halo = t // _HALO_BLOCK
    ext = tm + 2 * _HALO
    pitch = ext // 8
    assert ext % BF16_SUBLANES == 0 and pitch % 2 == 0 and pitch % 8 != 0, (tm, ext)
    const = lambda i: (0, 0)
    cur = lambda i: jnp.minimum(i, n_tiles - 1)
    return pl.pallas_call(
        functools.partial(_ffn_up_kernel, tiles_per_seq=tiles_per_seq, n_tiles=n_tiles),
        grid=(n_tiles + 1,),
        in_specs=[
            pl.BlockSpec((tm, D_MODEL), lambda i: (cur(i), 0)),
            pl.BlockSpec((_HALO_BLOCK, D_MODEL),
                         lambda i: (jnp.maximum(cur(i) * halo_per_tile - 1, 0), 0)),
            pl.BlockSpec((_HALO_BLOCK, D_MODEL),
                         lambda i: (jnp.minimum((cur(i) + 1) * halo_per_tile, n_halo - 1), 0)),
            pl.BlockSpec((D_MODEL, 2 * D_FF), const),
            pl.BlockSpec((CONV_WIDTH, 2 * D_FF), const),
            pl.BlockSpec((1, 2 * D_FF), const),
        ],
        out_specs=pl.BlockSpec((tm, D_FF), lambda i: (jnp.maximum(i - 1, 0), 0)),
        out_shape=jax.ShapeDtypeStruct((t, D_FF), BF16),
        scratch_shapes=[pltpu.VMEM((D_MODEL // LANES, ext, LANES), F32),
                        pltpu.VMEM((ext, D_MODEL), BF16),
                        pltpu.VMEM((2, ext, LANES), F32),
                        pltpu.VMEM((ext, 2 * D_FF), F32),
                        pltpu.VMEM((ext, 2 * D_FF), F32)],
        compiler_params=_params(1),
        name="ffn_up",
    )(h2, h2, h2, w_up, conv_w, conv_b)


def _ffn_down_kernel(a_ref, w32_ref, x1_ref, mod_ref, g_ref, o_ref, w_ref):
    _cast_once(w32_ref, w_ref)
    gpost = mod_ref[5:6, :] * g_ref[...]
    for r in range(a_ref.shape[0] // _ROW_CHUNK):
        rows = slice(r * _ROW_CHUNK, (r + 1) * _ROW_CHUNK)
        y = _dot(a_ref[rows, :], w_ref[...])
        o_ref[rows, :] = x1_ref[rows, :] + (y * _rms_scale(y)) * gpost


def _ffn_down(act, w_down, x1, mod3, g_post, seq, tm):
    t = x1.shape[0]
    per_b = seq // tm
    row = lambda i: (i, 0)
    const = lambda i: (0, 0)
    return pl.pallas_call(
        _ffn_down_kernel,
        grid=(t // tm,),
        in_specs=[
            pl.BlockSpec((tm, D_FF), row),
            pl.BlockSpec((D_FF, D_MODEL), const),
            pl.BlockSpec((tm, D_MODEL), row),
            pl.BlockSpec((None, 6, D_MODEL), lambda i: (i // per_b, 0, 0)),
            pl.BlockSpec((1, D_MODEL), const),
        ],
        out_specs=pl.BlockSpec((tm, D_MODEL), row),
        out_shape=jax.ShapeDtypeStruct((t, D_MODEL), F32),
        scratch_shapes=[pltpu.VMEM((D_FF, D_MODEL), BF16)],
        compiler_params=_params(1),
        name="ffn_down",
    )(act, w_down, x1, mod3, g_post)


@functools.lru_cache(maxsize=None)
def _rope_tables(seq):
    pos = np.arange(seq)
    row = (pos // GRID_W).astype(np.float64)
    col = (pos % GRID_W).astype(np.float64)
    inv = ROPE_THETA ** (-np.arange(ROPE_PAIRS_PER_AXIS, dtype=np.float64) / ROPE_PAIRS_PER_AXIS)
    ang = np.concatenate([row[:, None] * inv, col[:, None] * inv], axis=-1)
    cos = np.repeat(np.cos(ang), 2, axis=-1)
    sin = np.repeat(np.sin(ang), 2, axis=-1)
    sign = np.tile(np.array([-1.0, 1.0]), HEAD_DIM // 2)
    reps = LANES // HEAD_DIM
    return (np.tile(cos, (1, reps)).astype(np.float32),
            np.tile(sin * sign, (1, reps)).astype(np.float32))


@functools.lru_cache(maxsize=None)
def _dft_tables(seq):
    gd = FOURIER_GROUP_DIM
    kc = (np.outer(np.arange(gd), np.arange(gd)) % gd).astype(np.float64) * (2.0 * np.pi / gd)
    norm = 1.0 / math.sqrt(seq * gd)
    cs_c = np.concatenate([np.cos(kc), np.sin(kc)], axis=1) * norm
    n1 = seq // _DFT_SPLIT
    t_idx = np.arange(seq // 2)
    ang_a = (np.outer(np.arange(n1), t_idx) % n1).astype(np.float64) * (2.0 * np.pi / n1)
    ang_b = (np.outer(np.arange(_DFT_SPLIT), t_idx) % seq).astype(np.float64) * (2.0 * np.pi / seq)
    cos_b = np.cos(ang_b)
    cos_b[:, 0] *= 0.5
    jw = np.zeros((_REV_BLOCK, 2 * _REV_BLOCK))
    jw[np.arange(_REV_BLOCK), _REV_BLOCK - np.arange(_REV_BLOCK)] = 1.0
    f32 = lambda a: a.astype(np.float32)
    return (f32(cs_c), f32(jw), f32(np.cos(ang_a)), f32(np.sin(ang_a)), f32(cos_b),
            f32(np.sin(ang_b)))


def kernel(x, c, w_ada, b_ada, mix_pre_g, w_in, q_norm_g, k_norm_g, b_gate, w_attn_branch,
           w_fourier_branch, w_out, mix_post_g, ffn_pre_g, w_up, conv_w, conv_b, w_down,
           ffn_post_g):
    batch, seq, _ = x.shape
    depth = w_ada.shape[0]
    t = batch * seq
    tiles = _TILES

    cos_np, sin_np = _rope_tables(seq)
    cos_t = jnp.asarray(cos_np)
    sin_t = jnp.asarray(sin_np)
    dft_tabs = [jnp.asarray(a) for a in _dft_tables(seq)]
    seg = np.arange(ATTN_WIDTH) // HEAD_DIM
    ones_blk = jnp.asarray((seg[:, None] == seg[None, :]).astype(np.float32), dtype=BF16)

    x2 = x.reshape(t, D_MODEL)
    for l in range(depth):
        mod3 = _adaln(c, w_ada[l], b_ada[l]).reshape(batch, 6, D_MODEL)
        gq = (jnp.tile(q_norm_g[l], N_Q_HEADS) * (HEAD_DIM ** -0.5 * _LOG2_E)).reshape(1, ATTN_WIDTH)
        gk = jnp.tile(k_norm_g[l], N_KV_HEADS).reshape(1, KV_WIDTH)
        g_pre1 = mix_pre_g[l].reshape(1, D_MODEL)
        qt, kp, vt, f = _in_proj(
            x2, mod3, g_pre1, w_in[l], gq, gk, cos_t, sin_t, ones_blk, batch, seq,
            tiles["in_proj"])
        o = _attention(qt, kp, vt, batch, seq, tiles["attn"])
        yf = _fourier(f, *dft_tabs, batch, seq)
        x1, h2 = _merge(
            o, yf, x2, mod3, g_pre1, w_in[l, :, _O_GA:].astype(BF16),
            b_gate[l].reshape(1, 2 * D_MODEL),
            w_attn_branch[l], w_fourier_branch[l], w_out[l],
            mix_post_g[l].reshape(1, D_MODEL), ffn_pre_g[l].reshape(1, D_MODEL), seq,
            tiles["merge"])
        glu_half = jnp.concatenate([jnp.ones((D_FF,), F32), jnp.full((D_FF,), 0.5, F32)])
        act = _ffn_up(h2, w_up[l].astype(BF16), conv_w[l] * glu_half,
                      (conv_b[l] * glu_half).reshape(1, 2 * D_FF), seq, tiles["ffn_up"])
        x2 = _ffn_down(act, w_down[l], x1, mod3,
                       ffn_post_g[l].reshape(1, D_MODEL), seq, tiles["ffn_down"])
    return x2.reshape(batch, seq, D_MODEL)
```

```python
import functools
import math

import numpy as np
import jax
import jax.numpy as jnp
from jax import lax
from jax.experimental import pallas as pl
from jax.experimental.pallas import tpu as pltpu

D_MODEL = 1024
GRID_W = 64
HEAD_DIM = 64
N_Q_HEADS = 8
N_KV_HEADS = 2
Q_PER_KV = N_Q_HEADS // N_KV_HEADS
ATTN_WIDTH = N_Q_HEADS * HEAD_DIM
KV_WIDTH = N_KV_HEADS * HEAD_DIM
N_FOURIER_GROUPS = 4
FOURIER_GROUP_DIM = 128
FOURIER_WIDTH = N_FOURIER_GROUPS * FOURIER_GROUP_DIM
IN_WIDTH = ATTN_WIDTH + 2 * KV_WIDTH + FOURIER_WIDTH + 2 * D_MODEL
ROPE_THETA = 10000.0
ROPE_PAIRS_PER_AXIS = HEAD_DIM // 4
D_FF = 2816
CONV_WIDTH = 3
NORM_EPS = 1e-6

LANES = 128
BF16_SUBLANES = 16
VMEM_LIMIT_BYTES = 56 * 1024 * 1024
_ROW_CHUNK = 256
_V_ONES_ROWS = BF16_SUBLANES
_V_ROWS = HEAD_DIM + _V_ONES_ROWS
_TILES = {"in_proj": 1024, "attn": 256, "merge": 1024, "ffn_up": 512, "ffn_down": 1024}

_LOG2_E = math.log2(math.e)

F32 = jnp.float32
BF16 = jnp.bfloat16

_O_K = ATTN_WIDTH
_O_V = _O_K + KV_WIDTH
_O_F = _O_V + KV_WIDTH
_O_GA = _O_F + FOURIER_WIDTH
_O_GF = _O_GA + D_MODEL


def _params(n_grid_dims, flags=None):
    return pltpu.CompilerParams(
        dimension_semantics=("arbitrary",) * n_grid_dims,
        vmem_limit_bytes=VMEM_LIMIT_BYTES,
        flags=flags,
    )


def _dot(a, b):
    return jnp.dot(a, b, preferred_element_type=F32)


def _rms_scale(v):
    return lax.rsqrt(jnp.mean(v * v, axis=-1, keepdims=True) + NORM_EPS)


def _adaln_kernel(c_ref, w_ref, b_ref, o_ref):
    c = c_ref[...]
    a = c / (1.0 + jnp.exp(-c))
    a_hi = a.astype(BF16)
    a_lo = (a - a_hi.astype(F32)).astype(BF16)
    n = a.shape[0]
    res = _dot(jnp.concatenate([a_hi, a_lo], axis=0), w_ref[...].astype(BF16))
    o_ref[...] = res[0:n, :] + res[n:2 * n, :] + b_ref[...]


def _adaln(c, w_ada, b_ada):
    batch = c.shape[0]
    n = w_ada.shape[1]
    tn = 768
    return pl.pallas_call(
        _adaln_kernel,
        grid=(n // tn,),
        in_specs=[
            pl.BlockSpec((batch, D_MODEL), lambda j: (0, 0)),
            pl.BlockSpec((D_MODEL, tn), lambda j: (0, j)),
            pl.BlockSpec((1, tn), lambda j: (0, j)),
        ],
        out_specs=pl.BlockSpec((batch, tn), lambda j: (0, j)),
        out_shape=jax.ShapeDtypeStruct((batch, n), F32),
        compiler_params=_params(1),
        name="adaln",
    )(c, w_ada, b_ada.reshape(1, n))


def _swap_pairs(v):
    n = v.shape[-1]
    lane = lax.broadcasted_iota(jnp.int32, v.shape, v.ndim - 1)
    from_right = pltpu.roll(v, n - 1, v.ndim - 1)
    from_left = pltpu.roll(v, 1, v.ndim - 1)
    return jnp.where((lane & 1) == 0, from_right, from_left)


def _head_norm_rope(z, gain, ones_blk, cos, sin):
    reps = z.shape[-1] // LANES
    ssq = _dot((z * z).astype(BF16), ones_blk)
    r = lax.rsqrt(ssq * (1.0 / HEAD_DIM) + NORM_EPS)
    zg = z * gain
    cos_t = jnp.concatenate([cos] * reps, axis=-1) if reps > 1 else cos
    sin_t = jnp.concatenate([sin] * reps, axis=-1) if reps > 1 else sin
    return (zg * cos_t + _swap_pairs(zg) * sin_t) * r


def _cast_once(src_ref, dst_ref):
    @pl.when(pl.program_id(0) == 0)
    def _():
        dst_ref[...] = src_ref[...].astype(BF16)


def _in_proj_kernel(x_ref, mod_ref, g_ref, w32_ref, gq_ref, gk_ref,
                    cos_ref, sin_ref, ones_ref,
                    qt_ref, kp_ref, vt_ref, f_ref, w_ref):
    _cast_once(w32_ref, w_ref)
    shift = mod_ref[0:1, :]
    scale = mod_ref[1:2, :]
    gmod = g_ref[...] * (1.0 + scale)
    for r in range(x_ref.shape[0] // _ROW_CHUNK):
        rows = slice(r * _ROW_CHUNK, (r + 1) * _ROW_CHUNK)
        x = x_ref[rows, :]
        hb = ((x * _rms_scale(x)) * gmod + shift).astype(BF16)
        cos = cos_ref[rows, :]
        sin = sin_ref[rows, :]

        zq = _dot(hb, w_ref[:, 0:_O_K])
        q = _head_norm_rope(zq, gq_ref[...], ones_ref[...], cos, sin)
        qt_ref[:, rows] = q.T.astype(BF16)

        zkv = _dot(hb, w_ref[:, _O_K:_O_F])
        k = _head_norm_rope(zkv[:, 0:KV_WIDTH], gk_ref[...],
                            ones_ref[0:KV_WIDTH, 0:KV_WIDTH], cos, sin)
        k_sw = pltpu.roll(k, HEAD_DIM, 1)
        first = lax.broadcasted_iota(jnp.int32, k.shape, 1) < HEAD_DIM
        kp_ref[rows, :] = jnp.concatenate(
            [jnp.where(first, k, 0.0), jnp.where(first, 0.0, k_sw),
             jnp.where(first, k_sw, 0.0), jnp.where(first, 0.0, k)], axis=-1).astype(BF16)
        zvt = zkv[:, KV_WIDTH:2 * KV_WIDTH].T
        ones_rows = jnp.ones((_V_ONES_ROWS, zvt.shape[1]), F32)
        vt_ref[:, rows] = jnp.concatenate(
            [piece for kv in range(N_KV_HEADS)
             for piece in (zvt[kv * HEAD_DIM:(kv + 1) * HEAD_DIM, :], ones_rows)],
            axis=0).astype(BF16)

        f_ref[rows, :] = _dot(hb, w_ref[:, _O_F:_O_GA]).astype(BF16)


def _in_proj(x2, mod3, g_pre, w_in, gq, gk, cos_t, sin_t, ones_blk, batch, seq, tm):
    t = batch * seq
    per_b = seq // tm
    row = lambda i: (i, 0)
    const = lambda i: (0, 0)
    return pl.pallas_call(
        _in_proj_kernel,
        grid=(t // tm,),
        in_specs=[
            pl.BlockSpec((tm, D_MODEL), row),
            pl.BlockSpec((None, 6, D_MODEL), lambda i: (i // per_b, 0, 0)),
            pl.BlockSpec((1, D_MODEL), const),
            pl.BlockSpec((D_MODEL, _O_GA), const),
            pl.BlockSpec((1, ATTN_WIDTH), const),
            pl.BlockSpec((1, KV_WIDTH), const),
            pl.BlockSpec((tm, LANES), lambda i: (i % per_b, 0)),
            pl.BlockSpec((tm, LANES), lambda i: (i % per_b, 0)),
            pl.BlockSpec((ATTN_WIDTH, ATTN_WIDTH), const),
        ],
        out_specs=[
            pl.BlockSpec((None, ATTN_WIDTH, tm), lambda i: (i // per_b, 0, i % per_b)),
            pl.BlockSpec((tm, 2 * N_KV_HEADS * LANES), row),
            pl.BlockSpec((None, N_KV_HEADS * _V_ROWS, tm), lambda i: (i // per_b, 0, i % per_b)),
            pl.BlockSpec((tm, FOURIER_WIDTH), row),
        ],
        out_shape=[
            jax.ShapeDtypeStruct((batch, ATTN_WIDTH, seq), BF16),
            jax.ShapeDtypeStruct((t, 2 * N_KV_HEADS * LANES), BF16),
            jax.ShapeDtypeStruct((batch, N_KV_HEADS * _V_ROWS, seq), BF16),
            jax.ShapeDtypeStruct((t, FOURIER_WIDTH), BF16),
        ],
        scratch_shapes=[pltpu.VMEM((D_MODEL, _O_GA), BF16)],
        compiler_params=_params(1),
        name="in_proj",
    )(x2, mod3, g_pre, w_in, gq, gk, cos_t, sin_t, ones_blk)


_ATTN_SLOTS = 2


def _attn_step(qt_ref, kp_ref, vt_ref, o_ref, p_ref, s_new_ref, s_old_ref):
    outs = []
    for h in range(N_Q_HEADS):
        kv = h // Q_PER_KV
        slot = h % _ATTN_SLOTS
        pair = h // 2
        variant = 2 * kv + h % 2
        s_new_ref[h] = _dot(kp_ref[:, variant * LANES:(variant + 1) * LANES],
                            qt_ref[pair * LANES:(pair + 1) * LANES, :])
        s = s_old_ref[h]
        m = jnp.max(s, axis=0, keepdims=True)
        p_ref[slot] = jnp.exp2(s - m).astype(BF16)
        ov = _dot(vt_ref[kv * _V_ROWS:(kv + 1) * _V_ROWS, :], p_ref[slot])
        on = ov[0:HEAD_DIM, :] / ov[HEAD_DIM:HEAD_DIM + 1, :]
        outs.append(on.T)
    o_ref[...] = jnp.concatenate(outs, axis=-1).astype(BF16)


def _attn_kernel(qt_ref, kp_ref, vt_ref, o_ref, p_ref, sa_ref, sb_ref):
    g = pl.program_id(0)

    @pl.when(g == 0)
    def _():
        sb_ref[...] = jnp.zeros(sb_ref.shape, F32)

    @pl.when(g % 2 == 0)
    def _():
        _attn_step(qt_ref, kp_ref, vt_ref, o_ref, p_ref, sa_ref, sb_ref)

    @pl.when(g % 2 == 1)
    def _():
        _attn_step(qt_ref, kp_ref, vt_ref, o_ref, p_ref, sb_ref, sa_ref)


def _attention(qt, kp, vt, batch, seq, tq):
    t = batch * seq
    per_b = seq // tq
    n_blk = t // tq
    cur = lambda g: jnp.minimum(g, n_blk - 1)
    prev = lambda g: jnp.maximum(g - 1, 0)
    return pl.pallas_call(
        _attn_kernel,
        grid=(n_blk + 1,),
        in_specs=[
            pl.BlockSpec((None, ATTN_WIDTH, tq), lambda g: (cur(g) // per_b, 0, cur(g) % per_b)),
            pl.BlockSpec((seq, 2 * N_KV_HEADS * LANES), lambda g: (cur(g) // per_b, 0)),
            pl.BlockSpec((None, N_KV_HEADS * _V_ROWS, seq), lambda g: (prev(g) // per_b, 0, 0)),
        ],
        out_specs=pl.BlockSpec((tq, ATTN_WIDTH), lambda g: (prev(g), 0)),
        out_shape=jax.ShapeDtypeStruct((t, ATTN_WIDTH), BF16),
        scratch_shapes=[pltpu.VMEM((_ATTN_SLOTS, seq, tq), BF16),
                        pltpu.VMEM((N_Q_HEADS, seq, tq), F32),
                        pltpu.VMEM((N_Q_HEADS, seq, tq), F32)],
        compiler_params=_params(1),
        name="attn",
    )(qt, kp, vt)


_DFT_SPLIT = 32
_REV_BLOCK = 256


def _fourier_kernel(f_ref, cs_ref, jw_ref, ca_ref, sa_ref, cb_ref, sb_ref, o_ref,
                    w_ref, x_ref, r_ref, d_ref):
    seq = f_ref.shape[0]
    half = seq // 2
    gd = FOURIER_GROUP_DIM
    fw = FOURIER_WIDTH

    @pl.when(pl.program_id(0) == 0)
    def _():
        cb = cb_ref[...]
        sb = sb_ref[...]

        def body(s1, carry):
            ca = ca_ref[pl.ds(s1, 1), :]
            sa = sa_ref[pl.ds(s1, 1), :]
            rows = pl.ds(pl.multiple_of(s1 * _DFT_SPLIT, _DFT_SPLIT), _DFT_SPLIT)
            w_ref[rows, 0:half] = (ca * cb - sa * sb).astype(BF16)
            w_ref[rows, half:seq] = (-(sa * cb + ca * sb)).astype(BF16)
            return carry

        lax.fori_loop(0, w_ref.shape[0] // _DFT_SPLIT, body, 0)

    cs = cs_ref[...].astype(BF16)
    for g in range(N_FOURIER_GROUPS):
        xcs = _dot(f_ref[:, g * gd:(g + 1) * gd], cs)
        x_ref[:, g * gd:(g + 1) * gd] = xcs[:, 0:gd].astype(BF16)
        x_ref[:, fw + g * gd:fw + (g + 1) * gd] = xcs[:, gd:2 * gd].astype(BF16)

    n_blk = seq // _REV_BLOCK
    jw = jw_ref[...].astype(BF16)
    for a in range(half // _REV_BLOCK):
        beta = n_blk - 1 - a
        nxt = (beta + 1) % n_blk
        win = jnp.concatenate([x_ref[beta * _REV_BLOCK:(beta + 1) * _REV_BLOCK, :],
                               x_ref[nxt * _REV_BLOCK:(nxt + 1) * _REV_BLOCK, :]], axis=0)
        rev = _dot(jw, win)
        top = x_ref[a * _REV_BLOCK:(a + 1) * _REV_BLOCK, :].astype(F32)
        rows = slice(a * _REV_BLOCK, (a + 1) * _REV_BLOCK)
        r_ref[rows, :] = (top[:, 0:fw] + rev[:, 0:fw]).astype(BF16)
        r_ref[half + a * _REV_BLOCK:half + (a + 1) * _REV_BLOCK, :] = (
            top[:, fw:2 * fw] - rev[:, fw:2 * fw]).astype(BF16)

    m_rows = d_ref.shape[0] - _REV_BLOCK + BF16_SUBLANES
    x_mid = x_ref[half:half + 1, 0:fw].astype(F32)
    d_ref[m_rows:d_ref.shape[0], :] = jnp.zeros((d_ref.shape[0] - m_rows, fw), BF16)
    split = half // 2
    for lo_r, hi_r in ((0, split), (split, m_rows)):
        n = hi_r - lo_r
        odd = (lax.broadcasted_iota(jnp.int32, (n, fw), 0) & 1) == 1
        a_part = _dot(w_ref[lo_r:hi_r, 0:half], r_ref[0:half, :]) + jnp.where(odd, -x_mid, x_mid)
        b_part = _dot(w_ref[lo_r:hi_r, half:seq], r_ref[half:seq, :])
        keep = min(hi_r, half) - lo_r
        o_ref[lo_r:lo_r + keep, :] = (a_part + b_part)[0:keep, :].astype(BF16)
        d_ref[lo_r:hi_r, :] = (a_part - b_part).astype(BF16)
    for a in range(half // _REV_BLOCK):
        beta = half // _REV_BLOCK - 1 - a
        win = d_ref[beta * _REV_BLOCK:(beta + 2) * _REV_BLOCK, :]
        o_ref[half + a * _REV_BLOCK:half + (a + 1) * _REV_BLOCK, :] = _dot(jw, win).astype(BF16)


def _fourier(f, cs_c, jw, ca, sa, cb, sb, batch, seq):
    t = batch * seq
    const = lambda b: (0, 0)
    return pl.pallas_call(
        _fourier_kernel,
        grid=(batch,),
        in_specs=[
            pl.BlockSpec((seq, FOURIER_WIDTH), lambda b: (b, 0)),
            pl.BlockSpec(cs_c.shape, const),
            pl.BlockSpec(jw.shape, const),
            pl.BlockSpec(ca.shape, const),
            pl.BlockSpec(sa.shape, const),
            pl.BlockSpec(cb.shape, const),
            pl.BlockSpec(sb.shape, const),
        ],
        out_specs=pl.BlockSpec((seq, FOURIER_WIDTH), lambda b: (b, 0)),
        out_shape=jax.ShapeDtypeStruct((t, FOURIER_WIDTH), BF16),
        scratch_shapes=[
            pltpu.VMEM(((seq // 2 // _DFT_SPLIT + 1) * _DFT_SPLIT, seq), BF16),
            pltpu.VMEM((seq, 2 * FOURIER_WIDTH), BF16),
            pltpu.VMEM((seq, FOURIER_WIDTH), BF16),
            pltpu.VMEM((seq // 2 + _REV_BLOCK, FOURIER_WIDTH), BF16)],
        compiler_params=_params(1),
        name="fourier",
    )(f, cs_c, jw, ca, sa, cb, sb)


def _merge_kernel(o_ref, yf_ref, x_ref, mod_ref, gpre1_ref, wg_ref, bg_ref,
                  wa32_ref, wf32_ref, wo32_ref, gpost_ref, gpre2_ref,
                  x1_ref, h2_ref, wa_ref, wf_ref, wo_ref):
    _cast_once(wa32_ref, wa_ref)
    _cast_once(wf32_ref, wf_ref)
    _cast_once(wo32_ref, wo_ref)
    shift1 = mod_ref[0:1, :]
    scale1 = mod_ref[1:2, :]
    gate1 = mod_ref[2:3, :]
    shift2 = mod_ref[3:4, :]
    scale2 = mod_ref[4:5, :]
    gmod1 = gpre1_ref[...] * (1.0 + scale1)
    gpost = gate1 * gpost_ref[...]
    gmod2 = gpre2_ref[...] * (1.0 + scale2)
    for r in range(x_ref.shape[0] // _ROW_CHUNK):
        rows = slice(r * _ROW_CHUNK, (r + 1) * _ROW_CHUNK)
        x = x_ref[rows, :]
        hb = ((x * _rms_scale(x)) * gmod1 + shift1).astype(BF16)
        ta = _dot(hb, wg_ref[:, 0:D_MODEL]) + bg_ref[:, 0:D_MODEL]
        ga = 0.5 * jnp.tanh(0.5 * ta) + 0.5
        tf = _dot(hb, wg_ref[:, D_MODEL:2 * D_MODEL]) + bg_ref[:, D_MODEL:2 * D_MODEL]
        gf = 0.5 * jnp.tanh(0.5 * tf) + 0.5
        ya = _dot(o_ref[rows, :], wa_ref[...])
        yf = _dot(yf_ref[rows, :], wf_ref[...])
        y = (ga * ya + gf * yf).astype(BF16)
        y2 = _dot(y, wo_ref[...])
        x1 = x + (y2 * _rms_scale(y2)) * gpost
        x1_ref[rows, :] = x1
        h2_ref[rows, :] = ((x1 * _rms_scale(x1)) * gmod2 + shift2).astype(BF16)


def _merge(o, yf, x2, mod3, g_pre1, w_gate, b_gate, wa, wf, wo, g_post, g_pre2, seq, tm):
    t = x2.shape[0]
    per_b = seq // tm
    row = lambda i: (i, 0)
    const = lambda i: (0, 0)
    return pl.pallas_call(
        _merge_kernel,
        grid=(t // tm,),
        in_specs=[
            pl.BlockSpec((tm, ATTN_WIDTH), row),
            pl.BlockSpec((tm, FOURIER_WIDTH), row),
            pl.BlockSpec((tm, D_MODEL), row),
            pl.BlockSpec((None, 6, D_MODEL), lambda i: (i // per_b, 0, 0)),
            pl.BlockSpec((1, D_MODEL), const),
            pl.BlockSpec((D_MODEL, 2 * D_MODEL), const),
            pl.BlockSpec((1, 2 * D_MODEL), const),
            pl.BlockSpec((ATTN_WIDTH, D_MODEL), const),
            pl.BlockSpec((FOURIER_WIDTH, D_MODEL), const),
            pl.BlockSpec((D_MODEL, D_MODEL), const),
            pl.BlockSpec((1, D_MODEL), const),
            pl.BlockSpec((1, D_MODEL), const),
        ],
        out_specs=[pl.BlockSpec((tm, D_MODEL), row), pl.BlockSpec((tm, D_MODEL), row)],
        out_shape=[jax.ShapeDtypeStruct((t, D_MODEL), F32),
                   jax.ShapeDtypeStruct((t, D_MODEL), BF16)],
        scratch_shapes=[pltpu.VMEM((ATTN_WIDTH, D_MODEL), BF16),
                        pltpu.VMEM((FOURIER_WIDTH, D_MODEL), BF16),
                        pltpu.VMEM((D_MODEL, D_MODEL), BF16)],
        compiler_params=_params(1),
        name="merge",
    )(o, yf, x2, mod3, g_pre1, w_gate, b_gate, wa, wf, wo, g_post, g_pre2)


_HALO_BLOCK = BF16_SUBLANES
_HALO = 8
_MXU_COLS = 256
_GELU_K1 = math.sqrt(2.0 / math.pi)
_GELU_K2 = _GELU_K1 * 0.044715


def _ffn_up_proj(h_ref, hp_ref, hn_ref, w_ref, hx_ref, hperm_ref, tiles_per_seq):
    tm = h_ref.shape[0]
    ext = tm + 2 * _HALO
    pitch = ext // 8
    i = pl.program_id(0)
    keep_prev = jnp.where(i % tiles_per_seq != 0, 1.0, 0.0)
    keep_next = jnp.where(i % tiles_per_seq != tiles_per_seq - 1, 1.0, 0.0)

    for c in range(D_MODEL // LANES):
        lc = slice(c * LANES, (c + 1) * LANES)
        hx_ref[c, 0:_HALO, :] = (
            hp_ref[:, lc].astype(F32)[_HALO_BLOCK - _HALO:_HALO_BLOCK, :] * keep_prev)
        hx_ref[c, _HALO:_HALO + tm, :] = h_ref[:, lc].astype(F32)
        hx_ref[c, _HALO + tm:ext, :] = hn_ref[:, lc].astype(F32)[0:_HALO, :] * keep_next
    for c in range(D_MODEL // LANES):
        lc = slice(c * LANES, (c + 1) * LANES)
        for b in range(0, pitch, 2):
            pair = jnp.concatenate([hx_ref[c, pl.ds(b, 8, stride=pitch), :],
                                    hx_ref[c, pl.ds(b + 1, 8, stride=pitch), :]], axis=0)
            hperm_ref[8 * b:8 * b + 16, lc] = pair.astype(BF16)
    hperm = hperm_ref[...]

    def pair_thunk(jj):
        def run(u_ref):
            for half in range(2):
                cols = slice(half * D_FF + jj * _MXU_COLS, half * D_FF + (jj + 1) * _MXU_COLS)
                u_ref[:, cols] = _dot(hperm, w_ref[:, cols])
        return run

    return [pair_thunk(jj) for jj in range(D_FF // _MXU_COLS)]


def _ffn_up_conv(u_ref, cw_ref, cb_ref, o_ref, st_ref, jj):
    tm = o_ref.shape[0]
    ext = tm + 2 * _HALO
    pitch = ext // 8
    for k in range(_MXU_COLS // LANES):
        j = 2 * jj + k
        la = slice(j * LANES, (j + 1) * LANES)
        lb = slice(D_FF + j * LANES, D_FF + (j + 1) * LANES)
        bc = lambda ref, r, l: jnp.broadcast_to(ref[r:r + 1, l], (8, LANES))
        wa = [bc(cw_ref, r, la) for r in range(CONV_WIDTH)]
        wb = [bc(cw_ref, r, lb) for r in range(CONV_WIDTH)]
        ba = bc(cb_ref, 0, la)
        bb = bc(cb_ref, 0, lb)

        def vreg(lanes, b):
            if b < 0:
                return pltpu.roll(u_ref[8 * (pitch - 1):8 * pitch, lanes], 1, 0)
            if b >= pitch:
                return pltpu.roll(u_ref[0:8, lanes], 7, 0)
            return u_ref[8 * b:8 * b + 8, lanes]

        for b in range(pitch):
            ca = vreg(la, b - 1) * wa[0] + vreg(la, b) * wa[1] + vreg(la, b + 1) * wa[2] + ba
            cv = vreg(lb, b - 1) * wb[0] + vreg(lb, b) * wb[1] + vreg(lb, b + 1) * wb[2] + bb
            t = jnp.tanh(ca * (_GELU_K1 + _GELU_K2 * (ca * ca)))
            st_ref[j % 2, pl.ds(b, 8, stride=pitch), :] = (ca * cv) * (1.0 + t)
        o_ref[:, la] = st_ref[j % 2, _HALO:_HALO + tm, :].astype(BF16)


def _ffn_up_kernel(h_ref, hp_ref, hn_ref, w_ref, cw_ref, cb_ref, o_ref,
                   hx_ref, hperm_ref, st_ref, u0_ref, u1_ref, *, tiles_per_seq, n_tiles):
    i = pl.program_id(0)
    u_refs = (u0_ref, u1_ref)
    n_pairs = D_FF // _MXU_COLS

    @pl.when(i == 0)
    def _():
        for thunk in _ffn_up_proj(h_ref, hp_ref, hn_ref, w_ref, hx_ref, hperm_ref, tiles_per_seq):
            thunk(u0_ref)

    for par in range(2):
        @pl.when((i > 0) & (i < n_tiles) & (i % 2 == par))
        def _(par=par):
            thunks = _ffn_up_proj(h_ref, hp_ref, hn_ref, w_ref, hx_ref, hperm_ref, tiles_per_seq)
            for jj in range(n_pairs):
                thunks[jj](u_refs[par])
                _ffn_up_conv(u_refs[1 - par], cw_ref, cb_ref, o_ref, st_ref, jj)

    @pl.when(i == n_tiles)
    def _():
        for jj in range(n_pairs):
            _ffn_up_conv(u_refs[(n_tiles - 1) % 2], cw_ref, cb_ref, o_ref, st_ref, jj)


def _ffn_up(h2, w_up, conv_w, conv_b, seq, tm):
    t = h2.shape[0]
    n_tiles = t // tm
    tiles_per_seq = seq // tm
    halo_per_tile = tm // _HALO_BLOCK
    n_halo = t // _HALO_BLOCK
    ext = tm + 2 * _HALO
    pitch = ext // 8
    assert ext % BF16_SUBLANES == 0 and pitch % 2 == 0 and pitch % 8 != 0, (tm, ext)
    const = lambda i: (0, 0)
    cur = lambda i: jnp.minimum(i, n_tiles - 1)
    return pl.pallas_call(
        functools.partial(_ffn_up_kernel, tiles_per_seq=tiles_per_seq, n_tiles=n_tiles),
        grid=(n_tiles + 1,),
        in_specs=[
            pl.BlockSpec((tm, D_MODEL), lambda i: (cur(i), 0)),
            pl.BlockSpec((_HALO_BLOCK, D_MODEL),
                         lambda i: (jnp.maximum(cur(i) * halo_per_tile - 1, 0), 0)),
            pl.BlockSpec((_HALO_BLOCK, D_MODEL),
                         lambda i: (jnp.minimum((cur(i) + 1) * halo_per_tile, n_halo - 1), 0)),
            pl.BlockSpec((D_MODEL, 2 * D_FF), const),
            pl.BlockSpec((CONV_WIDTH, 2 * D_FF), const),
            pl.BlockSpec((1, 2 * D_FF), const),
        ],
        out_specs=pl.BlockSpec((tm, D_FF), lambda i: (jnp.maximum(i - 1, 0), 0)),
        out_shape=jax.ShapeDtypeStruct((t, D_FF), BF16),
        scratch_shapes=[pltpu.VMEM((D_MODEL // LANES, ext, LANES), F32),
                        pltpu.VMEM((ext, D_MODEL), BF16),
                        pltpu.VMEM((2, ext, LANES), F32),
                        pltpu.VMEM((ext, 2 * D_FF), F32),
                        pltpu.VMEM((ext, 2 * D_FF), F32)],
        compiler_params=_params(1),
        name="ffn_up",
    )(h2, h2, h2, w_up, conv_w, conv_b)


def _ffn_down_kernel(a_ref, w32_ref, x1_ref, mod_ref, g_ref, o_ref, w_ref):
    _cast_once(w32_ref, w_ref)
    gpost = mod_ref[5:6, :] * g_ref[...]
    for r in range(a_ref.shape[0] // _ROW_CHUNK):
        rows = slice(r * _ROW_CHUNK, (r + 1) * _ROW_CHUNK)
        y = _dot(a_ref[rows, :], w_ref[...])
        o_ref[rows, :] = x1_ref[rows, :] + (y * _rms_scale(y)) * gpost


def _ffn_down(act, w_down, x1, mod3, g_post, seq, tm):
    t = x1.shape[0]
    per_b = seq // tm
    row = lambda i: (i, 0)
    const = lambda i: (0, 0)
    return pl.pallas_call(
        _ffn_down_kernel,
        grid=(t // tm,),
        in_specs=[
            pl.BlockSpec((tm, D_FF), row),
            pl.BlockSpec((D_FF, D_MODEL), const),
            pl.BlockSpec((tm, D_MODEL), row),
            pl.BlockSpec((None, 6, D_MODEL), lambda i: (i // per_b, 0, 0)),
            pl.BlockSpec((1, D_MODEL), const),
        ],
        out_specs=pl.BlockSpec((tm, D_MODEL), row),
        out_shape=jax.ShapeDtypeStruct((t, D_MODEL), F32),
        scratch_shapes=[pltpu.VMEM((D_FF, D_MODEL), BF16)],
        compiler_params=_params(1),
        name="ffn_down",
    )(act, w_down, x1, mod3, g_post)


@functools.lru_cache(maxsize=None)
def _rope_tables(seq):
    pos = np.arange(seq)
    row = (pos // GRID_W).astype(np.float64)
    col = (pos % GRID_W).astype(np.float64)
    inv = ROPE_THETA ** (-np.arange(ROPE_PAIRS_PER_AXIS, dtype=np.float64) / ROPE_PAIRS_PER_AXIS)
    ang = np.concatenate([row[:, None] * inv, col[:, None] * inv], axis=-1)
    cos = np.repeat(np.cos(ang), 2, axis=-1)
    sin = np.repeat(np.sin(ang), 2, axis=-1)
    sign = np.tile(np.array([-1.0, 1.0]), HEAD_DIM // 2)
    reps = LANES // HEAD_DIM
    return (np.tile(cos, (1, reps)).astype(np.float32),
            np.tile(sin * sign, (1, reps)).astype(np.float32))


@functools.lru_cache(maxsize=None)
def _dft_tables(seq):
    gd = FOURIER_GROUP_DIM
    kc = (np.outer(np.arange(gd), np.arange(gd)) % gd).astype(np.float64) * (2.0 * np.pi / gd)
    norm = 1.0 / math.sqrt(seq * gd)
    cs_c = np.concatenate([np.cos(kc), np.sin(kc)], axis=1) * norm
    n1 = seq // _DFT_SPLIT
    t_idx = np.arange(seq // 2)
    ang_a = (np.outer(np.arange(n1), t_idx) % n1).astype(np.float64) * (2.0 * np.pi / n1)
    ang_b = (np.outer(np.arange(_DFT_SPLIT), t_idx) % seq).astype(np.float64) * (2.0 * np.pi / seq)
    cos_b = np.cos(ang_b)
    cos_b[:, 0] *= 0.5
    jw = np.zeros((_REV_BLOCK, 2 * _REV_BLOCK))
    jw[np.arange(_REV_BLOCK), _REV_BLOCK - np.arange(_REV_BLOCK)] = 1.0
    f32 = lambda a: a.astype(np.float32)
    return (f32(cs_c), f32(jw), f32(np.cos(ang_a)), f32(np.sin(ang_a)), f32(cos_b),
            f32(np.sin(ang_b)))


def kernel(x, c, w_ada, b_ada, mix_pre_g, w_in, q_norm_g, k_norm_g, b_gate, w_attn_branch,
           w_fourier_branch, w_out, mix_post_g, ffn_pre_g, w_up, conv_w, conv_b, w_down,
           ffn_post_g):
    batch, seq, _ = x.shape
    depth = w_ada.shape[0]
    t = batch * seq
    tiles = _TILES

    cos_np, sin_np = _rope_tables(seq)
    cos_t = jnp.asarray(cos_np)
    sin_t = jnp.asarray(sin_np)
    dft_tabs = [jnp.asarray(a) for a in _dft_tables(seq)]
    seg = np.arange(ATTN_WIDTH) // HEAD_DIM
    ones_blk = jnp.asarray((seg[:, None] == seg[None, :]).astype(np.float32), dtype=BF16)

    x2 = x.reshape(t, D_MODEL)
    for l in range(depth):
        mod3 = _adaln(c, w_ada[l], b_ada[l]).reshape(batch, 6, D_MODEL)
        gq = (jnp.tile(q_norm_g[l], N_Q_HEADS) * (HEAD_DIM ** -0.5 * _LOG2_E)).reshape(1, ATTN_WIDTH)
        gk = jnp.tile(k_norm_g[l], N_KV_HEADS).reshape(1, KV_WIDTH)
        g_pre1 = mix_pre_g[l].reshape(1, D_MODEL)
        qt, kp, vt, f = _in_proj(
            x2, mod3, g_pre1, w_in[l], gq, gk, cos_t, sin_t, ones_blk, batch, seq,
            tiles["in_proj"])
        o = _attention(qt, kp, vt, batch, seq, tiles["attn"])
        yf = _fourier(f, *dft_tabs, batch, seq)
        x1, h2 = _merge(
            o, yf, x2, mod3, g_pre1, w_in[l, :, _O_GA:].astype(BF16),
            b_gate[l].reshape(1, 2 * D_MODEL),
            w_attn_branch[l], w_fourier_branch[l], w_out[l],
            mix_post_g[l].reshape(1, D_MODEL), ffn_pre_g[l].reshape(1, D_MODEL), seq,
            tiles["merge"])
        glu_half = jnp.concatenate([jnp.ones((D_FF,), F32), jnp.full((D_FF,), 0.5, F32)])
        act = _ffn_up(h2, w_up[l].astype(BF16), conv_w[l] * glu_half,
                      (conv_b[l] * glu_half).reshape(1, 2 * D_FF), seq, tiles["ffn_up"])
        x2 = _ffn_down(act, w_down[l], x1, mod3,
                       ffn_post_g[l].reshape(1, D_MODEL), seq, tiles["ffn_down"])
    return x2.reshape(batch, seq, D_MODEL)
```
